```python
import math
import jax, jax.numpy as jnp
from jax import lax
import numpy as np

D_MODEL = 1024
BATCH = 8
SEQ = 4096
DEPTH = 1
DEC_BATCH = 32
DEC_SEQ = 16
PAST_LEN = 4096

CHUNK = 64
Q_BLOCK = 128
N_HEADS = 4
HEAD_DIM = 64
V_DIM = 2 * HEAD_DIM
ATTN_WIDTH = N_HEADS * V_DIM
CONV_WIDTH = D_MODEL - ATTN_WIDTH
MIX_WIDTH = ATTN_WIDTH + CONV_WIDTH
CONV_K = 31
QK_COLS = N_HEADS * 2 * HEAD_DIM
IN_COLS = 2 * QK_COLS + ATTN_WIDTH + 2 * CONV_WIDTH
D_FF = 4 * D_MODEL
N_MEM = 256
MEM_HEADS = 4
MEM_HEAD_DIM = 128
MEM_WIDTH = MEM_HEADS * MEM_HEAD_DIM
N_BUCKETS = 32
MAX_DISTANCE = 128
EPS = 1e-6
NEG_INF = -1e30

kernel_name = 'hybrid_diffattn_conformer_stream_step'


def rmsnorm(x, g):
    xf = x.astype(jnp.float32)
    y = xf * lax.rsqrt(jnp.mean(xf * xf, axis=-1, keepdims=True) + EPS)
    return (y * g.astype(jnp.float32)).astype(x.dtype)


def layernorm(x, g, b):
    xf = x.astype(jnp.float32)
    mu = jnp.mean(xf, axis=-1, keepdims=True)
    xc = xf - mu
    y = xc * lax.rsqrt(jnp.mean(xc * xc, axis=-1, keepdims=True) + EPS)
    return (y * g.astype(jnp.float32) + b.astype(jnp.float32)).astype(x.dtype)


def rel_bucket(rel):
    half = N_BUCKETS // 2
    max_exact = half // 2
    ret = jnp.where(rel > 0, half, 0)
    n = jnp.abs(rel)
    nf = jnp.maximum(n, 1).astype(jnp.float32)
    large = max_exact + (jnp.log(nf / max_exact) / math.log(MAX_DISTANCE / max_exact)
                         * (half - max_exact)).astype(jnp.int32)
    large = jnp.minimum(large, half - 1)
    return ret + jnp.where(n < max_exact, n, large)


def rel_bias(q_pos, k_pos, table):
    b = rel_bucket(k_pos[None, :] - q_pos[:, None])
    return jnp.transpose(table[b], (2, 0, 1)).astype(jnp.float32)


def chunk_mask(q_pos, k_pos):
    return (k_pos[None, :] // CHUNK) <= (q_pos[:, None] // CHUNK)


def diff_attend(q, k, v, bias, mask, lam):
    s = jnp.einsum('bqhmd,bkhmd->bhmqk', q, k).astype(jnp.float32) * (HEAD_DIM ** -0.5)
    s = jnp.where(mask, s + bias[None, :, None], NEG_INF)
    p = jax.nn.softmax(s, axis=-1)
    w = p[:, :, 0] - lam * p[:, :, 1]
    return jnp.einsum('bhqk,bkhe->bqhe', w.astype(v.dtype), v)


def depthwise_causal_conv(c_hist, w_conv, b_conv):
    out = lax.conv_general_dilated(
        c_hist, w_conv[:, None, :].astype(c_hist.dtype), window_strides=(1,), padding='VALID',
        dimension_numbers=('NWC', 'WIO', 'NWC'), feature_group_count=CONV_WIDTH)
    return out + b_conv


def memory_kv(mem, g_mem, w_mk, w_mv, g_mk):
    B, M, _ = mem.shape
    m = rmsnorm(mem, g_mem)
    mk = rmsnorm((m @ w_mk).reshape(B, M, MEM_HEADS, MEM_HEAD_DIM), g_mk)
    mv = (m @ w_mv).reshape(B, M, MEM_HEADS, MEM_HEAD_DIM)
    return mk, mv


def cross_attend(h, mk, mv, w_mq, g_mq, w_mo):
    B, T, _ = h.shape
    q = rmsnorm((h @ w_mq).reshape(B, T, MEM_HEADS, MEM_HEAD_DIM), g_mq)
    s = jnp.einsum('bqhd,bkhd->bhqk', q, mk.astype(q.dtype)).astype(jnp.float32) * (MEM_HEAD_DIM ** -0.5)
    p = jax.nn.softmax(s, axis=-1)
    o = jnp.einsum('bhqk,bkhd->bqhd', p.astype(h.dtype), mv.astype(h.dtype)).reshape(B, T, MEM_WIDTH)
    return o @ w_mo


def trunk_layer(x, k_past, v_past, c_past, mk, mv, rel_table, g_mix, w_in, g_q, g_k, lam_vec,
                g_sub, w_conv, b_conv, ln_g, ln_b, w_out, g_cross, w_mq, g_mq, w_mo,
                g_ffn, w_ff1, w_ff2, lam_init):
    B, T, _ = x.shape
    P = k_past.shape[1]
    h = rmsnorm(x, g_mix)
    z = h @ w_in
    zq, zk, zv, zu = jnp.split(z, [QK_COLS, 2 * QK_COLS, 2 * QK_COLS + ATTN_WIDTH], axis=-1)
    q = rmsnorm(zq.reshape(B, T, N_HEADS, 2, HEAD_DIM), g_q)
    k = rmsnorm(zk.reshape(B, T, N_HEADS, 2, HEAD_DIM), g_k)
    v = zv.reshape(B, T, N_HEADS, V_DIM)
    k_all = jnp.concatenate([k_past.astype(k.dtype), k], axis=1)
    v_all = jnp.concatenate([v_past.astype(v.dtype), v], axis=1)
    lp = lam_vec.astype(jnp.float32)
    lam = jnp.exp(jnp.sum(lp[0] * lp[1])) - jnp.exp(jnp.sum(lp[2] * lp[3])) + lam_init
    k_pos = jnp.arange(P + T, dtype=jnp.int32)
    if T > Q_BLOCK and T % Q_BLOCK == 0:
        nb = T // Q_BLOCK
        q_blocks = jnp.moveaxis(q.reshape(B, nb, Q_BLOCK, N_HEADS, 2, HEAD_DIM), 1, 0)

        def attend_block(args):
            i, q_blk = args
            q_pos = P + i * Q_BLOCK + jnp.arange(Q_BLOCK, dtype=jnp.int32)
            return diff_attend(q_blk, k_all, v_all, rel_bias(q_pos, k_pos, rel_table),
                               chunk_mask(q_pos, k_pos), lam)

        o = lax.map(attend_block, (jnp.arange(nb, dtype=jnp.int32), q_blocks))
        o = jnp.moveaxis(o, 0, 1).reshape(B, T, N_HEADS, V_DIM)
    else:
        q_pos = P + jnp.arange(T, dtype=jnp.int32)
        o = diff_attend(q, k_all, v_all, rel_bias(q_pos, k_pos, rel_table),
                        chunk_mask(q_pos, k_pos), lam)
    o = (rmsnorm(o, g_sub) * (1.0 - lam_init)).reshape(B, T, ATTN_WIDTH)
    a, gate = jnp.split(zu, 2, axis=-1)
    c = a * jax.nn.sigmoid(gate)
    c_hist = jnp.concatenate([c_past.astype(c.dtype), c], axis=1)
    cv = jax.nn.silu(layernorm(depthwise_causal_conv(c_hist, w_conv, b_conv), ln_g, ln_b))
    x = x + jnp.concatenate([o, cv], axis=-1) @ w_out
    x = x + cross_attend(rmsnorm(x, g_cross), mk, mv, w_mq, g_mq, w_mo)
    hf = rmsnorm(x, g_ffn)
    x = x + jnp.square(jax.nn.relu(hf @ w_ff1)) @ w_ff2
    return x, k, v, c_hist[:, -(CONV_K - 1):]


def setup_inputs(seed: int = 0) -> dict:
    key = jax.random.key(seed)
    ks = jax.random.split(key, 40)

    def nrm(k, shape, scale):
        return jax.random.normal(k, shape, jnp.float32) * scale

    def gain(k, shape):
        return 1.0 + 0.05 * jax.random.normal(k, shape, jnp.float32)

    return {
        'x_prompt': nrm(ks[0], (BATCH, SEQ, D_MODEL), 1.0),
        'x_sample': nrm(ks[1], (DEC_BATCH, DEC_SEQ, D_MODEL), 1.0),
        'cache_k': nrm(ks[2], (DEPTH, DEC_BATCH, PAST_LEN, N_HEADS, 2, HEAD_DIM), 1.0),
        'cache_v': nrm(ks[3], (DEPTH, DEC_BATCH, PAST_LEN, N_HEADS, V_DIM), 1.0),
        'cache_conv': nrm(ks[4], (DEPTH, DEC_BATCH, CONV_K - 1, CONV_WIDTH), 0.5),
        'cache_mem_k': nrm(ks[5], (DEPTH, DEC_BATCH, N_MEM, MEM_HEADS, MEM_HEAD_DIM), 1.0),
        'cache_mem_v': nrm(ks[6], (DEPTH, DEC_BATCH, N_MEM, MEM_HEADS, MEM_HEAD_DIM), 1.0),
        'mem_prompt': nrm(ks[7], (BATCH, N_MEM, D_MODEL), 1.0),
        'rel_table': nrm(ks[8], (N_BUCKETS, N_HEADS), 0.5),
        'g_mix': gain(ks[9], (DEPTH, D_MODEL)),
        'w_in': nrm(ks[10], (DEPTH, D_MODEL, IN_COLS), D_MODEL ** -0.5),
        'g_q': gain(ks[11], (DEPTH, HEAD_DIM)),
        'g_k': gain(ks[12], (DEPTH, HEAD_DIM)),
        'lam_vec': nrm(ks[13], (DEPTH, 4, HEAD_DIM), 0.1),
        'g_sub': gain(ks[14], (DEPTH, V_DIM)),
        'w_conv': nrm(ks[15], (DEPTH, CONV_K, CONV_WIDTH), CONV_K ** -0.5),
        'b_conv': nrm(ks[16], (DEPTH, CONV_WIDTH), 0.02),
        'ln_g': gain(ks[17], (DEPTH, CONV_WIDTH)),
        'ln_b': nrm(ks[18], (DEPTH, CONV_WIDTH), 0.02),
        'w_out': nrm(ks[19], (DEPTH, MIX_WIDTH, D_MODEL), MIX_WIDTH ** -0.5),
        'g_cross': gain(ks[20], (DEPTH, D_MODEL)),
        'g_mem': gain(ks[21], (DEPTH, D_MODEL)),
        'w_mq': nrm(ks[22], (DEPTH, D_MODEL, MEM_WIDTH), D_MODEL ** -0.5),
        'w_mk': nrm(ks[23], (DEPTH, D_MODEL, MEM_WIDTH), D_MODEL ** -0.5),
        'w_mv': nrm(ks[24], (DEPTH, D_MODEL, MEM_WIDTH), D_MODEL ** -0.5),
        'g_mq': gain(ks[25], (DEPTH, MEM_HEAD_DIM)),
        'g_mk': gain(ks[26], (DEPTH, MEM_HEAD_DIM)),
        'w_mo': nrm(ks[27], (DEPTH, MEM_WIDTH, D_MODEL), MEM_WIDTH ** -0.5),
        'g_ffn': gain(ks[28], (DEPTH, D_MODEL)),
        'w_ff1': nrm(ks[29], (DEPTH, D_MODEL, D_FF), D_MODEL ** -0.5),
        'w_ff2': nrm(ks[30], (DEPTH, D_FF, D_MODEL), D_FF ** -0.5),
    }


def reference(x_prompt, x_sample, cache_k, cache_v, cache_conv, cache_mem_k, cache_mem_v,
              mem_prompt, rel_table, g_mix, w_in, g_q, g_k, lam_vec, g_sub, w_conv, b_conv,
              ln_g, ln_b, w_out, g_cross, g_mem, w_mq, w_mk, w_mv, g_mq, g_mk, w_mo,
              g_ffn, w_ff1, w_ff2):
    B = x_prompt.shape[0]
    yp, ys = x_prompt, x_sample
    empty_k = jnp.zeros((B, 0, N_HEADS, 2, HEAD_DIM), x_prompt.dtype)
    empty_v = jnp.zeros((B, 0, N_HEADS, V_DIM), x_prompt.dtype)
    zero_conv = jnp.zeros((B, CONV_K - 1, CONV_WIDTH), x_prompt.dtype)
    kp_l, vp_l, cp_l, mkp_l, mvp_l, ks_l, vs_l, cs_l = [], [], [], [], [], [], [], []
    for l in range(DEPTH):
        lam_init = 0.8 - 0.6 * math.exp(-0.3 * l)
        lw = (rel_table, g_mix[l], w_in[l], g_q[l], g_k[l], lam_vec[l], g_sub[l], w_conv[l],
              b_conv[l], ln_g[l], ln_b[l], w_out[l], g_cross[l], w_mq[l], g_mq[l], w_mo[l],
              g_ffn[l], w_ff1[l], w_ff2[l])
        mk_p, mv_p = memory_kv(mem_prompt, g_mem[l], w_mk[l], w_mv[l], g_mk[l])
        yp, kp, vp, cp = trunk_layer(yp, empty_k, empty_v, zero_conv, mk_p, mv_p, *lw, lam_init)
        ys, kn, vn, cn = trunk_layer(ys, cache_k[l], cache_v[l], cache_conv[l],
                                     cache_mem_k[l], cache_mem_v[l], *lw, lam_init)
        kp_l.append(kp); vp_l.append(vp); cp_l.append(cp)
        mkp_l.append(mk_p); mvp_l.append(mv_p)
        ks_l.append(kn); vs_l.append(vn); cs_l.append(cn)
    return (yp, ys, jnp.stack(kp_l), jnp.stack(vp_l), jnp.stack(cp_l), jnp.stack(mkp_l),
            jnp.stack(mvp_l), jnp.stack(ks_l), jnp.stack(vs_l), jnp.stack(cs_l))
```

```python
import functools
import math

import jax
import jax.numpy as jnp
import numpy as np
from jax import lax
from jax.experimental import pallas as pl
from jax.experimental.pallas import tpu as pltpu

F32 = jnp.float32
BF16 = jnp.bfloat16

CHUNK = 64
N_HEADS = 4
HEAD_DIM = 64
V_DIM = 2 * HEAD_DIM
HEAD_COLS = 2 * HEAD_DIM
CONV_K = 31
CONV_HALO = 32
N_BUCKETS = 32
MAX_DISTANCE = 128
MEM_HEADS = 4
MEM_HEAD_DIM = 128
EPS = 1e-6
NEG_INF = -1e30
LOG2E = math.log2(math.e)

ATTN_TQ = 256
ATTN_TK = 256
CACHE_TK = 1024
NEW_KEY_PAD = 128
ROW_TILE = 512
CONV_ROWS = 64
SAMPLE_BATCH_TILE = 8
VMEM_LIMIT = 56 * 1024 * 1024


def _cparams(sem):
    return pltpu.CompilerParams(dimension_semantics=sem, vmem_limit_bytes=VMEM_LIMIT)


def _rms(x, g):
    ms = jnp.mean(x * x, axis=-1, keepdims=True)
    return x * lax.rsqrt(ms + EPS) * g


def _const_spec(shape):
    return pl.BlockSpec(shape, lambda *_: (0,) * len(shape), pipeline_mode=pl.Buffered(1))


def _memkv_kernel(mem_ref, g_ref, wk_ref, wv_ref, gk_ref, mk_ref, mv_ref):
    m = _rms(mem_ref[...], g_ref[...]).astype(BF16)
    zk = jnp.dot(m, wk_ref[...], preferred_element_type=F32)
    mv_ref[...] = jnp.dot(m, wv_ref[...], preferred_element_type=F32)
    for h in range(MEM_HEADS):
        sl = slice(h * MEM_HEAD_DIM, (h + 1) * MEM_HEAD_DIM)
        mk_ref[:, sl] = _rms(zk[:, sl], gk_ref[...])


def _memkv(mem2d, g_mem, w_mk, w_mv, g_mk):
    n, d = mem2d.shape
    w = w_mk.shape[1]
    tm = min(ROW_TILE, n)
    return pl.pallas_call(
        _memkv_kernel,
        grid=(n // tm,),
        in_specs=[pl.BlockSpec((tm, d), lambda i: (i, 0)), _const_spec((1, d)),
                  _const_spec((d, w)), _const_spec((d, w)), _const_spec((1, MEM_HEAD_DIM))],
        out_specs=[pl.BlockSpec((tm, w), lambda i: (i, 0))] * 2,
        out_shape=[jax.ShapeDtypeStruct((n, w), F32)] * 2,
        compiler_params=_cparams(("parallel",)),
        name="memkv",
    )(mem2d, g_mem.reshape(1, d), w_mk.astype(BF16), w_mv.astype(BF16),
      g_mk.reshape(1, MEM_HEAD_DIM))


def _inproj_kernel(x_ref, g_ref, w_ref, gq_ref, gk_ref, seg_ref, *out_refs, qk_cols, emit_t):
    if emit_t:
        qt_ref, kb_ref, vt_ref, k_ref, v_ref, c_ref = out_refs
    else:
        qb_ref, k_ref, v_ref, c_ref = out_refs
    h = _rms(x_ref[0], g_ref[...]).astype(BF16)

    def proj(lo, hi):
        return jnp.dot(h, w_ref[:, lo:hi], preferred_element_type=F32)

    def mapnorm(z, g):
        ms = jnp.dot((z * z).astype(BF16), seg_ref[...], preferred_element_type=F32)
        return z * lax.rsqrt(ms + EPS) * g

    c0 = qk_cols
    q = mapnorm(proj(0, c0), gq_ref[...])
    k = mapnorm(proj(c0, 2 * c0), gk_ref[...])
    v = proj(2 * c0, 3 * c0)
    a = proj(3 * c0, 4 * c0)
    gate = proj(4 * c0, 5 * c0)
    k_ref[0] = k
    v_ref[0] = v
    c_ref[0] = a * jax.nn.sigmoid(gate)
    if emit_t:
        kb_ref[0] = k.astype(BF16)
        qt = q.T.astype(BF16)
        vt = v.T.astype(BF16)
        for j in range(qt_ref.shape[1]):
            qt_ref[0, j] = qt[:, j * ATTN_TQ:(j + 1) * ATTN_TQ]
        for j in range(vt_ref.shape[1]):
            vt_ref[0, j] = vt[:, j * ATTN_TK:(j + 1) * ATTN_TK]
    else:
        qb_ref[0] = q.astype(BF16)


def _inproj(x, g_mix, w_in_bf, gq_row, gk_row, seg, *, emit_t):
    b, t, d = x.shape
    cols = gq_row.shape[1]
    tm = min(ROW_TILE, t)
    nt = t // tm
    row = lambda width: pl.BlockSpec((1, tm, width), lambda bi, ti: (bi, ti, 0))
    f32_outs = [jax.ShapeDtypeStruct((b, t, cols), F32)] * 3
    f32_specs = [row(cols)] * 3
    if emit_t:
        nq, nk = tm // ATTN_TQ, tm // ATTN_TK
        out_shape = [jax.ShapeDtypeStruct((b, t // ATTN_TQ, cols, ATTN_TQ), BF16),
                     jax.ShapeDtypeStruct((b, t, cols), BF16),
                     jax.ShapeDtypeStruct((b, t // ATTN_TK, cols, ATTN_TK), BF16)] + f32_outs
        out_specs = [pl.BlockSpec((1, nq, cols, ATTN_TQ), lambda bi, ti: (bi, ti, 0, 0)),
                     row(cols),
                     pl.BlockSpec((1, nk, cols, ATTN_TK), lambda bi, ti: (bi, ti, 0, 0))] + f32_specs
    else:
        out_shape = [jax.ShapeDtypeStruct((b, t, cols), BF16)] + f32_outs
        out_specs = [row(cols)] + f32_specs
    return pl.pallas_call(
        functools.partial(_inproj_kernel, qk_cols=cols, emit_t=emit_t),
        grid=(b, nt),
        in_specs=[row(d), _const_spec((1, d)), _const_spec(w_in_bf.shape),
                  _const_spec((1, cols)), _const_spec((1, cols)), _const_spec((cols, cols))],
        out_specs=out_specs,
        out_shape=out_shape,
        compiler_params=_cparams(("parallel", "parallel")),
        name="inproj",
    )(x, g_mix.reshape(1, d), w_in_bf, gq_row, gk_row, seg)


def _attn_prompt_kernel(sc_ref, qt_ref, kb_ref, vt_ref, bias_ref, gsub_ref, o_ref,
                        qbd_ref, m_ref, l_ref, acc_ref):
    i = pl.program_id(1)
    tq, tk = ATTN_TQ, ATTN_TK
    lam = sc_ref[N_HEADS]

    upper = lax.broadcasted_iota(jnp.int32, (HEAD_COLS, tq), 0) < HEAD_DIM
    for h in range(N_HEADS):
        qh = qt_ref[0, 0, h * HEAD_COLS:(h + 1) * HEAD_COLS, :]
        zero = jnp.zeros_like(qh)
        qbd_ref[h, :, :tq] = jnp.where(upper, qh, zero)
        qbd_ref[h, :, tq:] = jnp.where(upper, zero, qh)
    m_ref[...] = jnp.full(m_ref.shape, NEG_INF, F32)
    l_ref[...] = jnp.zeros(l_ref.shape, F32)
    acc_ref[...] = jnp.zeros(acc_ref.shape, F32)

    def step(j, tile):
        row0 = pl.multiple_of(j * tk, tk)
        for h in range(N_HEADS):
            cols = slice(h * HEAD_COLS, (h + 1) * HEAD_COLS)
            kh = kb_ref[0, pl.ds(row0, tk), cols]
            s = jnp.dot(kh, qbd_ref[h], preferred_element_type=F32)
            m_prev = m_ref[h]
            if tile is None:
                shift = sc_ref[h]
                m_new = jnp.maximum(m_prev, jnp.max(s, axis=0, keepdims=True) + shift)
                p = jnp.exp2(s - (m_new - shift))
            else:
                s = s + bias_ref[h, tile]
                m_new = jnp.maximum(m_prev, jnp.max(s, axis=0, keepdims=True))
                p = jnp.exp2(s - m_new)
            alpha = jnp.exp2(m_prev - m_new)
            l_ref[h] = alpha * l_ref[h] + jnp.sum(p, axis=0, keepdims=True)
            pv = jnp.dot(vt_ref[0, j, cols, :], p.astype(BF16), preferred_element_type=F32)
            acc_ref[h] = alpha * acc_ref[h] + pv
            m_ref[h] = m_new

    def far_body(j, carry):
        step(j, None)
        return carry

    lax.fori_loop(0, jnp.maximum(i - 1, 0), far_body, 0)

    @pl.when(i >= 1)
    def _():
        step(i - 1, 0)

    step(i, 1)

    for h in range(N_HEADS):
        acc = acc_ref[h]
        r = 1.0 / l_ref[h]
        ot = acc[:, :tq] * r[:, :tq] - lam * (acc[:, tq:] * r[:, tq:])
        ms = jnp.mean(ot * ot, axis=0, keepdims=True)
        ot = ot * lax.rsqrt(ms + EPS)
        o_ref[0, :, h * V_DIM:(h + 1) * V_DIM] = (ot.T * gsub_ref[...]).astype(BF16)


def _attn_prompt(scalars, qt, kb, vt, bias_t, gsub_row):
    b, nq, cols, tq = qt.shape
    t = kb.shape[1]
    nk = vt.shape[1]
    return pl.pallas_call(
        _attn_prompt_kernel,
        grid=(b, nq),
        in_specs=[pl.BlockSpec(memory_space=pltpu.SMEM),
                  pl.BlockSpec((1, 1, cols, tq), lambda bi, i: (bi, i, 0, 0)),
                  pl.BlockSpec((1, t, cols), lambda bi, i: (bi, 0, 0)),
                  pl.BlockSpec((1, nk, cols, ATTN_TK), lambda bi, i: (bi, 0, 0, 0)),
                  _const_spec(bias_t.shape), _const_spec((1, V_DIM))],
        out_specs=pl.BlockSpec((1, tq, cols), lambda bi, i: (bi, i, 0)),
        out_shape=jax.ShapeDtypeStruct((b, t, cols), BF16),
        scratch_shapes=[pltpu.VMEM((N_HEADS, HEAD_COLS, 2 * tq), BF16),
                        pltpu.VMEM((N_HEADS, 1, 2 * tq), F32),
                        pltpu.VMEM((N_HEADS, 1, 2 * tq), F32),
                        pltpu.VMEM((N_HEADS, V_DIM, 2 * tq), F32)],
        compiler_params=_cparams(("parallel", "arbitrary")),
        name="attn_prompt",
    )(scalars, qt, kb, vt, bias_t, gsub_row)


def _attn_sample_kernel(sc_ref, q_ref, kc_ref, vc_ref, kn_ref, vn_ref, bc_ref, bn_ref, gsub_ref,
                        o_ref, qbd_ref, m_ref, l_ref, acc_ref):
    j = pl.program_id(1)
    tq = q_ref.shape[1]
    lam = sc_ref[N_HEADS]

    @pl.when(j == 0)
    def _():
        left = lax.broadcasted_iota(jnp.int32, (tq, HEAD_COLS), 1) < HEAD_DIM
        for h in range(N_HEADS):
            qh = q_ref[0, :, h * HEAD_COLS:(h + 1) * HEAD_COLS].astype(F32)
            qbd_ref[h, :tq, :] = jnp.where(left, qh, 0.0)
            qbd_ref[h, tq:, :] = jnp.where(left, 0.0, qh)
        m_ref[...] = jnp.full(m_ref.shape, NEG_INF, F32)
        l_ref[...] = jnp.zeros(l_ref.shape, F32)
        acc_ref[...] = jnp.zeros(acc_ref.shape, F32)

    def update(h, kh, vh, bias):
        s = lax.dot_general(qbd_ref[h].astype(BF16), kh, (((1,), (1,)), ((), ())),
                            preferred_element_type=F32) + bias
        m_prev = m_ref[h]
        m_new = jnp.maximum(m_prev, jnp.max(s, axis=-1, keepdims=True))
        alpha = jnp.exp2(m_prev - m_new)
        p = jnp.exp2(s - m_new)
        l_ref[h] = alpha * l_ref[h] + jnp.sum(p, axis=-1, keepdims=True)
        acc_ref[h] = alpha * acc_ref[h] + jnp.dot(p.astype(BF16), vh, preferred_element_type=F32)
        m_ref[h] = m_new

    for h in range(N_HEADS):
        cols = slice(h * HEAD_COLS, (h + 1) * HEAD_COLS)
        update(h, kc_ref[0, :, cols].astype(BF16), vc_ref[0, :, cols].astype(BF16), bc_ref[j, h])

    @pl.when(j == pl.num_programs(1) - 1)
    def _():
        for h in range(N_HEADS):
            cols = slice(h * HEAD_COLS, (h + 1) * HEAD_COLS)
            update(h, kn_ref[0, :, cols], vn_ref[0, :, cols], bn_ref[h])
            acc = acc_ref[h]
            r = 1.0 / l_ref[h]
            o = acc[:tq] * r[:tq] - lam * (acc[tq:] * r[tq:])
            o_ref[0, :, cols] = (_rms(o, gsub_ref[...])).astype(BF16)


def _attn_sample(scalars, q, kc, vc, kn, vn, bias_c, bias_n, gsub_row):
    b, tq, cols = q.shape
    p = kc.shape[1]
    nkc = p // CACHE_TK
    return pl.pallas_call(
        _attn_sample_kernel,
        grid=(b, nkc),
        in_specs=[pl.BlockSpec(memory_space=pltpu.SMEM),
                  pl.BlockSpec((1, tq, cols), lambda bi, j: (bi, 0, 0)),
                  pl.BlockSpec((1, CACHE_TK, cols), lambda bi, j: (bi, j, 0)),
                  pl.BlockSpec((1, CACHE_TK, cols), lambda bi, j: (bi, j, 0)),
                  pl.BlockSpec((1, NEW_KEY_PAD, cols), lambda bi, j: (bi, 0, 0)),
                  pl.BlockSpec((1, NEW_KEY_PAD, cols), lambda bi, j: (bi, 0, 0)),
                  _const_spec(bias_c.shape), _const_spec(bias_n.shape), _const_spec((1, V_DIM))],
        out_specs=pl.BlockSpec((1, tq, cols), lambda bi, j: (bi, 0, 0)),
        out_shape=jax.ShapeDtypeStruct((b, tq, cols), BF16),
        scratch_shapes=[pltpu.VMEM((N_HEADS, 2 * tq, HEAD_COLS), F32),
                        pltpu.VMEM((N_HEADS, 2 * tq, 1), F32),
                        pltpu.VMEM((N_HEADS, 2 * tq, 1), F32),
                        pltpu.VMEM((N_HEADS, 2 * tq, V_DIM), F32)],
        compiler_params=_cparams(("parallel", "arbitrary")),
        name="attn_sample",
    )(scalars, q, kc, vc, kn, vn, bias_c, bias_n, gsub_row)


def _mix_kernel(x_ref, o_ref, c_ref, hist_ref, wconv_ref, bconv_ref, lng_ref, lnb_ref,
                woo_ref, woc_ref, gcross_ref, wmq_ref, gmq_ref, mk_ref, mv_ref, wmo_ref,
                out_ref, ext_ref, cv_ref, ca_ref):
    bb, tt, d = x_ref.shape
    rows = bb * tt
    rc = min(CONV_ROWS, tt)
    lead = CONV_HALO - (CONV_K - 1)

    for b in range(bb):
        ext_ref[b, :CONV_HALO, :] = hist_ref[b, 0]
        ext_ref[b, CONV_HALO:, :] = c_ref[b]
    for b in range(bb):
        for r0 in range(0, tt, rc):
            acc = jnp.zeros((rc, c_ref.shape[2]), F32) + bconv_ref[...]
            for k in range(CONV_K):
                acc = acc + wconv_ref[k:k + 1, :] * ext_ref[b, r0 + lead + k:r0 + lead + k + rc, :]
            mu = jnp.mean(acc, axis=-1, keepdims=True)
            xc = acc - mu
            var = jnp.mean(xc * xc, axis=-1, keepdims=True)
            y = xc * lax.rsqrt(var + EPS) * lng_ref[...] + lnb_ref[...]
            cv_ref[b * tt + r0:b * tt + r0 + rc, :] = (y * jax.nn.sigmoid(y)).astype(BF16)

    x = x_ref[...].reshape(rows, d)
    o = o_ref[...].reshape(rows, o_ref.shape[2])
    x1 = (x + jnp.dot(o, woo_ref[...], preferred_element_type=F32)
          + jnp.dot(cv_ref[...], woc_ref[...], preferred_element_type=F32))

    hc = _rms(x1, gcross_ref[...]).astype(BF16)
    qm = jnp.dot(hc, wmq_ref[...], preferred_element_type=F32)
    for h in range(MEM_HEADS):
        cols = slice(h * MEM_HEAD_DIM, (h + 1) * MEM_HEAD_DIM)
        qn = _rms(qm[:, cols], gmq_ref[...]).astype(BF16)
        for b in range(bb):
            mk = mk_ref[b, :, cols].astype(BF16)
            mv = mv_ref[b, :, cols].astype(BF16)
            s = lax.dot_general(qn[b * tt:(b + 1) * tt], mk, (((1,), (1,)), ((), ())),
                                preferred_element_type=F32)
            p = jnp.exp2(s - jnp.max(s, axis=-1, keepdims=True))
            l = jnp.sum(p, axis=-1, keepdims=True)
            oh = jnp.dot(p.astype(BF16), mv, preferred_element_type=F32) / l
            ca_ref[b * tt:(b + 1) * tt, cols] = oh.astype(BF16)
    x2 = x1 + jnp.dot(ca_ref[...], wmo_ref[...], preferred_element_type=F32)
    out_ref[...] = x2.reshape(bb, tt, d)


def _mix(x, o, c, hist, wconv, bconv, lng, lnb, woo, woc, gcross, wmq, gmq_row, mk, mv, wmo,
         *, bb, tt):
    b, t, d = x.shape
    cw = c.shape[2]
    n_mem, mw = mk.shape[1], mk.shape[2]
    rows = bb * tt
    tile = lambda width: pl.BlockSpec((bb, tt, width), lambda bi, ti: (bi, ti, 0))
    return pl.pallas_call(
        _mix_kernel,
        grid=(b // bb, t // tt),
        in_specs=[tile(d), tile(cw), tile(cw),
                  pl.BlockSpec((bb, 1, CONV_HALO, cw), lambda bi, ti: (bi, ti, 0, 0)),
                  _const_spec(wconv.shape), _const_spec((1, cw)), _const_spec((1, cw)),
                  _const_spec((1, cw)), _const_spec(woo.shape), _const_spec(woc.shape),
                  _const_spec((1, d)), _const_spec(wmq.shape), _const_spec((1, MEM_HEAD_DIM)),
                  pl.BlockSpec((bb, n_mem, mw), lambda bi, ti: (bi, 0, 0)),
                  pl.BlockSpec((bb, n_mem, mw), lambda bi, ti: (bi, 0, 0)),
                  _const_spec(wmo.shape)],
        out_specs=tile(d),
        out_shape=jax.ShapeDtypeStruct((b, t, d), F32),
        scratch_shapes=[pltpu.VMEM((bb, CONV_HALO + tt, cw), F32),
                        pltpu.VMEM((rows, cw), BF16),
                        pltpu.VMEM((rows, mw), BF16)],
        compiler_params=_cparams(("parallel", "parallel")),
        name="mix",
    )(x, o, c, hist, wconv, bconv, lng, lnb, woo, woc, gcross, wmq, gmq_row, mk, mv, wmo)


def _mlp_kernel(x_ref, g_ref, w1_ref, w2_ref, out_ref):
    x = x_ref[...]
    hf = _rms(x, g_ref[...]).astype(BF16)
    u = jnp.maximum(jnp.dot(hf, w1_ref[...], preferred_element_type=F32), 0.0)
    out_ref[...] = x + jnp.dot((u * u).astype(BF16), w2_ref[...], preferred_element_type=F32)


def _mlp(x2d, g_ffn, w1, w2):
    n, d = x2d.shape
    tm = min(ROW_TILE, n)
    return pl.pallas_call(
        _mlp_kernel,
        grid=(n // tm,),
        in_specs=[pl.BlockSpec((tm, d), lambda i: (i, 0)), _const_spec((1, d)),
                  _const_spec(w1.shape), _const_spec(w2.shape)],
        out_specs=pl.BlockSpec((tm, d), lambda i: (i, 0)),
        out_shape=jax.ShapeDtypeStruct((n, d), F32),
        compiler_params=_cparams(("parallel",)),
        name="mlp",
    )(x2d, g_ffn.reshape(1, d), w1, w2)


def _rel_bucket(rel):
    half = N_BUCKETS // 2
    max_exact = half // 2
    ret = jnp.where(rel > 0, half, 0)
    n = jnp.abs(rel)
    nf = jnp.maximum(n, 1).astype(jnp.float32)
    large = max_exact + (jnp.log(nf / max_exact) / math.log(MAX_DISTANCE / max_exact)
                         * (half - max_exact)).astype(jnp.int32)
    large = jnp.minimum(large, half - 1)
    return ret + jnp.where(n < max_exact, n, large)


def _masked_bias(rel_table, q_pos, k_pos):
    bias = jnp.transpose(rel_table[_rel_bucket(k_pos[None, :] - q_pos[:, None])], (2, 0, 1))
    mask = (k_pos[None, :] // CHUNK) <= (q_pos[:, None] // CHUNK)
    return jnp.where(mask[None], bias.astype(F32) * LOG2E, NEG_INF)


def _far_bucket_is_saturated(min_distance):
    half = N_BUCKETS // 2
    max_exact = half // 2
    large = max_exact + int(np.log(min_distance / max_exact) / math.log(MAX_DISTANCE / max_exact)
                            * (half - max_exact) * (1 - 1e-6))
    return large >= half - 1


def _layer(x, k_past, v_past, c_past, mk, mv, w, lam, lam_init):
    b, t, d = x.shape
    prompt = k_past is None
    cols = N_HEADS * HEAD_COLS
    gsub_row = (w["g_sub"] * (1.0 - lam_init)).reshape(1, V_DIM)

    if prompt:
        assert t % ROW_TILE == 0 and ROW_TILE % ATTN_TQ == 0 and ATTN_TQ == ATTN_TK
        assert ATTN_TK % CHUNK == 0 and _far_bucket_is_saturated(ATTN_TK + 1)
        qt, kb, vt, k, v, c = _inproj(x, w["g_mix"], w["w_in"], w["gq_row"], w["gk_row"], w["seg"],
                                      emit_t=True)
        q_pos = ATTN_TQ + jnp.arange(ATTN_TQ, dtype=jnp.int32)
        near = _masked_bias(w["rel_table"], q_pos, jnp.arange(2 * ATTN_TK, dtype=jnp.int32))
        near = jnp.stack([near[:, :, :ATTN_TK], near[:, :, ATTN_TK:]], axis=1)
        near_t = jnp.swapaxes(near, 2, 3)
        bias_t = jnp.concatenate([near_t, near_t], axis=3)
        far = w["rel_table"][_rel_bucket(jnp.int32(-(ATTN_TK + 1)))].astype(F32) * LOG2E
        scalars = jnp.concatenate([far, lam.reshape(1)]).astype(F32)
        o = _attn_prompt(scalars, qt, kb, vt, bias_t, gsub_row)
        bb, tt = 1, ROW_TILE
    else:
        p = k_past.shape[1]
        assert p % CACHE_TK == 0 and t <= NEW_KEY_PAD and t % 16 == 0
        qb, k, v, c = _inproj(x.reshape(1, b * t, d), w["g_mix"], w["w_in"], w["gq_row"],
                              w["gk_row"], w["seg"], emit_t=False)
        qb, k, v, c = (a.reshape(b, t, cols) for a in (qb, k, v, c))
        pad = ((0, 0), (0, NEW_KEY_PAD - t), (0, 0))
        kn = jnp.pad(k.astype(BF16), pad)
        vn = jnp.pad(v.astype(BF16), pad)
        q_pos = p + jnp.arange(t, dtype=jnp.int32)
        bias = _masked_bias(w["rel_table"], q_pos, jnp.arange(p + NEW_KEY_PAD, dtype=jnp.int32))
        bias = jnp.where(jnp.arange(p + NEW_KEY_PAD) < p + t, bias, NEG_INF)
        bias = jnp.concatenate([bias, bias], axis=1)
        bias_c = bias[:, :, :p].reshape(N_HEADS, 2 * t, p // CACHE_TK, CACHE_TK)
        bias_c = jnp.transpose(bias_c, (2, 0, 1, 3))
        scalars = jnp.concatenate([jnp.zeros((N_HEADS,), F32), lam.reshape(1)]).astype(F32)
        o = _attn_sample(scalars, qb, k_past, v_past, kn, vn, bias_c, bias[:, :, p:], gsub_row)
        bb, tt = SAMPLE_BATCH_TILE, t
        assert b % bb == 0

    nt = t // tt
    first = jnp.pad(c_past, ((0, 0), (CONV_HALO - (CONV_K - 1), 0), (0, 0)))[:, None]
    if nt > 1:
        tails = c.reshape(b, nt, tt, c.shape[2])[:, :-1, tt - CONV_HALO:, :]
        hist = jnp.concatenate([first, tails], axis=1)
    else:
        hist = first
    x2 = _mix(x, o, c, hist, w["w_conv"], w["b_conv"], w["ln_g"], w["ln_b"], w["w_out_o"],
              w["w_out_c"], w["g_cross"], w["w_mq"], w["gmq_row"], mk, mv, w["w_mo"], bb=bb, tt=tt)
    y = _mlp(x2.reshape(b * t, d), w["g_ffn"], w["w_ff1"], w["w_ff2"]).reshape(b, t, d)
    if t >= CONV_K - 1:
        c_hist_tail = c[:, t - (CONV_K - 1):]
    else:
        c_hist_tail = jnp.concatenate([c_past[:, t:], c], axis=1)
    return y, k, v, c_hist_tail


def kernel(x_prompt, x_sample, cache_k, cache_v, cache_conv, cache_mem_k, cache_mem_v, mem_prompt,
           rel_table, g_mix, w_in, g_q, g_k, lam_vec, g_sub, w_conv, b_conv, ln_g, ln_b, w_out,
           g_cross, g_mem, w_mq, w_mk, w_mv, g_mq, g_mk, w_mo, g_ffn, w_ff1, w_ff2):
    depth = g_mix.shape[0]
    assert depth == 1
    b, t, d = x_prompt.shape
    bs, ts, _ = x_sample.shape
    cols = N_HEADS * HEAD_COLS
    cw = w_conv.shape[2]
    attn_w = N_HEADS * V_DIM
    l = 0
    lam_init = 0.8 - 0.6 * math.exp(-0.3 * l)
    lp = lam_vec[l].astype(F32)
    lam = jnp.exp(jnp.sum(lp[0] * lp[1])) - jnp.exp(jnp.sum(lp[2] * lp[3])) + lam_init

    seg = jnp.kron(jnp.eye(cols // HEAD_DIM, dtype=F32),
                   jnp.full((HEAD_DIM, HEAD_DIM), 1.0 / HEAD_DIM, F32)).astype(BF16)
    n_maps = cols // HEAD_DIM
    w = dict(
        rel_table=rel_table, g_mix=g_mix[l], w_in=w_in[l].astype(BF16), seg=seg,
        gq_row=jnp.tile(g_q[l] * (HEAD_DIM ** -0.5 * LOG2E), n_maps).reshape(1, cols),
        gk_row=jnp.tile(g_k[l], n_maps).reshape(1, cols),
        g_sub=g_sub[l],
        w_conv=jnp.pad(w_conv[l], ((0, CONV_HALO - CONV_K), (0, 0))),
        b_conv=b_conv[l].reshape(1, cw), ln_g=ln_g[l].reshape(1, cw), ln_b=ln_b[l].reshape(1, cw),
        w_out_o=w_out[l][:attn_w].astype(BF16), w_out_c=w_out[l][attn_w:].astype(BF16),
        g_cross=g_cross[l].reshape(1, d), w_mq=w_mq[l].astype(BF16),
        gmq_row=(g_mq[l] * (MEM_HEAD_DIM ** -0.5 * LOG2E)).reshape(1, MEM_HEAD_DIM),
        w_mo=w_mo[l].astype(BF16), g_ffn=g_ffn[l],
        w_ff1=w_ff1[l].astype(BF16), w_ff2=w_ff2[l].astype(BF16),
    )

    n_mem = mem_prompt.shape[1]
    mk_p, mv_p = _memkv(mem_prompt.reshape(b * n_mem, d), g_mem[l], w_mk[l], w_mv[l], g_mk[l])
    mk_p = mk_p.reshape(b, n_mem, -1)
    mv_p = mv_p.reshape(b, n_mem, -1)

    zero_conv = jnp.zeros((b, CONV_K - 1, cw), F32)
    yp, kp, vp, cp = _layer(x_prompt, None, None, zero_conv, mk_p, mv_p, w, lam, lam_init)

    p = cache_k.shape[2]
    ys, kn, vn, cn = _layer(x_sample, cache_k[l].reshape(bs, p, cols), cache_v[l].reshape(bs, p, cols),
                            cache_conv[l], cache_mem_k[l].reshape(bs, n_mem, -1),
                            cache_mem_v[l].reshape(bs, n_mem, -1), w, lam, lam_init)

    return (yp, ys,
            kp.reshape(1, b, t, N_HEADS, 2, HEAD_DIM), vp.reshape(1, b, t, N_HEADS, V_DIM),
            cp[None],
            mk_p.reshape(1, b, n_mem, MEM_HEADS, MEM_HEAD_DIM),
            mv_p.reshape(1, b, n_mem, MEM_HEADS, MEM_HEAD_DIM),
            kn.reshape(1, bs, ts, N_HEADS, 2, HEAD_DIM), vn.reshape(1, bs, ts, N_HEADS, V_DIM),
            cn[None])
```

```python
import functools
import math

import jax
import jax.numpy as jnp
import numpy as np
from jax import lax
from jax.experimental import pallas as pl
from jax.experimental.pallas import tpu as pltpu

F32 = jnp.float32
BF16 = jnp.bfloat16

CHUNK = 64
N_HEADS = 4
HEAD_DIM = 64
V_DIM = 2 * HEAD_DIM
HEAD_COLS = 2 * HEAD_DIM
CONV_K = 31
CONV_HALO = 32
N_BUCKETS = 32
MAX_DISTANCE = 128
MEM_HEADS = 4
MEM_HEAD_DIM = 128
EPS = 1e-6
NEG_INF = -1e30
LOG2E = math.log2(math.e)

ATTN_TQ = 256
ATTN_TK = 256
CACHE_TK = 1024
NEW_KEY_PAD = 128
ROW_TILE = 512
CONV_ROWS = 64
SAMPLE_BATCH_TILE = 8
VMEM_LIMIT = 56 * 1024 * 1024


def _cparams(sem):
    return pltpu.CompilerParams(dimension_semantics=sem, vmem_limit_bytes=VMEM_LIMIT)


def _rms(x, g):
    ms = jnp.mean(x * x, axis=-1, keepdims=True)
    return x * lax.rsqrt(ms + EPS) * g


def _const_spec(shape):
    return pl.BlockSpec(shape, lambda *_: (0,) * len(shape), pipeline_mode=pl.Buffered(1))


def _memkv_kernel(mem_ref, g_ref, wk_ref, wv_ref, gk_ref, mk_ref, mv_ref):
    m = _rms(mem_ref[...], g_ref[...]).astype(BF16)
    zk = jnp.dot(m, wk_ref[...], preferred_element_type=F32)
    zv = jnp.dot(m, wv_ref[...], preferred_element_type=F32)
    tm = mem_ref.shape[0]
    for h in range(MEM_HEADS):
        sl = slice(h * MEM_HEAD_DIM, (h + 1) * MEM_HEAD_DIM)
        rows = pl.ds(h, tm, stride=MEM_HEADS)
        mk_ref[rows, :] = _rms(zk[:, sl], gk_ref[...])
        mv_ref[rows, :] = zv[:, sl]


def _memkv(mem2d, g_mem, w_mk, w_mv, g_mk):
    n, d = mem2d.shape
    w = w_mk.shape[1]
    tm = min(ROW_TILE, n)
    return pl.pallas_call(
        _memkv_kernel,
        grid=(n // tm,),
        in_specs=[pl.BlockSpec((tm, d), lambda i: (i, 0)), _const_spec((1, d)),
                  _const_spec((d, w)), _const_spec((d, w)), _const_spec((1, MEM_HEAD_DIM))],
        out_specs=[pl.BlockSpec((tm * MEM_HEADS, MEM_HEAD_DIM), lambda i: (i, 0))] * 2,
        out_shape=[jax.ShapeDtypeStruct((n * MEM_HEADS, MEM_HEAD_DIM), F32)] * 2,
        compiler_params=_cparams(("parallel",)),
        name="memkv",
    )(mem2d, g_mem.reshape(1, d), w_mk.astype(BF16), w_mv.astype(BF16),
      g_mk.reshape(1, MEM_HEAD_DIM))


def _inproj_kernel(x_ref, g_ref, w_ref, gq_ref, gk_ref, seg_ref, *out_refs, qk_cols, emit_t):
    if emit_t:
        qt_ref, kb_ref, vt_ref, kt_ref, v4_ref, c_ref = out_refs
    else:
        qb_ref, k_ref, v_ref, c_ref = out_refs
    h = _rms(x_ref[0], g_ref[...]).astype(BF16)

    def proj(lo, hi):
        return jnp.dot(h, w_ref[:, lo:hi], preferred_element_type=F32)

    def mapnorm(z, g):
        ms = jnp.dot((z * z).astype(BF16), seg_ref[...], preferred_element_type=F32)
        return z * lax.rsqrt(ms + EPS) * g

    c0 = qk_cols
    q = mapnorm(proj(0, c0), gq_ref[...])
    k = mapnorm(proj(c0, 2 * c0), gk_ref[...])
    v = proj(2 * c0, 3 * c0)
    a = proj(3 * c0, 4 * c0)
    gate = proj(4 * c0, 5 * c0)
    c_ref[0] = a * jax.nn.sigmoid(gate)
    if emit_t:
        kb_ref[0] = k.astype(BF16)
        kt_ref[0] = k.T
        tm = x_ref.shape[1]
        for hh in range(N_HEADS):
            v4_ref[0, pl.ds(hh, tm, stride=N_HEADS), :] = v[:, hh * V_DIM:(hh + 1) * V_DIM]
        qt = q.T.astype(BF16)
        vt = v.T.astype(BF16)
        for j in range(qt_ref.shape[1]):
            qt_ref[0, j] = qt[:, j * ATTN_TQ:(j + 1) * ATTN_TQ]
        for j in range(vt_ref.shape[1]):
            vt_ref[0, j] = vt[:, j * ATTN_TK:(j + 1) * ATTN_TK]
    else:
        k_ref[0] = k
        v_ref[0] = v
        qb_ref[0] = q.astype(BF16)


def _inproj(x, g_mix, w_in_bf, gq_row, gk_row, seg, *, emit_t):
    b, t, d = x.shape
    cols = gq_row.shape[1]
    tm = min(ROW_TILE, t)
    nt = t // tm
    row = lambda width: pl.BlockSpec((1, tm, width), lambda bi, ti: (bi, ti, 0))
    f32_out = jax.ShapeDtypeStruct((b, t, cols), F32)
    if emit_t:
        nq, nk = tm // ATTN_TQ, tm // ATTN_TK
        out_shape = [jax.ShapeDtypeStruct((b, t // ATTN_TQ, cols, ATTN_TQ), BF16),
                     jax.ShapeDtypeStruct((b, t, cols), BF16),
                     jax.ShapeDtypeStruct((b, t // ATTN_TK, cols, ATTN_TK), BF16),
                     jax.ShapeDtypeStruct((b, cols, t), F32),
                     jax.ShapeDtypeStruct((b, t * N_HEADS, V_DIM), F32),
                     f32_out]
        out_specs = [pl.BlockSpec((1, nq, cols, ATTN_TQ), lambda bi, ti: (bi, ti, 0, 0)),
                     row(cols),
                     pl.BlockSpec((1, nk, cols, ATTN_TK), lambda bi, ti: (bi, ti, 0, 0)),
                     pl.BlockSpec((1, cols, tm), lambda bi, ti: (bi, 0, ti)),
                     pl.BlockSpec((1, tm * N_HEADS, V_DIM), lambda bi, ti: (bi, ti, 0)),
                     row(cols)]
    else:
        out_shape = [jax.ShapeDtypeStruct((b, t, cols), BF16)] + [f32_out] * 3
        out_specs = [row(cols)] * 4
    return pl.pallas_call(
        functools.partial(_inproj_kernel, qk_cols=cols, emit_t=emit_t),
        grid=(b, nt),
        in_specs=[row(d), _const_spec((1, d)), _const_spec(w_in_bf.shape),
                  _const_spec((1, cols)), _const_spec((1, cols)), _const_spec((cols, cols))],
        out_specs=out_specs,
        out_shape=out_shape,
        compiler_params=_cparams(("parallel", "parallel")),
        name="inproj",
    )(x, g_mix.reshape(1, d), w_in_bf, gq_row, gk_row, seg)


def _attn_prompt_kernel(sc_ref, qt_ref, kb_ref, vt_ref, bias_ref, gsub_ref, o_ref,
                        qbd_ref, m_ref, l_ref, acc_ref):
    i = pl.program_id(1)
    tq, tk = ATTN_TQ, ATTN_TK
    lam = sc_ref[N_HEADS]

    upper = lax.broadcasted_iota(jnp.int32, (HEAD_COLS, tq), 0) < HEAD_DIM
    for h in range(N_HEADS):
        qh = qt_ref[0, 0, h * HEAD_COLS:(h + 1) * HEAD_COLS, :]
        zero = jnp.zeros_like(qh)
        qbd_ref[h, :, :tq] = jnp.where(upper, qh, zero)
        qbd_ref[h, :, tq:] = jnp.where(upper, zero, qh)
    m_ref[...] = jnp.full(m_ref.shape, NEG_INF, F32)
    l_ref[...] = jnp.zeros(l_ref.shape, F32)
    acc_ref[...] = jnp.zeros(acc_ref.shape, F32)

    def step(j, tile):
        row0 = pl.multiple_of(j * tk, tk)
        for h in range(N_HEADS):
            cols = slice(h * HEAD_COLS, (h + 1) * HEAD_COLS)
            kh = kb_ref[0, pl.ds(row0, tk), cols]
            s = jnp.dot(kh, qbd_ref[h], preferred_element_type=F32)
            m_prev = m_ref[h]
            if tile is None:
                shift = sc_ref[h]
                m_new = jnp.maximum(m_prev, jnp.max(s, axis=0, keepdims=True) + shift)
                p = jnp.exp2(s - (m_new - shift))
            else:
                s = s + bias_ref[h, tile]
                m_new = jnp.maximum(m_prev, jnp.max(s, axis=0, keepdims=True))
                p = jnp.exp2(s - m_new)
            alpha = jnp.exp2(m_prev - m_new)
            l_ref[h] = alpha * l_ref[h] + jnp.sum(p, axis=0, keepdims=True)
            pv = jnp.dot(vt_ref[0, j, cols, :], p.astype(BF16), preferred_element_type=F32)
            acc_ref[h] = alpha * acc_ref[h] + pv
            m_ref[h] = m_new

    def far_body(j, carry):
        step(j, None)
        return carry

    lax.fori_loop(0, jnp.maximum(i - 1, 0), far_body, 0)

    @pl.when(i >= 1)
    def _():
        step(i - 1, 0)

    step(i, 1)

    for h in range(N_HEADS):
        acc = acc_ref[h]
        r = 1.0 / l_ref[h]
        ot = acc[:, :tq] * r[:, :tq] - lam * (acc[:, tq:] * r[:, tq:])
        ms = jnp.mean(ot * ot, axis=0, keepdims=True)
        ot = ot * lax.rsqrt(ms + EPS)
        o_ref[0, :, h * V_DIM:(h + 1) * V_DIM] = (ot.T * gsub_ref[...]).astype(BF16)


def _attn_prompt(scalars, qt, kb, vt, bias_t, gsub_row):
    b, nq, cols, tq = qt.shape
    t = kb.shape[1]
    nk = vt.shape[1]
    return pl.pallas_call(
        _attn_prompt_kernel,
        grid=(b, nq),
        in_specs=[pl.BlockSpec(memory_space=pltpu.SMEM),
                  pl.BlockSpec((1, 1, cols, tq), lambda bi, i: (bi, i, 0, 0)),
                  pl.BlockSpec((1, t, cols), lambda bi, i: (bi, 0, 0)),
                  pl.BlockSpec((1, nk, cols, ATTN_TK), lambda bi, i: (bi, 0, 0, 0)),
                  _const_spec(bias_t.shape), _const_spec((1, V_DIM))],
        out_specs=pl.BlockSpec((1, tq, cols), lambda bi, i: (bi, i, 0)),
        out_shape=jax.ShapeDtypeStruct((b, t, cols), BF16),
        scratch_shapes=[pltpu.VMEM((N_HEADS, HEAD_COLS, 2 * tq), BF16),
                        pltpu.VMEM((N_HEADS, 1, 2 * tq), F32),
                        pltpu.VMEM((N_HEADS, 1, 2 * tq), F32),
                        pltpu.VMEM((N_HEADS, V_DIM, 2 * tq), F32)],
        compiler_params=_cparams(("parallel", "arbitrary")),
        name="attn_prompt",
    )(scalars, qt, kb, vt, bias_t, gsub_row)


def _attn_sample_kernel(sc_ref, q_ref, kt_ref, vc_ref, ktn_ref, vn_ref, bc_ref, bn_ref, gsub_ref,
                        o_ref, qbd_ref, m_ref, l_ref, acc_ref):
    j = pl.program_id(1)
    tq, cols = q_ref.shape[1], q_ref.shape[2]
    lam = sc_ref[N_HEADS]

    @pl.when(j == 0)
    def _():
        q = q_ref[0].astype(F32)
        col = lax.broadcasted_iota(jnp.int32, (tq, cols), 1)
        for hm in range(cols // HEAD_DIM):
            mine = (col >= hm * HEAD_DIM) & (col < (hm + 1) * HEAD_DIM)
            qbd_ref[hm * tq:(hm + 1) * tq, :] = jnp.where(mine, q, 0.0)
        m_ref[...] = jnp.full(m_ref.shape, NEG_INF, F32)
        l_ref[...] = jnp.zeros(l_ref.shape, F32)
        acc_ref[...] = jnp.zeros(acc_ref.shape, F32)

    def update(kt, v_of_head, bias):
        s = jnp.dot(qbd_ref[...].astype(BF16), kt, preferred_element_type=F32) + bias
        m_prev = m_ref[...]
        m_new = jnp.maximum(m_prev, jnp.max(s, axis=-1, keepdims=True))
        alpha = jnp.exp2(m_prev - m_new)
        p = jnp.exp2(s - m_new)
        l_ref[...] = alpha * l_ref[...] + jnp.sum(p, axis=-1, keepdims=True)
        pb = p.astype(BF16)
        for h in range(N_HEADS):
            rows = slice(h * 2 * tq, (h + 1) * 2 * tq)
            acc_ref[rows, :] = alpha[rows] * acc_ref[rows, :] + jnp.dot(
                pb[rows], v_of_head(h), preferred_element_type=F32)
        m_ref[...] = m_new

    tk = kt_ref.shape[2]
    update(kt_ref[0].astype(BF16),
           lambda h: vc_ref[0, pl.ds(h, tk, stride=N_HEADS), :].astype(BF16), bc_ref[j])

    @pl.when(j == pl.num_programs(1) - 1)
    def _():
        update(ktn_ref[0], lambda h: vn_ref[0, :, h * V_DIM:(h + 1) * V_DIM], bn_ref[...])
        for h in range(N_HEADS):
            r0 = h * 2 * tq
            inv0 = 1.0 / l_ref[r0:r0 + tq, :]
            inv1 = 1.0 / l_ref[r0 + tq:r0 + 2 * tq, :]
            o = acc_ref[r0:r0 + tq, :] * inv0 - lam * (acc_ref[r0 + tq:r0 + 2 * tq, :] * inv1)
            o_ref[0, :, h * V_DIM:(h + 1) * V_DIM] = _rms(o, gsub_ref[...]).astype(BF16)


def _attn_sample(scalars, q, kt, v4, ktn, vn, bias_c, bias_n, gsub_row):
    b, tq, cols = q.shape
    p = kt.shape[2]
    nkc = p // CACHE_TK
    rows = 2 * N_HEADS * tq
    return pl.pallas_call(
        _attn_sample_kernel,
        grid=(b, nkc),
        in_specs=[pl.BlockSpec(memory_space=pltpu.SMEM),
                  pl.BlockSpec((1, tq, cols), lambda bi, j: (bi, 0, 0)),
                  pl.BlockSpec((1, cols, CACHE_TK), lambda bi, j: (bi, 0, j)),
                  pl.BlockSpec((1, CACHE_TK * N_HEADS, V_DIM), lambda bi, j: (bi, j, 0)),
                  pl.BlockSpec((1, cols, NEW_KEY_PAD), lambda bi, j: (bi, 0, 0)),
                  pl.BlockSpec((1, NEW_KEY_PAD, cols), lambda bi, j: (bi, 0, 0)),
                  _const_spec(bias_c.shape), _const_spec(bias_n.shape), _const_spec((1, V_DIM))],
        out_specs=pl.BlockSpec((1, tq, cols), lambda bi, j: (bi, 0, 0)),
        out_shape=jax.ShapeDtypeStruct((b, tq, cols), BF16),
        scratch_shapes=[pltpu.VMEM((rows, cols), F32),
                        pltpu.VMEM((rows, 1), F32),
                        pltpu.VMEM((rows, 1), F32),
                        pltpu.VMEM((rows, V_DIM), F32)],
        compiler_params=_cparams(("parallel", "arbitrary")),
        name="attn_sample",
    )(scalars, q, kt, v4, ktn, vn, bias_c, bias_n, gsub_row)


def _mix_kernel(x_ref, o_ref, c_ref, hist_ref, wconv_ref, bconv_ref, lng_ref, lnb_ref,
                woo_ref, woc_ref, gcross_ref, wmq_ref, gmq_ref, mk_ref, mv_ref, wmo_ref,
                out_ref, ext_ref, cv_ref, ca_ref):
    bb, tt, d = x_ref.shape
    rows = bb * tt
    n_mem = mk_ref.shape[1] // MEM_HEADS
    rc = min(CONV_ROWS, tt)
    lead = CONV_HALO - (CONV_K - 1)

    for b in range(bb):
        ext_ref[b, :CONV_HALO, :] = hist_ref[b, 0]
        ext_ref[b, CONV_HALO:, :] = c_ref[b]
    for b in range(bb):
        for r0 in range(0, tt, rc):
            acc = jnp.zeros((rc, c_ref.shape[2]), F32) + bconv_ref[...]
            for k in range(CONV_K):
                acc = acc + wconv_ref[k:k + 1, :] * ext_ref[b, r0 + lead + k:r0 + lead + k + rc, :]
            mu = jnp.mean(acc, axis=-1, keepdims=True)
            xc = acc - mu
            var = jnp.mean(xc * xc, axis=-1, keepdims=True)
            y = xc * lax.rsqrt(var + EPS) * lng_ref[...] + lnb_ref[...]
            cv_ref[b * tt + r0:b * tt + r0 + rc, :] = (y * jax.nn.sigmoid(y)).astype(BF16)

    x = x_ref[...].reshape(rows, d)
    o = o_ref[...].reshape(rows, o_ref.shape[2])
    x1 = (x + jnp.dot(o, woo_ref[...], preferred_element_type=F32)
          + jnp.dot(cv_ref[...], woc_ref[...], preferred_element_type=F32))

    hc = _rms(x1, gcross_ref[...]).astype(BF16)
    qm = jnp.dot(hc, wmq_ref[...], preferred_element_type=F32)
    for h in range(MEM_HEADS):
        cols = slice(h * MEM_HEAD_DIM, (h + 1) * MEM_HEAD_DIM)
        qn = _rms(qm[:, cols], gmq_ref[...]).astype(BF16)
        for b in range(bb):
            mem_rows = pl.ds(h, n_mem, stride=MEM_HEADS)
            mk = mk_ref[b, mem_rows, :].astype(BF16)
            mv = mv_ref[b, mem_rows, :].astype(BF16)
            s = lax.dot_general(qn[b * tt:(b + 1) * tt], mk, (((1,), (1,)), ((), ())),
                                preferred_element_type=F32)
            p = jnp.exp2(s - jnp.max(s, axis=-1, keepdims=True))
            l = jnp.sum(p, axis=-1, keepdims=True)
            oh = jnp.dot(p.astype(BF16), mv, preferred_element_type=F32) / l
            ca_ref[b * tt:(b + 1) * tt, cols] = oh.astype(BF16)
    x2 = x1 + jnp.dot(ca_ref[...], wmo_ref[...], preferred_element_type=F32)
    out_ref[...] = x2.reshape(bb, tt, d)


def _mix(x, o, c, hist, wconv, bconv, lng, lnb, woo, woc, gcross, wmq, gmq_row, mk, mv, wmo,
         *, bb, tt):
    b, t, d = x.shape
    cw = c.shape[2]
    mem_rows, mhd = mk.shape[1], mk.shape[2]
    mw = wmq.shape[1]
    rows = bb * tt
    tile = lambda width: pl.BlockSpec((bb, tt, width), lambda bi, ti: (bi, ti, 0))
    return pl.pallas_call(
        _mix_kernel,
        grid=(b // bb, t // tt),
        in_specs=[tile(d), tile(cw), tile(cw),
                  pl.BlockSpec((bb, 1, CONV_HALO, cw), lambda bi, ti: (bi, ti, 0, 0)),
                  _const_spec(wconv.shape), _const_spec((1, cw)), _const_spec((1, cw)),
                  _const_spec((1, cw)), _const_spec(woo.shape), _const_spec(woc.shape),
                  _const_spec((1, d)), _const_spec(wmq.shape), _const_spec((1, MEM_HEAD_DIM)),
                  pl.BlockSpec((bb, mem_rows, mhd), lambda bi, ti: (bi, 0, 0)),
                  pl.BlockSpec((bb, mem_rows, mhd), lambda bi, ti: (bi, 0, 0)),
                  _const_spec(wmo.shape)],
        out_specs=tile(d),
        out_shape=jax.ShapeDtypeStruct((b, t, d), F32),
        scratch_shapes=[pltpu.VMEM((bb, CONV_HALO + tt, cw), F32),
                        pltpu.VMEM((rows, cw), BF16),
                        pltpu.VMEM((rows, mw), BF16)],
        compiler_params=_cparams(("parallel", "parallel")),
        name="mix",
    )(x, o, c, hist, wconv, bconv, lng, lnb, woo, woc, gcross, wmq, gmq_row, mk, mv, wmo)


def _mlp_kernel(x_ref, g_ref, w1_ref, w2_ref, out_ref):
    x = x_ref[...]
    hf = _rms(x, g_ref[...]).astype(BF16)
    u = jnp.maximum(jnp.dot(hf, w1_ref[...], preferred_element_type=F32), 0.0)
    out_ref[...] = x + jnp.dot((u * u).astype(BF16), w2_ref[...], preferred_element_type=F32)


def _mlp(x2d, g_ffn, w1, w2):
    n, d = x2d.shape
    tm = min(ROW_TILE, n)
    return pl.pallas_call(
        _mlp_kernel,
        grid=(n // tm,),
        in_specs=[pl.BlockSpec((tm, d), lambda i: (i, 0)), _const_spec((1, d)),
                  _const_spec(w1.shape), _const_spec(w2.shape)],
        out_specs=pl.BlockSpec((tm, d), lambda i: (i, 0)),
        out_shape=jax.ShapeDtypeStruct((n, d), F32),
        compiler_params=_cparams(("parallel",)),
        name="mlp",
    )(x2d, g_ffn.reshape(1, d), w1, w2)


def _rel_bucket(rel):
    half = N_BUCKETS // 2
    max_exact = half // 2
    ret = jnp.where(rel > 0, half, 0)
    n = jnp.abs(rel)
    nf = jnp.maximum(n, 1).astype(jnp.float32)
    large = max_exact + (jnp.log(nf / max_exact) / math.log(MAX_DISTANCE / max_exact)
                         * (half - max_exact)).astype(jnp.int32)
    large = jnp.minimum(large, half - 1)
    return ret + jnp.where(n < max_exact, n, large)


def _masked_bias(rel_table, q_pos, k_pos):
    bucket = _rel_bucket(k_pos[None, :] - q_pos[:, None])[None]
    table = rel_table.astype(F32) * LOG2E
    bias = jnp.zeros((rel_table.shape[1],) + bucket.shape[1:], F32)
    for bkt in range(N_BUCKETS):
        bias = jnp.where(bucket == bkt, table[bkt][:, None, None], bias)
    mask = (k_pos[None, :] // CHUNK) <= (q_pos[:, None] // CHUNK)
    return jnp.where(mask[None], bias, NEG_INF)


def _far_bucket_is_saturated(min_distance):
    half = N_BUCKETS // 2
    max_exact = half // 2
    large = max_exact + int(np.log(min_distance / max_exact) / math.log(MAX_DISTANCE / max_exact)
                            * (half - max_exact) * (1 - 1e-6))
    return large >= half - 1


def _layer(x, k_past, v_past, c_past, mk, mv, w, lam, lam_init):
    b, t, d = x.shape
    prompt = k_past is None
    cols = N_HEADS * HEAD_COLS
    gsub_row = (w["g_sub"] * (1.0 - lam_init)).reshape(1, V_DIM)

    if prompt:
        assert t % ROW_TILE == 0 and ROW_TILE % ATTN_TQ == 0 and ATTN_TQ == ATTN_TK
        assert ATTN_TK % CHUNK == 0 and _far_bucket_is_saturated(ATTN_TK + 1)
        qt, kb, vt, kt, v4, c = _inproj(x, w["g_mix"], w["w_in"], w["gq_row"], w["gk_row"],
                                        w["seg"], emit_t=True)
        k_out = jnp.transpose(kt.reshape(b, N_HEADS, 2, HEAD_DIM, t), (0, 4, 1, 2, 3))
        v_out = v4.reshape(b, t, N_HEADS, V_DIM)
        q_pos = ATTN_TQ + jnp.arange(ATTN_TQ, dtype=jnp.int32)
        near = _masked_bias(w["rel_table"], q_pos, jnp.arange(2 * ATTN_TK, dtype=jnp.int32))
        near = jnp.stack([near[:, :, :ATTN_TK], near[:, :, ATTN_TK:]], axis=1)
        near_t = jnp.swapaxes(near, 2, 3)
        bias_t = jnp.concatenate([near_t, near_t], axis=3)
        far = w["rel_table"][_rel_bucket(jnp.int32(-(ATTN_TK + 1)))].astype(F32) * LOG2E
        scalars = jnp.concatenate([far, lam.reshape(1)]).astype(F32)
        o = _attn_prompt(scalars, qt, kb, vt, bias_t, gsub_row)
        bb, tt = 1, ROW_TILE
    else:
        p = k_past.shape[2]
        assert p % CACHE_TK == 0 and t <= NEW_KEY_PAD and t % 16 == 0
        qb, k, v, c = _inproj(x.reshape(1, b * t, d), w["g_mix"], w["w_in"], w["gq_row"],
                              w["gk_row"], w["seg"], emit_t=False)
        qb, k, v, c = (a.reshape(b, t, cols) for a in (qb, k, v, c))
        k_out = k.reshape(b, t, N_HEADS, 2, HEAD_DIM)
        v_out = v.reshape(b, t, N_HEADS, V_DIM)
        ktn = jnp.pad(jnp.swapaxes(k, 1, 2).astype(BF16), ((0, 0), (0, 0), (0, NEW_KEY_PAD - t)))
        vn = jnp.pad(v.astype(BF16), ((0, 0), (0, NEW_KEY_PAD - t), (0, 0)))
        q_pos = p + jnp.arange(t, dtype=jnp.int32)
        bias = _masked_bias(w["rel_table"], q_pos, jnp.arange(p + NEW_KEY_PAD, dtype=jnp.int32))
        bias = jnp.where(jnp.arange(p + NEW_KEY_PAD) < p + t, bias, NEG_INF)
        rows = 2 * N_HEADS * t
        bias = jnp.broadcast_to(bias[:, None], (N_HEADS, 2, t, p + NEW_KEY_PAD)).reshape(rows, -1)
        bias_c = jnp.swapaxes(bias[:, :p].reshape(rows, p // CACHE_TK, CACHE_TK), 0, 1)
        scalars = jnp.concatenate([jnp.zeros((N_HEADS,), F32), lam.reshape(1)]).astype(F32)
        o = _attn_sample(scalars, qb, k_past, v_past, ktn, vn, bias_c, bias[:, p:], gsub_row)
        bb, tt = SAMPLE_BATCH_TILE, t
        assert b % bb == 0

    nt = t // tt
    first = jnp.pad(c_past, ((0, 0), (CONV_HALO - (CONV_K - 1), 0), (0, 0)))[:, None]
    if nt > 1:
        tails = c.reshape(b, nt, tt, c.shape[2])[:, :-1, tt - CONV_HALO:, :]
        hist = jnp.concatenate([first, tails], axis=1)
    else:
        hist = first
    x2 = _mix(x, o, c, hist, w["w_conv"], w["b_conv"], w["ln_g"], w["ln_b"], w["w_out_o"],
              w["w_out_c"], w["g_cross"], w["w_mq"], w["gmq_row"], mk, mv, w["w_mo"], bb=bb, tt=tt)
    y = _mlp(x2.reshape(b * t, d), w["g_ffn"], w["w_ff1"], w["w_ff2"]).reshape(b, t, d)
    if t >= CONV_K - 1:
        c_hist_tail = c[:, t - (CONV_K - 1):]
    else:
        c_hist_tail = jnp.concatenate([c_past[:, t:], c], axis=1)
    return y, k_out, v_out, c_hist_tail


def kernel(x_prompt, x_sample, cache_k, cache_v, cache_conv, cache_mem_k, cache_mem_v, mem_prompt,
           rel_table, g_mix, w_in, g_q, g_k, lam_vec, g_sub, w_conv, b_conv, ln_g, ln_b, w_out,
           g_cross, g_mem, w_mq, w_mk, w_mv, g_mq, g_mk, w_mo, g_ffn, w_ff1, w_ff2):
    depth = g_mix.shape[0]
    assert depth == 1
    b, t, d = x_prompt.shape
    bs, ts, _ = x_sample.shape
    cols = N_HEADS * HEAD_COLS
    cw = w_conv.shape[2]
    attn_w = N_HEADS * V_DIM
    l = 0
    lam_init = 0.8 - 0.6 * math.exp(-0.3 * l)
    lp = lam_vec[l].astype(F32)
    lam = jnp.exp(jnp.sum(lp[0] * lp[1])) - jnp.exp(jnp.sum(lp[2] * lp[3])) + lam_init

    seg = jnp.kron(jnp.eye(cols // HEAD_DIM, dtype=F32),
                   jnp.full((HEAD_DIM, HEAD_DIM), 1.0 / HEAD_DIM, F32)).astype(BF16)
    n_maps = cols // HEAD_DIM
    w = dict(
        rel_table=rel_table, g_mix=g_mix[l], w_in=w_in[l].astype(BF16), seg=seg,
        gq_row=jnp.tile(g_q[l] * (HEAD_DIM ** -0.5 * LOG2E), n_maps).reshape(1, cols),
        gk_row=jnp.tile(g_k[l], n_maps).reshape(1, cols),
        g_sub=g_sub[l],
        w_conv=jnp.pad(w_conv[l], ((0, CONV_HALO - CONV_K), (0, 0))),
        b_conv=b_conv[l].reshape(1, cw), ln_g=ln_g[l].reshape(1, cw), ln_b=ln_b[l].reshape(1, cw),
        w_out_o=w_out[l][:attn_w].astype(BF16), w_out_c=w_out[l][attn_w:].astype(BF16),
        g_cross=g_cross[l].reshape(1, d), w_mq=w_mq[l].astype(BF16),
        gmq_row=(g_mq[l] * (MEM_HEAD_DIM ** -0.5 * LOG2E)).reshape(1, MEM_HEAD_DIM),
        w_mo=w_mo[l].astype(BF16), g_ffn=g_ffn[l],
        w_ff1=w_ff1[l].astype(BF16), w_ff2=w_ff2[l].astype(BF16),
    )

    n_mem = mem_prompt.shape[1]
    mk_p, mv_p = _memkv(mem_prompt.reshape(b * n_mem, d), g_mem[l], w_mk[l], w_mv[l], g_mk[l])
    mk_p = mk_p.reshape(b, n_mem * MEM_HEADS, MEM_HEAD_DIM)
    mv_p = mv_p.reshape(b, n_mem * MEM_HEADS, MEM_HEAD_DIM)

    zero_conv = jnp.zeros((b, CONV_K - 1, cw), F32)
    yp, kp, vp, cp = _layer(x_prompt, None, None, zero_conv, mk_p, mv_p, w, lam, lam_init)

    p = cache_k.shape[2]
    k_past = jnp.transpose(cache_k[l], (0, 2, 3, 4, 1)).reshape(bs, cols, p)
    v_past = cache_v[l].reshape(bs, p * N_HEADS, V_DIM)
    ys, kn, vn, cn = _layer(x_sample, k_past, v_past, cache_conv[l],
                            cache_mem_k[l].reshape(bs, n_mem * MEM_HEADS, MEM_HEAD_DIM),
                            cache_mem_v[l].reshape(bs, n_mem * MEM_HEADS, MEM_HEAD_DIM),
                            w, lam, lam_init)

    return (yp, ys, kp[None], vp[None], cp[None],
            mk_p.reshape(1, b, n_mem, MEM_HEADS, MEM_HEAD_DIM),
            mv_p.reshape(1, b, n_mem, MEM_HEADS, MEM_HEAD_DIM),
            kn[None], vn[None], cn[None])
```

```python
import functools
import math

import jax
import jax.numpy as jnp
import numpy as np
from jax import lax
from jax.experimental import pallas as pl
from jax.experimental.pallas import tpu as pltpu

F32 = jnp.float32
BF16 = jnp.bfloat16

CHUNK = 64
N_HEADS = 4
HEAD_DIM = 64
V_DIM = 2 * HEAD_DIM
HEAD_COLS = 2 * HEAD_DIM
CONV_K = 31
CONV_HALO = 32
N_BUCKETS = 32
MAX_DISTANCE = 128
MEM_HEADS = 4
MEM_HEAD_DIM = 128
EPS = 1e-6
NEG_INF = -1e30
LOG2E = math.log2(math.e)

ATTN_TQ = 256
ATTN_TK = 256
CACHE_TK = 1024
NEW_KEY_PAD = 128
ROW_TILE = 512
CONV_ROWS = 64
SAMPLE_BATCH_TILE = 8
VMEM_LIMIT = 56 * 1024 * 1024


def _cparams(sem):
    return pltpu.CompilerParams(dimension_semantics=sem, vmem_limit_bytes=VMEM_LIMIT)


def _rms(x, g):
    ms = jnp.mean(x * x, axis=-1, keepdims=True)
    return x * lax.rsqrt(ms + EPS) * g


def _const_spec(shape):
    return pl.BlockSpec(shape, lambda *_: (0,) * len(shape), pipeline_mode=pl.Buffered(1))


def _memkv_kernel(mem_ref, g_ref, wk_ref, wv_ref, gk_ref, mk_ref, mv_ref):
    m = _rms(mem_ref[...], g_ref[...]).astype(BF16)
    zk = jnp.dot(m, wk_ref[...], preferred_element_type=F32)
    zv = jnp.dot(m, wv_ref[...], preferred_element_type=F32)
    tm = mem_ref.shape[0]
    for h in range(MEM_HEADS):
        sl = slice(h * MEM_HEAD_DIM, (h + 1) * MEM_HEAD_DIM)
        rows = pl.ds(h, tm, stride=MEM_HEADS)
        mk_ref[rows, :] = _rms(zk[:, sl], gk_ref[...])
        mv_ref[rows, :] = zv[:, sl]


def _memkv(mem2d, g_mem, w_mk, w_mv, g_mk):
    n, d = mem2d.shape
    w = w_mk.shape[1]
    tm = min(ROW_TILE, n)
    return pl.pallas_call(
        _memkv_kernel,
        grid=(n // tm,),
        in_specs=[pl.BlockSpec((tm, d), lambda i: (i, 0)), _const_spec((1, d)),
                  _const_spec((d, w)), _const_spec((d, w)), _const_spec((1, MEM_HEAD_DIM))],
        out_specs=[pl.BlockSpec((tm * MEM_HEADS, MEM_HEAD_DIM), lambda i: (i, 0))] * 2,
        out_shape=[jax.ShapeDtypeStruct((n * MEM_HEADS, MEM_HEAD_DIM), F32)] * 2,
        compiler_params=_cparams(("parallel",)),
        name="memkv",
    )(mem2d, g_mem.reshape(1, d), w_mk.astype(BF16), w_mv.astype(BF16),
      g_mk.reshape(1, MEM_HEAD_DIM))


def _inproj_kernel(x_ref, g_ref, w_ref, gq_ref, gk_ref, seg_ref, *out_refs, qk_cols, emit_t):
    if emit_t:
        qt_ref, kb_ref, vt_ref, kt_ref, v4_ref, c_ref = out_refs
    else:
        qb_ref, k_ref, v_ref, c_ref = out_refs
    h = _rms(x_ref[0], g_ref[...]).astype(BF16)

    def proj(lo, hi):
        return jnp.dot(h, w_ref[:, lo:hi], preferred_element_type=F32)

    def mapnorm(z, g):
        ms = jnp.dot((z * z).astype(BF16), seg_ref[...], preferred_element_type=F32)
        return z * lax.rsqrt(ms + EPS) * g

    c0 = qk_cols
    q = mapnorm(proj(0, c0), gq_ref[...])
    k = mapnorm(proj(c0, 2 * c0), gk_ref[...])
    v = proj(2 * c0, 3 * c0)
    a = proj(3 * c0, 4 * c0)
    gate = proj(4 * c0, 5 * c0)
    c_ref[0] = a * jax.nn.sigmoid(gate)
    if emit_t:
        kb_ref[0] = k.astype(BF16)
        kt_ref[0] = k.T
        tm = x_ref.shape[1]
        for hh in range(N_HEADS):
            v4_ref[0, pl.ds(hh, tm, stride=N_HEADS), :] = v[:, hh * V_DIM:(hh + 1) * V_DIM]
        qt = q.T.astype(BF16)
        vt = v.T.astype(BF16)
        for j in range(qt_ref.shape[1]):
            qt_ref[0, j] = qt[:, j * ATTN_TQ:(j + 1) * ATTN_TQ]
        for j in range(vt_ref.shape[1]):
            vt_ref[0, j] = vt[:, j * ATTN_TK:(j + 1) * ATTN_TK]
    else:
        k_ref[0] = k
        v_ref[0] = v
        qb_ref[0] = q.astype(BF16)


def _inproj(x, g_mix, w_in_bf, gq_row, gk_row, seg, *, emit_t):
    b, t, d = x.shape
    cols = gq_row.shape[1]
    tm = min(ROW_TILE, t)
    nt = t // tm
    row = lambda width: pl.BlockSpec((1, tm, width), lambda bi, ti: (bi, ti, 0))
    f32_out = jax.ShapeDtypeStruct((b, t, cols), F32)
    if emit_t:
        nq, nk = tm // ATTN_TQ, tm // ATTN_TK
        out_shape = [jax.ShapeDtypeStruct((b, t // ATTN_TQ, cols, ATTN_TQ), BF16),
                     jax.ShapeDtypeStruct((b, t, cols), BF16),
                     jax.ShapeDtypeStruct((b, t // ATTN_TK, cols, ATTN_TK), BF16),
                     jax.ShapeDtypeStruct((b, cols, t), F32),
                     jax.ShapeDtypeStruct((b, t * N_HEADS, V_DIM), F32),
                     f32_out]
        out_specs = [pl.BlockSpec((1, nq, cols, ATTN_TQ), lambda bi, ti: (bi, ti, 0, 0)),
                     row(cols),
                     pl.BlockSpec((1, nk, cols, ATTN_TK), lambda bi, ti: (bi, ti, 0, 0)),
                     pl.BlockSpec((1, cols, tm), lambda bi, ti: (bi, 0, ti)),
                     pl.BlockSpec((1, tm * N_HEADS, V_DIM), lambda bi, ti: (bi, ti, 0)),
                     row(cols)]
    else:
        out_shape = [jax.ShapeDtypeStruct((b, t, cols), BF16)] + [f32_out] * 3
        out_specs = [row(cols)] * 4
    return pl.pallas_call(
        functools.partial(_inproj_kernel, qk_cols=cols, emit_t=emit_t),
        grid=(b, nt),
        in_specs=[row(d), _const_spec((1, d)), _const_spec(w_in_bf.shape),
                  _const_spec((1, cols)), _const_spec((1, cols)), _const_spec((cols, cols))],
        out_specs=out_specs,
        out_shape=out_shape,
        compiler_params=_cparams(("parallel", "parallel")),
        name="inproj",
    )(x, g_mix.reshape(1, d), w_in_bf, gq_row, gk_row, seg)


def _attn_prompt_kernel(sc_ref, qt_ref, kb_ref, vt_ref, bias_ref, gsub_ref, o_ref,
                        qbd_ref, m_ref, l_ref, acc_ref, sa_ref, sb_ref, mca_ref, mcb_ref):
    i = pl.program_id(1)
    tq, tk = ATTN_TQ, ATTN_TK
    lam = sc_ref[N_HEADS]
    PREV, DIAG, FAR = 0, 1, None
    slot_a, slot_b = (sa_ref, mca_ref), (sb_ref, mcb_ref)

    upper = lax.broadcasted_iota(jnp.int32, (HEAD_COLS, tq), 0) < HEAD_DIM
    for h in range(N_HEADS):
        qh = qt_ref[0, 0, h * HEAD_COLS:(h + 1) * HEAD_COLS, :]
        zero = jnp.zeros_like(qh)
        qbd_ref[h, :, :tq] = jnp.where(upper, qh, zero)
        qbd_ref[h, :, tq:] = jnp.where(upper, zero, qh)
    m_ref[...] = jnp.full(m_ref.shape, NEG_INF, F32)
    l_ref[...] = jnp.zeros(l_ref.shape, F32)
    acc_ref[...] = jnp.zeros(acc_ref.shape, F32)

    def stage(j, slot, tile):
        s_ref, mc_ref = slot
        row0 = pl.multiple_of(j * tk, tk)
        for h in range(N_HEADS):
            kh = kb_ref[0, pl.ds(row0, tk), h * HEAD_COLS:(h + 1) * HEAD_COLS]
            s = jnp.dot(kh, qbd_ref[h], preferred_element_type=F32)
            if tile is not FAR:
                s = s + bias_ref[h, tile]
            s_ref[h] = s
            mc_ref[h] = jnp.max(s, axis=0, keepdims=True)

    def consume(j, slot, tile):
        s_ref, mc_ref = slot
        for h in range(N_HEADS):
            m_prev = m_ref[h]
            if tile is FAR:
                shift = sc_ref[h]
                m_new = jnp.maximum(m_prev, mc_ref[h] + shift)
                p = jnp.exp2(s_ref[h] - (m_new - shift))
            else:
                m_new = jnp.maximum(m_prev, mc_ref[h])
                p = jnp.exp2(s_ref[h] - m_new)
            alpha = jnp.exp2(m_prev - m_new)
            l_ref[h] = alpha * l_ref[h] + jnp.sum(p, axis=0, keepdims=True)
            vth = vt_ref[0, j, h * V_DIM:(h + 1) * V_DIM, :]
            pv = jnp.dot(vth, p.astype(BF16), preferred_element_type=F32)
            acc_ref[h] = alpha * acc_ref[h] + pv
            m_ref[h] = m_new

    n_far = jnp.maximum(i - 1, 0)
    stage(i, slot_a, DIAG)

    @pl.when(i == 0)
    def _():
        consume(i, slot_a, DIAG)

    @pl.when(i >= 1)
    def _():
        consume(i, slot_a, DIAG)
        stage(i - 1, slot_b, PREV)

    @pl.when(i == 1)
    def _():
        consume(i - 1, slot_b, PREV)

    @pl.when(i >= 2)
    def _():
        consume(i - 1, slot_b, PREV)
        stage(0, slot_a, FAR)

    n_pairs = jnp.maximum(n_far - 1, 0) // 2

    def pair_body(jj, carry):
        f = 2 * jj
        consume(f, slot_a, FAR)
        stage(f + 1, slot_b, FAR)
        consume(f + 1, slot_b, FAR)
        stage(f + 2, slot_a, FAR)
        return carry

    lax.fori_loop(0, n_pairs, pair_body, 0)
    f_last = 2 * n_pairs
    left = n_far - f_last

    @pl.when(left == 1)
    def _():
        consume(f_last, slot_a, FAR)

    @pl.when(left == 2)
    def _():
        consume(f_last, slot_a, FAR)
        stage(f_last + 1, slot_b, FAR)

    @pl.when(left == 2)
    def _():
        consume(f_last + 1, slot_b, FAR)

    for h in range(N_HEADS):
        acc = acc_ref[h]
        r = 1.0 / l_ref[h]
        ot = acc[:, :tq] * r[:, :tq] - lam * (acc[:, tq:] * r[:, tq:])
        ms = jnp.mean(ot * ot, axis=0, keepdims=True)
        ot = ot * lax.rsqrt(ms + EPS)
        o_ref[0, :, h * V_DIM:(h + 1) * V_DIM] = (ot.T * gsub_ref[...]).astype(BF16)


def _attn_prompt(scalars, qt, kb, vt, bias_t, gsub_row):
    b, nq, cols, tq = qt.shape
    t = kb.shape[1]
    nk = vt.shape[1]
    return pl.pallas_call(
        _attn_prompt_kernel,
        grid=(b, nq),
        in_specs=[pl.BlockSpec(memory_space=pltpu.SMEM),
                  pl.BlockSpec((1, 1, cols, tq), lambda bi, i: (bi, i, 0, 0)),
                  pl.BlockSpec((1, t, cols), lambda bi, i: (bi, 0, 0)),
                  pl.BlockSpec((1, nk, cols, ATTN_TK), lambda bi, i: (bi, 0, 0, 0)),
                  _const_spec(bias_t.shape), _const_spec((1, V_DIM))],
        out_specs=pl.BlockSpec((1, tq, cols), lambda bi, i: (bi, i, 0)),
        out_shape=jax.ShapeDtypeStruct((b, t, cols), BF16),
        scratch_shapes=[pltpu.VMEM((N_HEADS, HEAD_COLS, 2 * tq), BF16),
                        pltpu.VMEM((N_HEADS, 1, 2 * tq), F32),
                        pltpu.VMEM((N_HEADS, 1, 2 * tq), F32),
                        pltpu.VMEM((N_HEADS, V_DIM, 2 * tq), F32),
                        pltpu.VMEM((N_HEADS, ATTN_TK, 2 * tq), F32),
                        pltpu.VMEM((N_HEADS, ATTN_TK, 2 * tq), F32),
                        pltpu.VMEM((N_HEADS, 1, 2 * tq), F32),
                        pltpu.VMEM((N_HEADS, 1, 2 * tq), F32)],
        compiler_params=_cparams(("parallel", "arbitrary")),
        name="attn_prompt",
    )(scalars, qt, kb, vt, bias_t, gsub_row)


def _attn_sample_kernel(sc_ref, q_ref, kt_ref, vc_ref, ktn_ref, vn_ref, bc_ref, bn_ref, gsub_ref,
                        o_ref, qbd_ref, m_ref, l_ref, acc_ref):
    j = pl.program_id(1)
    tq, cols = q_ref.shape[1], q_ref.shape[2]
    lam = sc_ref[N_HEADS]

    @pl.when(j == 0)
    def _():
        q = q_ref[0].astype(F32)
        col = lax.broadcasted_iota(jnp.int32, (tq, cols), 1)
        for hm in range(cols // HEAD_DIM):
            mine = (col >= hm * HEAD_DIM) & (col < (hm + 1) * HEAD_DIM)
            qbd_ref[hm * tq:(hm + 1) * tq, :] = jnp.where(mine, q, 0.0)
        m_ref[...] = jnp.full(m_ref.shape, NEG_INF, F32)
        l_ref[...] = jnp.zeros(l_ref.shape, F32)
        acc_ref[...] = jnp.zeros(acc_ref.shape, F32)

    def update(kt, v_of_head, bias):
        s = jnp.dot(qbd_ref[...].astype(BF16), kt, preferred_element_type=F32) + bias
        m_prev = m_ref[...]
        m_new = jnp.maximum(m_prev, jnp.max(s, axis=-1, keepdims=True))
        alpha = jnp.exp2(m_prev - m_new)
        p = jnp.exp2(s - m_new)
        l_ref[...] = alpha * l_ref[...] + jnp.sum(p, axis=-1, keepdims=True)
        pb = p.astype(BF16)
        for h in range(N_HEADS):
            rows = slice(h * 2 * tq, (h + 1) * 2 * tq)
            acc_ref[rows, :] = alpha[rows] * acc_ref[rows, :] + jnp.dot(
                pb[rows], v_of_head(h), preferred_element_type=F32)
        m_ref[...] = m_new

    tk = kt_ref.shape[2]
    update(kt_ref[0].astype(BF16),
           lambda h: vc_ref[0, pl.ds(h, tk, stride=N_HEADS), :].astype(BF16), bc_ref[j])

    @pl.when(j == pl.num_programs(1) - 1)
    def _():
        update(ktn_ref[0], lambda h: vn_ref[0, :, h * V_DIM:(h + 1) * V_DIM], bn_ref[...])
        for h in range(N_HEADS):
            r0 = h * 2 * tq
            inv0 = 1.0 / l_ref[r0:r0 + tq, :]
            inv1 = 1.0 / l_ref[r0 + tq:r0 + 2 * tq, :]
            o = acc_ref[r0:r0 + tq, :] * inv0 - lam * (acc_ref[r0 + tq:r0 + 2 * tq, :] * inv1)
            o_ref[0, :, h * V_DIM:(h + 1) * V_DIM] = _rms(o, gsub_ref[...]).astype(BF16)


def _attn_sample(scalars, q, kt, v4, ktn, vn, bias_c, bias_n, gsub_row):
    b, tq, cols = q.shape
    p = kt.shape[2]
    nkc = p // CACHE_TK
    rows = 2 * N_HEADS * tq
    return pl.pallas_call(
        _attn_sample_kernel,
        grid=(b, nkc),
        in_specs=[pl.BlockSpec(memory_space=pltpu.SMEM),
                  pl.BlockSpec((1, tq, cols), lambda bi, j: (bi, 0, 0)),
                  pl.BlockSpec((1, cols, CACHE_TK), lambda bi, j: (bi, 0, j)),
                  pl.BlockSpec((1, CACHE_TK * N_HEADS, V_DIM), lambda bi, j: (bi, j, 0)),
                  pl.BlockSpec((1, cols, NEW_KEY_PAD), lambda bi, j: (bi, 0, 0)),
                  pl.BlockSpec((1, NEW_KEY_PAD, cols), lambda bi, j: (bi, 0, 0)),
                  _const_spec(bias_c.shape), _const_spec(bias_n.shape), _const_spec((1, V_DIM))],
        out_specs=pl.BlockSpec((1, tq, cols), lambda bi, j: (bi, 0, 0)),
        out_shape=jax.ShapeDtypeStruct((b, tq, cols), BF16),
        scratch_shapes=[pltpu.VMEM((rows, cols), F32),
                        pltpu.VMEM((rows, 1), F32),
                        pltpu.VMEM((rows, 1), F32),
                        pltpu.VMEM((rows, V_DIM), F32)],
        compiler_params=_cparams(("parallel", "arbitrary")),
        name="attn_sample",
    )(scalars, q, kt, v4, ktn, vn, bias_c, bias_n, gsub_row)


def _mix_kernel(x_ref, o_ref, c_ref, hist_ref, wconv_ref, bconv_ref, lng_ref, lnb_ref,
                woo_ref, woc_ref, gcross_ref, wmq_ref, gmq_ref, mk_ref, mv_ref, wmo_ref,
                out_ref, ext_ref, cv_ref, ca_ref):
    bb, tt, d = x_ref.shape
    rows = bb * tt
    n_mem = mk_ref.shape[1] // MEM_HEADS
    rc = min(CONV_ROWS, tt)
    lead = CONV_HALO - (CONV_K - 1)

    for b in range(bb):
        ext_ref[b, :CONV_HALO, :] = hist_ref[b, 0]
        ext_ref[b, CONV_HALO:, :] = c_ref[b]
    for b in range(bb):
        for r0 in range(0, tt, rc):
            acc = jnp.zeros((rc, c_ref.shape[2]), F32) + bconv_ref[...]
            for k in range(CONV_K):
                acc = acc + wconv_ref[k:k + 1, :] * ext_ref[b, r0 + lead + k:r0 + lead + k + rc, :]
            mu = jnp.mean(acc, axis=-1, keepdims=True)
            xc = acc - mu
            var = jnp.mean(xc * xc, axis=-1, keepdims=True)
            y = xc * lax.rsqrt(var + EPS) * lng_ref[...] + lnb_ref[...]
            cv_ref[b * tt + r0:b * tt + r0 + rc, :] = (y * jax.nn.sigmoid(y)).astype(BF16)

    x = x_ref[...].reshape(rows, d)
    o = o_ref[...].reshape(rows, o_ref.shape[2])
    x1 = (x + jnp.dot(o, woo_ref[...], preferred_element_type=F32)
          + jnp.dot(cv_ref[...], woc_ref[...], preferred_element_type=F32))

    hc = _rms(x1, gcross_ref[...]).astype(BF16)
    qm = jnp.dot(hc, wmq_ref[...], preferred_element_type=F32)
    for h in range(MEM_HEADS):
        cols = slice(h * MEM_HEAD_DIM, (h + 1) * MEM_HEAD_DIM)
        qn = _rms(qm[:, cols], gmq_ref[...]).astype(BF16)
        for b in range(bb):
            mem_rows = pl.ds(h, n_mem, stride=MEM_HEADS)
            mk = mk_ref[b, mem_rows, :].astype(BF16)
            mv = mv_ref[b, mem_rows, :].astype(BF16)
            s = lax.dot_general(qn[b * tt:(b + 1) * tt], mk, (((1,), (1,)), ((), ())),
                                preferred_element_type=F32)
            p = jnp.exp2(s - jnp.max(s, axis=-1, keepdims=True))
            l = jnp.sum(p, axis=-1, keepdims=True)
            oh = jnp.dot(p.astype(BF16), mv, preferred_element_type=F32) / l
            ca_ref[b * tt:(b + 1) * tt, cols] = oh.astype(BF16)
    x2 = x1 + jnp.dot(ca_ref[...], wmo_ref[...], preferred_element_type=F32)
    out_ref[...] = x2.reshape(bb, tt, d)


def _mix(x, o, c, hist, wconv, bconv, lng, lnb, woo, woc, gcross, wmq, gmq_row, mk, mv, wmo,
         *, bb, tt):
    b, t, d = x.shape
    cw = c.shape[2]
    mem_rows, mhd = mk.shape[1], mk.shape[2]
    mw = wmq.shape[1]
    rows = bb * tt
    tile = lambda width: pl.BlockSpec((bb, tt, width), lambda bi, ti: (bi, ti, 0))
    return pl.pallas_call(
        _mix_kernel,
        grid=(b // bb, t // tt),
        in_specs=[tile(d), tile(cw), tile(cw),
                  pl.BlockSpec((bb, 1, CONV_HALO, cw), lambda bi, ti: (bi, ti, 0, 0)),
                  _const_spec(wconv.shape), _const_spec((1, cw)), _const_spec((1, cw)),
                  _const_spec((1, cw)), _const_spec(woo.shape), _const_spec(woc.shape),
                  _const_spec((1, d)), _const_spec(wmq.shape), _const_spec((1, MEM_HEAD_DIM)),
                  pl.BlockSpec((bb, mem_rows, mhd), lambda bi, ti: (bi, 0, 0)),
                  pl.BlockSpec((bb, mem_rows, mhd), lambda bi, ti: (bi, 0, 0)),
                  _const_spec(wmo.shape)],
        out_specs=tile(d),
        out_shape=jax.ShapeDtypeStruct((b, t, d), F32),
        scratch_shapes=[pltpu.VMEM((bb, CONV_HALO + tt, cw), F32),
                        pltpu.VMEM((rows, cw), BF16),
                        pltpu.VMEM((rows, mw), BF16)],
        compiler_params=_cparams(("parallel", "parallel")),
        name="mix",
    )(x, o, c, hist, wconv, bconv, lng, lnb, woo, woc, gcross, wmq, gmq_row, mk, mv, wmo)


def _mlp_kernel(x_ref, g_ref, w1_ref, w2_ref, out_ref):
    x = x_ref[...]
    hf = _rms(x, g_ref[...]).astype(BF16)
    u = jnp.maximum(jnp.dot(hf, w1_ref[...], preferred_element_type=F32), 0.0)
    out_ref[...] = x + jnp.dot((u * u).astype(BF16), w2_ref[...], preferred_element_type=F32)


def _mlp(x2d, g_ffn, w1, w2):
    n, d = x2d.shape
    tm = min(ROW_TILE, n)
    return pl.pallas_call(
        _mlp_kernel,
        grid=(n // tm,),
        in_specs=[pl.BlockSpec((tm, d), lambda i: (i, 0)), _const_spec((1, d)),
                  _const_spec(w1.shape), _const_spec(w2.shape)],
        out_specs=pl.BlockSpec((tm, d), lambda i: (i, 0)),
        out_shape=jax.ShapeDtypeStruct((n, d), F32),
        compiler_params=_cparams(("parallel",)),
        name="mlp",
    )(x2d, g_ffn.reshape(1, d), w1, w2)


def _rel_bucket(rel):
    half = N_BUCKETS // 2
    max_exact = half // 2
    ret = jnp.where(rel > 0, half, 0)
    n = jnp.abs(rel)
    nf = jnp.maximum(n, 1).astype(jnp.float32)
    large = max_exact + (jnp.log(nf / max_exact) / math.log(MAX_DISTANCE / max_exact)
                         * (half - max_exact)).astype(jnp.int32)
    large = jnp.minimum(large, half - 1)
    return ret + jnp.where(n < max_exact, n, large)


def _masked_bias(rel_table, q_pos, k_pos):
    bucket = _rel_bucket(k_pos[None, :] - q_pos[:, None])[None]
    table = rel_table.astype(F32) * LOG2E
    bias = jnp.zeros((rel_table.shape[1],) + bucket.shape[1:], F32)
    for bkt in range(N_BUCKETS):
        bias = jnp.where(bucket == bkt, table[bkt][:, None, None], bias)
    mask = (k_pos[None, :] // CHUNK) <= (q_pos[:, None] // CHUNK)
    return jnp.where(mask[None], bias, NEG_INF)


def _far_bucket_is_saturated(min_distance):
    half = N_BUCKETS // 2
    max_exact = half // 2
    large = max_exact + int(np.log(min_distance / max_exact) / math.log(MAX_DISTANCE / max_exact)
                            * (half - max_exact) * (1 - 1e-6))
    return large >= half - 1


def _layer(x, k_past, v_past, c_past, mk, mv, w, lam, lam_init):
    b, t, d = x.shape
    prompt = k_past is None
    cols = N_HEADS * HEAD_COLS
    gsub_row = (w["g_sub"] * (1.0 - lam_init)).reshape(1, V_DIM)

    if prompt:
        assert t % ROW_TILE == 0 and ROW_TILE % ATTN_TQ == 0 and ATTN_TQ == ATTN_TK
        assert ATTN_TK % CHUNK == 0 and _far_bucket_is_saturated(ATTN_TK + 1)
        qt, kb, vt, kt, v4, c = _inproj(x, w["g_mix"], w["w_in"], w["gq_row"], w["gk_row"],
                                        w["seg"], emit_t=True)
        k_out = jnp.transpose(kt.reshape(b, N_HEADS, 2, HEAD_DIM, t), (0, 4, 1, 2, 3))
        v_out = v4.reshape(b, t, N_HEADS, V_DIM)
        q_pos = ATTN_TQ + jnp.arange(ATTN_TQ, dtype=jnp.int32)
        near = _masked_bias(w["rel_table"], q_pos, jnp.arange(2 * ATTN_TK, dtype=jnp.int32))
        near = jnp.stack([near[:, :, :ATTN_TK], near[:, :, ATTN_TK:]], axis=1)
        near_t = jnp.swapaxes(near, 2, 3)
        bias_t = jnp.concatenate([near_t, near_t], axis=3)
        far = w["rel_table"][_rel_bucket(jnp.int32(-(ATTN_TK + 1)))].astype(F32) * LOG2E
        scalars = jnp.concatenate([far, lam.reshape(1)]).astype(F32)
        o = _attn_prompt(scalars, qt, kb, vt, bias_t, gsub_row)
        bb, tt = 1, ROW_TILE
    else:
        p = k_past.shape[2]
        assert p % CACHE_TK == 0 and t <= NEW_KEY_PAD and t % 16 == 0
        qb, k, v, c = _inproj(x.reshape(1, b * t, d), w["g_mix"], w["w_in"], w["gq_row"],
                              w["gk_row"], w["seg"], emit_t=False)
        qb, k, v, c = (a.reshape(b, t, cols) for a in (qb, k, v, c))
        k_out = k.reshape(b, t, N_HEADS, 2, HEAD_DIM)
        v_out = v.reshape(b, t, N_HEADS, V_DIM)
        ktn = jnp.pad(jnp.swapaxes(k, 1, 2).astype(BF16), ((0, 0), (0, 0), (0, NEW_KEY_PAD - t)))
        vn = jnp.pad(v.astype(BF16), ((0, 0), (0, NEW_KEY_PAD - t), (0, 0)))
        q_pos = p + jnp.arange(t, dtype=jnp.int32)
        bias = _masked_bias(w["rel_table"], q_pos, jnp.arange(p + NEW_KEY_PAD, dtype=jnp.int32))
        bias = jnp.where(jnp.arange(p + NEW_KEY_PAD) < p + t, bias, NEG_INF)
        rows = 2 * N_HEADS * t
        bias = jnp.broadcast_to(bias[:, None], (N_HEADS, 2, t, p + NEW_KEY_PAD)).reshape(rows, -1)
        bias_c = jnp.swapaxes(bias[:, :p].reshape(rows, p // CACHE_TK, CACHE_TK), 0, 1)
        scalars = jnp.concatenate([jnp.zeros((N_HEADS,), F32), lam.reshape(1)]).astype(F32)
        o = _attn_sample(scalars, qb, k_past, v_past, ktn, vn, bias_c, bias[:, p:], gsub_row)
        bb, tt = SAMPLE_BATCH_TILE, t
        assert b % bb == 0

    nt = t // tt
    first = jnp.pad(c_past, ((0, 0), (CONV_HALO - (CONV_K - 1), 0), (0, 0)))[:, None]
    if nt > 1:
        tails = c.reshape(b, nt, tt, c.shape[2])[:, :-1, tt - CONV_HALO:, :]
        hist = jnp.concatenate([first, tails], axis=1)
    else:
        hist = first
    x2 = _mix(x, o, c, hist, w["w_conv"], w["b_conv"], w["ln_g"], w["ln_b"], w["w_out_o"],
              w["w_out_c"], w["g_cross"], w["w_mq"], w["gmq_row"], mk, mv, w["w_mo"], bb=bb, tt=tt)
    y = _mlp(x2.reshape(b * t, d), w["g_ffn"], w["w_ff1"], w["w_ff2"]).reshape(b, t, d)
    if t >= CONV_K - 1:
        c_hist_tail = c[:, t - (CONV_K - 1):]
    else:
        c_hist_tail = jnp.concatenate([c_past[:, t:], c], axis=1)
    return y, k_out, v_out, c_hist_tail


def kernel(x_prompt, x_sample, cache_k, cache_v, cache_conv, cache_mem_k, cache_mem_v, mem_prompt,
           rel_table, g_mix, w_in, g_q, g_k, lam_vec, g_sub, w_conv, b_conv, ln_g, ln_b, w_out,
           g_cross, g_mem, w_mq, w_mk, w_mv, g_mq, g_mk, w_mo, g_ffn, w_ff1, w_ff2):
    depth = g_mix.shape[0]
    assert depth == 1
    b, t, d = x_prompt.shape
    bs, ts, _ = x_sample.shape
    cols = N_HEADS * HEAD_COLS
    cw = w_conv.shape[2]
    attn_w = N_HEADS * V_DIM
    l = 0
    lam_init = 0.8 - 0.6 * math.exp(-0.3 * l)
    lp = lam_vec[l].astype(F32)
    lam = jnp.exp(jnp.sum(lp[0] * lp[1])) - jnp.exp(jnp.sum(lp[2] * lp[3])) + lam_init

    seg = jnp.kron(jnp.eye(cols // HEAD_DIM, dtype=F32),
                   jnp.full((HEAD_DIM, HEAD_DIM), 1.0 / HEAD_DIM, F32)).astype(BF16)
    n_maps = cols // HEAD_DIM
    w = dict(
        rel_table=rel_table, g_mix=g_mix[l], w_in=w_in[l].astype(BF16), seg=seg,
        gq_row=jnp.tile(g_q[l] * (HEAD_DIM ** -0.5 * LOG2E), n_maps).reshape(1, cols),
        gk_row=jnp.tile(g_k[l], n_maps).reshape(1, cols),
        g_sub=g_sub[l],
        w_conv=jnp.pad(w_conv[l], ((0, CONV_HALO - CONV_K), (0, 0))),
        b_conv=b_conv[l].reshape(1, cw), ln_g=ln_g[l].reshape(1, cw), ln_b=ln_b[l].reshape(1, cw),
        w_out_o=w_out[l][:attn_w].astype(BF16), w_out_c=w_out[l][attn_w:].astype(BF16),
        g_cross=g_cross[l].reshape(1, d), w_mq=w_mq[l].astype(BF16),
        gmq_row=(g_mq[l] * (MEM_HEAD_DIM ** -0.5 * LOG2E)).reshape(1, MEM_HEAD_DIM),
        w_mo=w_mo[l].astype(BF16), g_ffn=g_ffn[l],
        w_ff1=w_ff1[l].astype(BF16), w_ff2=w_ff2[l].astype(BF16),
    )

    n_mem = mem_prompt.shape[1]
    mk_p, mv_p = _memkv(mem_prompt.reshape(b * n_mem, d), g_mem[l], w_mk[l], w_mv[l], g_mk[l])
    mk_p = mk_p.reshape(b, n_mem * MEM_HEADS, MEM_HEAD_DIM)
    mv_p = mv_p.reshape(b, n_mem * MEM_HEADS, MEM_HEAD_DIM)

    zero_conv = jnp.zeros((b, CONV_K - 1, cw), F32)
    yp, kp, vp, cp = _layer(x_prompt, None, None, zero_conv, mk_p, mv_p, w, lam, lam_init)

    p = cache_k.shape[2]
    k_past = jnp.transpose(cache_k[l], (0, 2, 3, 4, 1)).reshape(bs, cols, p)
    v_past = cache_v[l].reshape(bs, p * N_HEADS, V_DIM)
    ys, kn, vn, cn = _layer(x_sample, k_past, v_past, cache_conv[l],
                            cache_mem_k[l].reshape(bs, n_mem * MEM_HEADS, MEM_HEAD_DIM),
                            cache_mem_v[l].reshape(bs, n_mem * MEM_HEADS, MEM_HEAD_DIM),
                            w, lam, lam_init)

    return (yp, ys, kp[None], vp[None], cp[None],
            mk_p.reshape(1, b, n_mem, MEM_HEADS, MEM_HEAD_DIM),
            mv_p.reshape(1, b, n_mem, MEM_HEADS, MEM_HEAD_DIM),
            kn[None], vn[None], cn[None])
```

```python
import functools
import math

import jax
import jax.numpy as jnp
import numpy as np
from jax import lax
from jax.experimental import pallas as pl
from jax.experimental.pallas import tpu as pltpu

F32 = jnp.float32
BF16 = jnp.bfloat16

CHUNK = 64
N_HEADS = 4
HEAD_DIM = 64
V_DIM = 2 * HEAD_DIM
HEAD_COLS = 2 * HEAD_DIM
CONV_K = 31
CONV_HALO = 32
N_BUCKETS = 32
MAX_DISTANCE = 128
MEM_HEADS = 4
MEM_HEAD_DIM = 128
EPS = 1e-6
NEG_INF = -1e30
LOG2E = math.log2(math.e)

ATTN_TQ = 256
ATTN_TK = 256
VT_ROWS = V_DIM + 16
CACHE_TK = 1024
NEW_KEY_PAD = 128
ROW_TILE = 512
CONV_ROWS = 64
SAMPLE_BATCH_TILE = 8
VMEM_LIMIT = 56 * 1024 * 1024


def _cparams(sem):
    return pltpu.CompilerParams(dimension_semantics=sem, vmem_limit_bytes=VMEM_LIMIT)


def _rms(x, g):
    ms = jnp.mean(x * x, axis=-1, keepdims=True)
    return x * lax.rsqrt(ms + EPS) * g


def _const_spec(shape):
    return pl.BlockSpec(shape, lambda *_: (0,) * len(shape), pipeline_mode=pl.Buffered(1))


def _memkv_kernel(mem_ref, g_ref, wk_ref, wv_ref, gk_ref, mk_ref, mv_ref):
    m = _rms(mem_ref[...], g_ref[...]).astype(BF16)
    zk = jnp.dot(m, wk_ref[...], preferred_element_type=F32)
    zv = jnp.dot(m, wv_ref[...], preferred_element_type=F32)
    tm = mem_ref.shape[0]
    for h in range(MEM_HEADS):
        sl = slice(h * MEM_HEAD_DIM, (h + 1) * MEM_HEAD_DIM)
        rows = pl.ds(h, tm, stride=MEM_HEADS)
        mk_ref[rows, :] = _rms(zk[:, sl], gk_ref[...])
        mv_ref[rows, :] = zv[:, sl]


def _memkv(mem2d, g_mem, w_mk, w_mv, g_mk):
    n, d = mem2d.shape
    w = w_mk.shape[1]
    tm = min(ROW_TILE, n)
    return pl.pallas_call(
        _memkv_kernel,
        grid=(n // tm,),
        in_specs=[pl.BlockSpec((tm, d), lambda i: (i, 0)), _const_spec((1, d)),
                  _const_spec((d, w)), _const_spec((d, w)), _const_spec((1, MEM_HEAD_DIM))],
        out_specs=[pl.BlockSpec((tm * MEM_HEADS, MEM_HEAD_DIM), lambda i: (i, 0))] * 2,
        out_shape=[jax.ShapeDtypeStruct((n * MEM_HEADS, MEM_HEAD_DIM), F32)] * 2,
        compiler_params=_cparams(("parallel",)),
        name="memkv",
    )(mem2d, g_mem.reshape(1, d), w_mk.astype(BF16), w_mv.astype(BF16),
      g_mk.reshape(1, MEM_HEAD_DIM))


def _inproj_kernel(x_ref, g_ref, w_ref, gq_ref, gk_ref, seg_ref, *out_refs, qk_cols, emit_t):
    if emit_t:
        qt_ref, kb_ref, vt_ref, kt_ref, v4_ref, c_ref = out_refs
    else:
        qb_ref, k_ref, v_ref, c_ref = out_refs
    h = _rms(x_ref[0], g_ref[...]).astype(BF16)

    def proj(lo, hi):
        return jnp.dot(h, w_ref[:, lo:hi], preferred_element_type=F32)

    def mapnorm(z, g):
        ms = jnp.dot((z * z).astype(BF16), seg_ref[...], preferred_element_type=F32)
        return z * lax.rsqrt(ms + EPS) * g

    c0 = qk_cols
    q = mapnorm(proj(0, c0), gq_ref[...])
    k = mapnorm(proj(c0, 2 * c0), gk_ref[...])
    v = proj(2 * c0, 3 * c0)
    a = proj(3 * c0, 4 * c0)
    gate = proj(4 * c0, 5 * c0)
    c_ref[0] = a * jax.nn.sigmoid(gate)
    if emit_t:
        kb_ref[0] = k.astype(BF16)
        kt_ref[0] = k.T
        tm = x_ref.shape[1]
        for hh in range(N_HEADS):
            v4_ref[0, pl.ds(hh, tm, stride=N_HEADS), :] = v[:, hh * V_DIM:(hh + 1) * V_DIM]
        qt = q.T.astype(BF16)
        vt = v.T.astype(BF16)
        for j in range(qt_ref.shape[1]):
            qt_ref[0, j] = qt[:, j * ATTN_TQ:(j + 1) * ATTN_TQ]
        ones_row = (lax.broadcasted_iota(jnp.int32, (VT_ROWS - V_DIM, ATTN_TK), 0) == 0).astype(BF16)
        for j in range(vt_ref.shape[1]):
            for hh in range(N_HEADS):
                vt_ref[0, j, hh, :V_DIM, :] = vt[hh * V_DIM:(hh + 1) * V_DIM,
                                                 j * ATTN_TK:(j + 1) * ATTN_TK]
                vt_ref[0, j, hh, V_DIM:, :] = ones_row
    else:
        k_ref[0] = k
        v_ref[0] = v
        qb_ref[0] = q.astype(BF16)


def _inproj(x, g_mix, w_in_bf, gq_row, gk_row, seg, *, emit_t):
    b, t, d = x.shape
    cols = gq_row.shape[1]
    tm = min(ROW_TILE, t)
    nt = t // tm
    row = lambda width: pl.BlockSpec((1, tm, width), lambda bi, ti: (bi, ti, 0))
    f32_out = jax.ShapeDtypeStruct((b, t, cols), F32)
    if emit_t:
        nq, nk = tm // ATTN_TQ, tm // ATTN_TK
        out_shape = [jax.ShapeDtypeStruct((b, t // ATTN_TQ, cols, ATTN_TQ), BF16),
                     jax.ShapeDtypeStruct((b, t, cols), BF16),
                     jax.ShapeDtypeStruct((b, t // ATTN_TK, N_HEADS, VT_ROWS, ATTN_TK), BF16),
                     jax.ShapeDtypeStruct((b, cols, t), F32),
                     jax.ShapeDtypeStruct((b, t * N_HEADS, V_DIM), F32),
                     f32_out]
        out_specs = [pl.BlockSpec((1, nq, cols, ATTN_TQ), lambda bi, ti: (bi, ti, 0, 0)),
                     row(cols),
                     pl.BlockSpec((1, nk, N_HEADS, VT_ROWS, ATTN_TK),
                                  lambda bi, ti: (bi, ti, 0, 0, 0)),
                     pl.BlockSpec((1, cols, tm), lambda bi, ti: (bi, 0, ti)),
                     pl.BlockSpec((1, tm * N_HEADS, V_DIM), lambda bi, ti: (bi, ti, 0)),
                     row(cols)]
    else:
        out_shape = [jax.ShapeDtypeStruct((b, t, cols), BF16)] + [f32_out] * 3
        out_specs = [row(cols)] * 4
    return pl.pallas_call(
        functools.partial(_inproj_kernel, qk_cols=cols, emit_t=emit_t),
        grid=(b, nt),
        in_specs=[row(d), _const_spec((1, d)), _const_spec(w_in_bf.shape),
                  _const_spec((1, cols)), _const_spec((1, cols)), _const_spec((cols, cols))],
        out_specs=out_specs,
        out_shape=out_shape,
        compiler_params=_cparams(("parallel", "parallel")),
        name="inproj",
    )(x, g_mix.reshape(1, d), w_in_bf, gq_row, gk_row, seg)


def _attn_prompt_kernel(sc_ref, qt_ref, kb_ref, vt_ref, bias_ref, gsub_ref, o_ref,
                        qbd_ref, m_ref, acc_ref, sa_ref, sb_ref, mca_ref, mcb_ref):
    i = pl.program_id(1)
    tq, tk = ATTN_TQ, ATTN_TK
    lam = sc_ref[N_HEADS]
    PREV, DIAG, FAR = 0, 1, None
    slot_a, slot_b = (sa_ref, mca_ref), (sb_ref, mcb_ref)

    upper = lax.broadcasted_iota(jnp.int32, (HEAD_COLS, tq), 0) < HEAD_DIM
    for h in range(N_HEADS):
        qh = qt_ref[0, 0, h * HEAD_COLS:(h + 1) * HEAD_COLS, :]
        zero = jnp.zeros_like(qh)
        qbd_ref[h, :, :tq] = jnp.where(upper, qh, zero)
        qbd_ref[h, :, tq:] = jnp.where(upper, zero, qh)
    m_ref[...] = jnp.full(m_ref.shape, NEG_INF, F32)
    acc_ref[...] = jnp.zeros(acc_ref.shape, F32)

    def stage(j, slot, tile):
        s_ref, mc_ref = slot
        row0 = pl.multiple_of(j * tk, tk)
        for h in range(N_HEADS):
            kh = kb_ref[0, pl.ds(row0, tk), h * HEAD_COLS:(h + 1) * HEAD_COLS]
            s = jnp.dot(kh, qbd_ref[h], preferred_element_type=F32)
            if tile is not FAR:
                s = s + bias_ref[h, tile]
            s_ref[h] = s
            mc_ref[h] = jnp.max(s, axis=0, keepdims=True)

    def consume(j, slot, tile):
        s_ref, mc_ref = slot
        for h in range(N_HEADS):
            m_prev = m_ref[h]
            if tile is FAR:
                shift = sc_ref[h]
                m_new = jnp.maximum(m_prev, mc_ref[h] + shift)
                p = jnp.exp2(s_ref[h] - (m_new - shift))
            else:
                m_new = jnp.maximum(m_prev, mc_ref[h])
                p = jnp.exp2(s_ref[h] - m_new)
            alpha = jnp.exp2(m_prev - m_new)
            pv = jnp.dot(vt_ref[0, j, h], p.astype(BF16), preferred_element_type=F32)
            acc_ref[h] = alpha * acc_ref[h] + pv
            m_ref[h] = m_new

    n_far = jnp.maximum(i - 1, 0)
    stage(i, slot_a, DIAG)

    @pl.when(i == 0)
    def _():
        consume(i, slot_a, DIAG)

    @pl.when(i >= 1)
    def _():
        consume(i, slot_a, DIAG)
        stage(i - 1, slot_b, PREV)

    @pl.when(i == 1)
    def _():
        consume(i - 1, slot_b, PREV)

    @pl.when(i >= 2)
    def _():
        consume(i - 1, slot_b, PREV)
        stage(0, slot_a, FAR)

    n_pairs = jnp.maximum(n_far - 1, 0) // 2

    def pair_body(jj, carry):
        f = 2 * jj
        consume(f, slot_a, FAR)
        stage(f + 1, slot_b, FAR)
        consume(f + 1, slot_b, FAR)
        stage(f + 2, slot_a, FAR)
        return carry

    lax.fori_loop(0, n_pairs, pair_body, 0)
    f_last = 2 * n_pairs
    left = n_far - f_last

    @pl.when(left == 1)
    def _():
        consume(f_last, slot_a, FAR)

    @pl.when(left == 2)
    def _():
        consume(f_last, slot_a, FAR)
        stage(f_last + 1, slot_b, FAR)

    @pl.when(left == 2)
    def _():
        consume(f_last + 1, slot_b, FAR)

    for h in range(N_HEADS):
        acc = acc_ref[h, :V_DIM, :]
        r = 1.0 / acc_ref[h, V_DIM:V_DIM + 1, :]
        ot = acc[:, :tq] * r[:, :tq] - lam * (acc[:, tq:] * r[:, tq:])
        ms = jnp.mean(ot * ot, axis=0, keepdims=True)
        ot = ot * lax.rsqrt(ms + EPS)
        o_ref[0, :, h * V_DIM:(h + 1) * V_DIM] = (ot.T * gsub_ref[...]).astype(BF16)


def _attn_prompt(scalars, qt, kb, vt, bias_t, gsub_row):
    b, nq, cols, tq = qt.shape
    t = kb.shape[1]
    nk = vt.shape[1]
    return pl.pallas_call(
        _attn_prompt_kernel,
        grid=(b, nq),
        in_specs=[pl.BlockSpec(memory_space=pltpu.SMEM),
                  pl.BlockSpec((1, 1, cols, tq), lambda bi, i: (bi, i, 0, 0)),
                  pl.BlockSpec((1, t, cols), lambda bi, i: (bi, 0, 0)),
                  pl.BlockSpec((1, nk, N_HEADS, VT_ROWS, ATTN_TK), lambda bi, i: (bi, 0, 0, 0, 0)),
                  _const_spec(bias_t.shape), _const_spec((1, V_DIM))],
        out_specs=pl.BlockSpec((1, tq, cols), lambda bi, i: (bi, i, 0)),
        out_shape=jax.ShapeDtypeStruct((b, t, cols), BF16),
        scratch_shapes=[pltpu.VMEM((N_HEADS, HEAD_COLS, 2 * tq), BF16),
                        pltpu.VMEM((N_HEADS, 1, 2 * tq), F32),
                        pltpu.VMEM((N_HEADS, VT_ROWS, 2 * tq), F32),
                        pltpu.VMEM((N_HEADS, ATTN_TK, 2 * tq), F32),
                        pltpu.VMEM((N_HEADS, ATTN_TK, 2 * tq), F32),
                        pltpu.VMEM((N_HEADS, 1, 2 * tq), F32),
                        pltpu.VMEM((N_HEADS, 1, 2 * tq), F32)],
        compiler_params=_cparams(("parallel", "arbitrary")),
        name="attn_prompt",
    )(scalars, qt, kb, vt, bias_t, gsub_row)


def _attn_sample_kernel(sc_ref, q_ref, kt_ref, vc_ref, ktn_ref, vn_ref, bc_ref, bn_ref, gsub_ref,
                        o_ref, qbd_ref, m_ref, l_ref, acc_ref):
    j = pl.program_id(1)
    tq, cols = q_ref.shape[1], q_ref.shape[2]
    lam = sc_ref[N_HEADS]

    @pl.when(j == 0)
    def _():
        q = q_ref[0].astype(F32)
        col = lax.broadcasted_iota(jnp.int32, (tq, cols), 1)
        for hm in range(cols // HEAD_DIM):
            mine = (col >= hm * HEAD_DIM) & (col < (hm + 1) * HEAD_DIM)
            qbd_ref[hm * tq:(hm + 1) * tq, :] = jnp.where(mine, q, 0.0)
        m_ref[...] = jnp.full(m_ref.shape, NEG_INF, F32)
        l_ref[...] = jnp.zeros(l_ref.shape, F32)
        acc_ref[...] = jnp.zeros(acc_ref.shape, F32)

    def update(kt, v_of_head, bias):
        s = jnp.dot(qbd_ref[...].astype(BF16), kt, preferred_element_type=F32) + bias
        m_prev = m_ref[...]
        m_new = jnp.maximum(m_prev, jnp.max(s, axis=-1, keepdims=True))
        alpha = jnp.exp2(m_prev - m_new)
        p = jnp.exp2(s - m_new)
        l_ref[...] = alpha * l_ref[...] + jnp.sum(p, axis=-1, keepdims=True)
        pb = p.astype(BF16)
        for h in range(N_HEADS):
            rows = slice(h * 2 * tq, (h + 1) * 2 * tq)
            acc_ref[rows, :] = alpha[rows] * acc_ref[rows, :] + jnp.dot(
                pb[rows], v_of_head(h), preferred_element_type=F32)
        m_ref[...] = m_new

    tk = kt_ref.shape[2]
    update(kt_ref[0].astype(BF16),
           lambda h: vc_ref[0, pl.ds(h, tk, stride=N_HEADS), :].astype(BF16), bc_ref[j])

    @pl.when(j == pl.num_programs(1) - 1)
    def _():
        update(ktn_ref[0], lambda h: vn_ref[0, :, h * V_DIM:(h + 1) * V_DIM], bn_ref[...])
        for h in range(N_HEADS):
            r0 = h * 2 * tq
            inv0 = 1.0 / l_ref[r0:r0 + tq, :]
            inv1 = 1.0 / l_ref[r0 + tq:r0 + 2 * tq, :]
            o = acc_ref[r0:r0 + tq, :] * inv0 - lam * (acc_ref[r0 + tq:r0 + 2 * tq, :] * inv1)
            o_ref[0, :, h * V_DIM:(h + 1) * V_DIM] = _rms(o, gsub_ref[...]).astype(BF16)


def _attn_sample(scalars, q, kt, v4, ktn, vn, bias_c, bias_n, gsub_row):
    b, tq, cols = q.shape
    p = kt.shape[2]
    nkc = p // CACHE_TK
    rows = 2 * N_HEADS * tq
    return pl.pallas_call(
        _attn_sample_kernel,
        grid=(b, nkc),
        in_specs=[pl.BlockSpec(memory_space=pltpu.SMEM),
                  pl.BlockSpec((1, tq, cols), lambda bi, j: (bi, 0, 0)),
                  pl.BlockSpec((1, cols, CACHE_TK), lambda bi, j: (bi, 0, j)),
                  pl.BlockSpec((1, CACHE_TK * N_HEADS, V_DIM), lambda bi, j: (bi, j, 0)),
                  pl.BlockSpec((1, cols, NEW_KEY_PAD), lambda bi, j: (bi, 0, 0)),
                  pl.BlockSpec((1, NEW_KEY_PAD, cols), lambda bi, j: (bi, 0, 0)),
                  _const_spec(bias_c.shape), _const_spec(bias_n.shape), _const_spec((1, V_DIM))],
        out_specs=pl.BlockSpec((1, tq, cols), lambda bi, j: (bi, 0, 0)),
        out_shape=jax.ShapeDtypeStruct((b, tq, cols), BF16),
        scratch_shapes=[pltpu.VMEM((rows, cols), F32),
                        pltpu.VMEM((rows, 1), F32),
                        pltpu.VMEM((rows, 1), F32),
                        pltpu.VMEM((rows, V_DIM), F32)],
        compiler_params=_cparams(("parallel", "arbitrary")),
        name="attn_sample",
    )(scalars, q, kt, v4, ktn, vn, bias_c, bias_n, gsub_row)


def _mix_kernel(x_ref, o_ref, c_ref, hist_ref, wconv_ref, bconv_ref, lng_ref, lnb_ref,
                woo_ref, woc_ref, gcross_ref, wmq_ref, gmq_ref, mk_ref, mv_ref, wmo_ref,
                out_ref, ext_ref, xs_ref, cv_ref, ca_ref):
    bb, tt, d = x_ref.shape
    rows = bb * tt
    n_mem = mk_ref.shape[1] // MEM_HEADS
    rc = min(CONV_ROWS, tt)
    lead = CONV_HALO - (CONV_K - 1)

    for b in range(bb):
        ext_ref[b, :CONV_HALO, :] = hist_ref[b, 0]
        ext_ref[b, CONV_HALO:, :] = c_ref[b]
    span = xs_ref.shape[2]
    for b in range(bb):
        for r in range(1, 8):
            xs_ref[r - 1, b] = ext_ref[b, r:r + span, :]
    for b in range(bb):
        for r0 in range(0, tt, rc):
            acc = jnp.zeros((rc, c_ref.shape[2]), F32) + bconv_ref[...]
            for k in range(CONV_K):
                a, r = divmod(k + lead, 8)
                lo = r0 + 8 * a
                src = ext_ref[b, lo:lo + rc, :] if r == 0 else xs_ref[r - 1, b, lo:lo + rc, :]
                acc = acc + wconv_ref[k:k + 1, :] * src
            mu = jnp.mean(acc, axis=-1, keepdims=True)
            xc = acc - mu
            var = jnp.mean(xc * xc, axis=-1, keepdims=True)
            y = xc * lax.rsqrt(var + EPS) * lng_ref[...] + lnb_ref[...]
            cv_ref[b * tt + r0:b * tt + r0 + rc, :] = (y * jax.nn.sigmoid(y)).astype(BF16)

    x = x_ref[...].reshape(rows, d)
    o = o_ref[...].reshape(rows, o_ref.shape[2])
    x1 = (x + jnp.dot(o, woo_ref[...], preferred_element_type=F32)
          + jnp.dot(cv_ref[...], woc_ref[...], preferred_element_type=F32))

    hc = _rms(x1, gcross_ref[...]).astype(BF16)
    qm = jnp.dot(hc, wmq_ref[...], preferred_element_type=F32)
    for h in range(MEM_HEADS):
        cols = slice(h * MEM_HEAD_DIM, (h + 1) * MEM_HEAD_DIM)
        qn = _rms(qm[:, cols], gmq_ref[...]).astype(BF16)
        for b in range(bb):
            mem_rows = pl.ds(h, n_mem, stride=MEM_HEADS)
            mk = mk_ref[b, mem_rows, :].astype(BF16)
            mv = mv_ref[b, mem_rows, :].astype(BF16)
            s = lax.dot_general(qn[b * tt:(b + 1) * tt], mk, (((1,), (1,)), ((), ())),
                                preferred_element_type=F32)
            p = jnp.exp2(s - jnp.max(s, axis=-1, keepdims=True))
            l = jnp.sum(p, axis=-1, keepdims=True)
            oh = jnp.dot(p.astype(BF16), mv, preferred_element_type=F32) / l
            ca_ref[b * tt:(b + 1) * tt, cols] = oh.astype(BF16)
    x2 = x1 + jnp.dot(ca_ref[...], wmo_ref[...], preferred_element_type=F32)
    out_ref[...] = x2.reshape(bb, tt, d)


def _mix(x, o, c, hist, wconv, bconv, lng, lnb, woo, woc, gcross, wmq, gmq_row, mk, mv, wmo,
         *, bb, tt):
    b, t, d = x.shape
    cw = c.shape[2]
    mem_rows, mhd = mk.shape[1], mk.shape[2]
    mw = wmq.shape[1]
    rows = bb * tt
    tile = lambda width: pl.BlockSpec((bb, tt, width), lambda bi, ti: (bi, ti, 0))
    return pl.pallas_call(
        _mix_kernel,
        grid=(b // bb, t // tt),
        in_specs=[tile(d), tile(cw), tile(cw),
                  pl.BlockSpec((bb, 1, CONV_HALO, cw), lambda bi, ti: (bi, ti, 0, 0)),
                  _const_spec(wconv.shape), _const_spec((1, cw)), _const_spec((1, cw)),
                  _const_spec((1, cw)), _const_spec(woo.shape), _const_spec(woc.shape),
                  _const_spec((1, d)), _const_spec(wmq.shape), _const_spec((1, MEM_HEAD_DIM)),
                  pl.BlockSpec((bb, mem_rows, mhd), lambda bi, ti: (bi, 0, 0)),
                  pl.BlockSpec((bb, mem_rows, mhd), lambda bi, ti: (bi, 0, 0)),
                  _const_spec(wmo.shape)],
        out_specs=tile(d),
        out_shape=jax.ShapeDtypeStruct((b, t, d), F32),
        scratch_shapes=[pltpu.VMEM((bb, CONV_HALO + tt, cw), F32),
                        pltpu.VMEM((7, bb, CONV_HALO - 8 + tt, cw), F32),
                        pltpu.VMEM((rows, cw), BF16),
                        pltpu.VMEM((rows, mw), BF16)],
        compiler_params=_cparams(("parallel", "parallel")),
        name="mix",
    )(x, o, c, hist, wconv, bconv, lng, lnb, woo, woc, gcross, wmq, gmq_row, mk, mv, wmo)


def _mlp_kernel(x_ref, g_ref, w1_ref, w2_ref, out_ref):
    x = x_ref[...]
    hf = _rms(x, g_ref[...]).astype(BF16)
    u = jnp.maximum(jnp.dot(hf, w1_ref[...], preferred_element_type=F32), 0.0)
    out_ref[...] = x + jnp.dot((u * u).astype(BF16), w2_ref[...], preferred_element_type=F32)


def _mlp(x2d, g_ffn, w1, w2):
    n, d = x2d.shape
    tm = min(ROW_TILE, n)
    return pl.pallas_call(
        _mlp_kernel,
        grid=(n // tm,),
        in_specs=[pl.BlockSpec((tm, d), lambda i: (i, 0)), _const_spec((1, d)),
                  _const_spec(w1.shape), _const_spec(w2.shape)],
        out_specs=pl.BlockSpec((tm, d), lambda i: (i, 0)),
        out_shape=jax.ShapeDtypeStruct((n, d), F32),
        compiler_params=_cparams(("parallel",)),
        name="mlp",
    )(x2d, g_ffn.reshape(1, d), w1, w2)


def _rel_bucket(rel):
    half = N_BUCKETS // 2
    max_exact = half // 2
    ret = jnp.where(rel > 0, half, 0)
    n = jnp.abs(rel)
    nf = jnp.maximum(n, 1).astype(jnp.float32)
    large = max_exact + (jnp.log(nf / max_exact) / math.log(MAX_DISTANCE / max_exact)
                         * (half - max_exact)).astype(jnp.int32)
    large = jnp.minimum(large, half - 1)
    return ret + jnp.where(n < max_exact, n, large)


def _masked_bias(rel_table, q_pos, k_pos):
    bucket = _rel_bucket(k_pos[None, :] - q_pos[:, None])[None]
    table = rel_table.astype(F32) * LOG2E
    bias = jnp.zeros((rel_table.shape[1],) + bucket.shape[1:], F32)
    for bkt in range(N_BUCKETS):
        bias = jnp.where(bucket == bkt, table[bkt][:, None, None], bias)
    mask = (k_pos[None, :] // CHUNK) <= (q_pos[:, None] // CHUNK)
    return jnp.where(mask[None], bias, NEG_INF)


def _far_bucket_is_saturated(min_distance):
    half = N_BUCKETS // 2
    max_exact = half // 2
    large = max_exact + int(np.log(min_distance / max_exact) / math.log(MAX_DISTANCE / max_exact)
                            * (half - max_exact) * (1 - 1e-6))
    return large >= half - 1


def _layer(x, k_past, v_past, c_past, mk, mv, w, lam, lam_init):
    b, t, d = x.shape
    prompt = k_past is None
    cols = N_HEADS * HEAD_COLS
    gsub_row = (w["g_sub"] * (1.0 - lam_init)).reshape(1, V_DIM)

    if prompt:
        assert t % ROW_TILE == 0 and ROW_TILE % ATTN_TQ == 0 and ATTN_TQ == ATTN_TK
        assert ATTN_TK % CHUNK == 0 and _far_bucket_is_saturated(ATTN_TK + 1)
        qt, kb, vt, kt, v4, c = _inproj(x, w["g_mix"], w["w_in"], w["gq_row"], w["gk_row"],
                                        w["seg"], emit_t=True)
        k_out = jnp.transpose(kt.reshape(b, N_HEADS, 2, HEAD_DIM, t), (0, 4, 1, 2, 3))
        v_out = v4.reshape(b, t, N_HEADS, V_DIM)
        q_pos = ATTN_TQ + jnp.arange(ATTN_TQ, dtype=jnp.int32)
        near = _masked_bias(w["rel_table"], q_pos, jnp.arange(2 * ATTN_TK, dtype=jnp.int32))
        near = jnp.stack([near[:, :, :ATTN_TK], near[:, :, ATTN_TK:]], axis=1)
        near_t = jnp.swapaxes(near, 2, 3)
        bias_t = jnp.concatenate([near_t, near_t], axis=3)
        far = w["rel_table"][_rel_bucket(jnp.int32(-(ATTN_TK + 1)))].astype(F32) * LOG2E
        scalars = jnp.concatenate([far, lam.reshape(1)]).astype(F32)
        o = _attn_prompt(scalars, qt, kb, vt, bias_t, gsub_row)
        bb, tt = 1, ROW_TILE
    else:
        p = k_past.shape[2]
        assert p % CACHE_TK == 0 and t <= NEW_KEY_PAD and t % 16 == 0
        qb, k, v, c = _inproj(x.reshape(1, b * t, d), w["g_mix"], w["w_in"], w["gq_row"],
                              w["gk_row"], w["seg"], emit_t=False)
        qb, k, v, c = (a.reshape(b, t, cols) for a in (qb, k, v, c))
        k_out = k.reshape(b, t, N_HEADS, 2, HEAD_DIM)
        v_out = v.reshape(b, t, N_HEADS, V_DIM)
        ktn = jnp.pad(jnp.swapaxes(k, 1, 2).astype(BF16), ((0, 0), (0, 0), (0, NEW_KEY_PAD - t)))
        vn = jnp.pad(v.astype(BF16), ((0, 0), (0, NEW_KEY_PAD - t), (0, 0)))
        q_pos = p + jnp.arange(t, dtype=jnp.int32)
        bias = _masked_bias(w["rel_table"], q_pos, jnp.arange(p + NEW_KEY_PAD, dtype=jnp.int32))
        bias = jnp.where(jnp.arange(p + NEW_KEY_PAD) < p + t, bias, NEG_INF)
        rows = 2 * N_HEADS * t
        bias = jnp.broadcast_to(bias[:, None], (N_HEADS, 2, t, p + NEW_KEY_PAD)).reshape(rows, -1)
        bias_c = jnp.swapaxes(bias[:, :p].reshape(rows, p // CACHE_TK, CACHE_TK), 0, 1)
        scalars = jnp.concatenate([jnp.zeros((N_HEADS,), F32), lam.reshape(1)]).astype(F32)
        o = _attn_sample(scalars, qb, k_past, v_past, ktn, vn, bias_c, bias[:, p:], gsub_row)
        bb, tt = SAMPLE_BATCH_TILE, t
        assert b % bb == 0

    nt = t // tt
    first = jnp.pad(c_past, ((0, 0), (CONV_HALO - (CONV_K - 1), 0), (0, 0)))[:, None]
    if nt > 1:
        tails = c.reshape(b, nt, tt, c.shape[2])[:, :-1, tt - CONV_HALO:, :]
        hist = jnp.concatenate([first, tails], axis=1)
    else:
        hist = first
    x2 = _mix(x, o, c, hist, w["w_conv"], w["b_conv"], w["ln_g"], w["ln_b"], w["w_out_o"],
              w["w_out_c"], w["g_cross"], w["w_mq"], w["gmq_row"], mk, mv, w["w_mo"], bb=bb, tt=tt)
    y = _mlp(x2.reshape(b * t, d), w["g_ffn"], w["w_ff1"], w["w_ff2"]).reshape(b, t, d)
    if t >= CONV_K - 1:
        c_hist_tail = c[:, t - (CONV_K - 1):]
    else:
        c_hist_tail = jnp.concatenate([c_past[:, t:], c], axis=1)
    return y, k_out, v_out, c_hist_tail


def kernel(x_prompt, x_sample, cache_k, cache_v, cache_conv, cache_mem_k, cache_mem_v, mem_prompt,
           rel_table, g_mix, w_in, g_q, g_k, lam_vec, g_sub, w_conv, b_conv, ln_g, ln_b, w_out,
           g_cross, g_mem, w_mq, w_mk, w_mv, g_mq, g_mk, w_mo, g_ffn, w_ff1, w_ff2):
    depth = g_mix.shape[0]
    assert depth == 1
    b, t, d = x_prompt.shape
    bs, ts, _ = x_sample.shape
    cols = N_HEADS * HEAD_COLS
    cw = w_conv.shape[2]
    attn_w = N_HEADS * V_DIM
    l = 0
    lam_init = 0.8 - 0.6 * math.exp(-0.3 * l)
    lp = lam_vec[l].astype(F32)
    lam = jnp.exp(jnp.sum(lp[0] * lp[1])) - jnp.exp(jnp.sum(lp[2] * lp[3])) + lam_init

    seg = jnp.kron(jnp.eye(cols // HEAD_DIM, dtype=F32),
                   jnp.full((HEAD_DIM, HEAD_DIM), 1.0 / HEAD_DIM, F32)).astype(BF16)
    n_maps = cols // HEAD_DIM
    w = dict(
        rel_table=rel_table, g_mix=g_mix[l], w_in=w_in[l].astype(BF16), seg=seg,
        gq_row=jnp.tile(g_q[l] * (HEAD_DIM ** -0.5 * LOG2E), n_maps).reshape(1, cols),
        gk_row=jnp.tile(g_k[l], n_maps).reshape(1, cols),
        g_sub=g_sub[l],
        w_conv=jnp.pad(w_conv[l], ((0, CONV_HALO - CONV_K), (0, 0))),
        b_conv=b_conv[l].reshape(1, cw), ln_g=ln_g[l].reshape(1, cw), ln_b=ln_b[l].reshape(1, cw),
        w_out_o=w_out[l][:attn_w].astype(BF16), w_out_c=w_out[l][attn_w:].astype(BF16),
        g_cross=g_cross[l].reshape(1, d), w_mq=w_mq[l].astype(BF16),
        gmq_row=(g_mq[l] * (MEM_HEAD_DIM ** -0.5 * LOG2E)).reshape(1, MEM_HEAD_DIM),
        w_mo=w_mo[l].astype(BF16), g_ffn=g_ffn[l],
        w_ff1=w_ff1[l].astype(BF16), w_ff2=w_ff2[l].astype(BF16),
    )

    n_mem = mem_prompt.shape[1]
    mk_p, mv_p = _memkv(mem_prompt.reshape(b * n_mem, d), g_mem[l], w_mk[l], w_mv[l], g_mk[l])
    mk_p = mk_p.reshape(b, n_mem * MEM_HEADS, MEM_HEAD_DIM)
    mv_p = mv_p.reshape(b, n_mem * MEM_HEADS, MEM_HEAD_DIM)

    zero_conv = jnp.zeros((b, CONV_K - 1, cw), F32)
    yp, kp, vp, cp = _layer(x_prompt, None, None, zero_conv, mk_p, mv_p, w, lam, lam_init)

    p = cache_k.shape[2]
    k_past = jnp.transpose(cache_k[l], (0, 2, 3, 4, 1)).reshape(bs, cols, p)
    v_past = cache_v[l].reshape(bs, p * N_HEADS, V_DIM)
    ys, kn, vn, cn = _layer(x_sample, k_past, v_past, cache_conv[l],
                            cache_mem_k[l].reshape(bs, n_mem * MEM_HEADS, MEM_HEAD_DIM),
                            cache_mem_v[l].reshape(bs, n_mem * MEM_HEADS, MEM_HEAD_DIM),
                            w, lam, lam_init)

    return (yp, ys, kp[None], vp[None], cp[None],
            mk_p.reshape(1, b, n_mem, MEM_HEADS, MEM_HEAD_DIM),
            mv_p.reshape(1, b, n_mem, MEM_HEADS, MEM_HEAD_DIM),
            kn[None], vn[None], cn[None])
```

```python
import functools
import math

import jax
import jax.numpy as jnp
import numpy as np
from jax import lax
from jax.experimental import pallas as pl
from jax.experimental.pallas import tpu as pltpu

F32 = jnp.float32
BF16 = jnp.bfloat16

CHUNK = 64
N_HEADS = 4
HEAD_DIM = 64
V_DIM = 2 * HEAD_DIM
HEAD_COLS = 2 * HEAD_DIM
CONV_K = 31
CONV_HALO = 32
N_BUCKETS = 32
MAX_DISTANCE = 128
MEM_HEADS = 4
MEM_HEAD_DIM = 128
EPS = 1e-6
NEG_INF = -1e30
LOG2E = math.log2(math.e)

ATTN_TQ = 256
ATTN_TK = 256
VT_ROWS = V_DIM + 16
MAX_UNSTABILISED_SCORE = 100.0
CACHE_TK = 1024
NEW_KEY_PAD = 128
ROW_TILE = 512
CONV_ROWS = 64
SAMPLE_BATCH_TILE = 8
VMEM_LIMIT = 56 * 1024 * 1024


def _cparams(sem):
    return pltpu.CompilerParams(dimension_semantics=sem, vmem_limit_bytes=VMEM_LIMIT)


def _rms(x, g):
    ms = jnp.mean(x * x, axis=-1, keepdims=True)
    return x * lax.rsqrt(ms + EPS) * g


def _const_spec(shape):
    return pl.BlockSpec(shape, lambda *_: (0,) * len(shape), pipeline_mode=pl.Buffered(1))


def _memkv_kernel(mem_ref, g_ref, wk_ref, wv_ref, gk_ref, mk_ref, mv_ref):
    m = _rms(mem_ref[...], g_ref[...]).astype(BF16)
    zk = jnp.dot(m, wk_ref[...], preferred_element_type=F32)
    zv = jnp.dot(m, wv_ref[...], preferred_element_type=F32)
    tm = mem_ref.shape[0]
    for h in range(MEM_HEADS):
        sl = slice(h * MEM_HEAD_DIM, (h + 1) * MEM_HEAD_DIM)
        rows = pl.ds(h, tm, stride=MEM_HEADS)
        mk_ref[rows, :] = _rms(zk[:, sl], gk_ref[...])
        mv_ref[rows, :] = zv[:, sl]


def _memkv(mem2d, g_mem, w_mk, w_mv, g_mk):
    n, d = mem2d.shape
    w = w_mk.shape[1]
    tm = min(ROW_TILE, n)
    return pl.pallas_call(
        _memkv_kernel,
        grid=(n // tm,),
        in_specs=[pl.BlockSpec((tm, d), lambda i: (i, 0)), _const_spec((1, d)),
                  _const_spec((d, w)), _const_spec((d, w)), _const_spec((1, MEM_HEAD_DIM))],
        out_specs=[pl.BlockSpec((tm * MEM_HEADS, MEM_HEAD_DIM), lambda i: (i, 0))] * 2,
        out_shape=[jax.ShapeDtypeStruct((n * MEM_HEADS, MEM_HEAD_DIM), F32)] * 2,
        compiler_params=_cparams(("parallel",)),
        name="memkv",
    )(mem2d, g_mem.reshape(1, d), w_mk.astype(BF16), w_mv.astype(BF16),
      g_mk.reshape(1, MEM_HEAD_DIM))


def _inproj_kernel(x_ref, g_ref, w_ref, gq_ref, gk_ref, seg_ref, *out_refs, qk_cols, emit_t):
    if emit_t:
        qt_ref, kb_ref, vt_ref, kt_ref, v4_ref, c_ref = out_refs
    else:
        qb_ref, k_ref, v_ref, c_ref = out_refs
    def mapnorm(z, g):
        ms = jnp.dot((z * z).astype(BF16), seg_ref[...], preferred_element_type=F32)
        return z * lax.rsqrt(ms + EPS) * g

    h = _rms(x_ref[0], g_ref[...]).astype(BF16)

    def proj(lo, hi):
        return jnp.dot(h, w_ref[:, lo:hi], preferred_element_type=F32)

    c0 = qk_cols
    q = mapnorm(proj(0, c0), gq_ref[...])
    k = mapnorm(proj(c0, 2 * c0), gk_ref[...])
    v = proj(2 * c0, 3 * c0)
    a = proj(3 * c0, 4 * c0)
    gate = proj(4 * c0, 5 * c0)
    c_ref[0] = a * jax.nn.sigmoid(gate)
    if emit_t:
        kb_ref[0] = k.astype(BF16)
        kt_ref[0] = k.T
        tm = x_ref.shape[1]
        for hh in range(N_HEADS):
            v4_ref[0, pl.ds(hh, tm, stride=N_HEADS), :] = v[:, hh * V_DIM:(hh + 1) * V_DIM]
        qt = q.T.astype(BF16)
        vt = v.T.astype(BF16)
        for j in range(qt_ref.shape[1]):
            qt_ref[0, j] = qt[:, j * ATTN_TQ:(j + 1) * ATTN_TQ]
        ones_row = (lax.broadcasted_iota(jnp.int32, (VT_ROWS - V_DIM, ATTN_TK), 0) == 0).astype(BF16)
        for j in range(vt_ref.shape[1]):
            for hh in range(N_HEADS):
                vt_ref[0, j, hh, :V_DIM, :] = vt[hh * V_DIM:(hh + 1) * V_DIM,
                                                 j * ATTN_TK:(j + 1) * ATTN_TK]
                vt_ref[0, j, hh, V_DIM:, :] = ones_row
    else:
        k_ref[0] = k
        v_ref[0] = v
        qb_ref[0] = q.astype(BF16)


def _inproj(x, g_mix, w_in_bf, gq_row, gk_row, seg, *, emit_t):
    b, t, d = x.shape
    cols = gq_row.shape[1]
    tm = min(ROW_TILE, t)
    nt = t // tm
    row = lambda width: pl.BlockSpec((1, tm, width), lambda bi, ti: (bi, ti, 0))
    f32_out = jax.ShapeDtypeStruct((b, t, cols), F32)
    if emit_t:
        nq, nk = tm // ATTN_TQ, tm // ATTN_TK
        out_shape = [jax.ShapeDtypeStruct((b, t // ATTN_TQ, cols, ATTN_TQ), BF16),
                     jax.ShapeDtypeStruct((b, t, cols), BF16),
                     jax.ShapeDtypeStruct((b, t // ATTN_TK, N_HEADS, VT_ROWS, ATTN_TK), BF16),
                     jax.ShapeDtypeStruct((b, cols, t), F32),
                     jax.ShapeDtypeStruct((b, t * N_HEADS, V_DIM), F32),
                     f32_out]
        out_specs = [pl.BlockSpec((1, nq, cols, ATTN_TQ), lambda bi, ti: (bi, ti, 0, 0)),
                     row(cols),
                     pl.BlockSpec((1, nk, N_HEADS, VT_ROWS, ATTN_TK),
                                  lambda bi, ti: (bi, ti, 0, 0, 0)),
                     pl.BlockSpec((1, cols, tm), lambda bi, ti: (bi, 0, ti)),
                     pl.BlockSpec((1, tm * N_HEADS, V_DIM), lambda bi, ti: (bi, ti, 0)),
                     row(cols)]
    else:
        out_shape = [jax.ShapeDtypeStruct((b, t, cols), BF16)] + [f32_out] * 3
        out_specs = [row(cols)] * 4
    return pl.pallas_call(
        functools.partial(_inproj_kernel, qk_cols=cols, emit_t=emit_t),
        grid=(b, nt),
        in_specs=[row(d), _const_spec((1, d)), _const_spec(w_in_bf.shape),
                  _const_spec((1, cols)), _const_spec((1, cols)), _const_spec((cols, cols))],
        out_specs=out_specs,
        out_shape=out_shape,
        compiler_params=_cparams(("parallel", "parallel")),
        name="inproj",
    )(x, g_mix.reshape(1, d), w_in_bf, gq_row, gk_row, seg)


PREV, DIAG, FAR = 0, 1, None


def _build_qbd(qt_ref, qbd_ref):
    tq = ATTN_TQ
    upper = lax.broadcasted_iota(jnp.int32, (HEAD_COLS, tq), 0) < HEAD_DIM
    for h in range(N_HEADS):
        qh = qt_ref[0, 0, h * HEAD_COLS:(h + 1) * HEAD_COLS, :]
        zero = jnp.zeros_like(qh)
        qbd_ref[h, :, :tq] = jnp.where(upper, qh, zero)
        qbd_ref[h, :, tq:] = jnp.where(upper, zero, qh)


def _attn_finalize(acc_ref, lam, gsub_ref, o_ref):
    tq = ATTN_TQ
    for h in range(N_HEADS):
        acc = acc_ref[h, :V_DIM, :]
        r = 1.0 / acc_ref[h, V_DIM:V_DIM + 1, :]
        ot = acc[:, :tq] * r[:, :tq] - lam * (acc[:, tq:] * r[:, tq:])
        ms = jnp.mean(ot * ot, axis=0, keepdims=True)
        ot = ot * lax.rsqrt(ms + EPS)
        o_ref[0, :, h * V_DIM:(h + 1) * V_DIM] = (ot.T * gsub_ref[...]).astype(BF16)


def _attn_prompt_kernel(sc_ref, qt_ref, kb_ref, vt_ref, bias_ref, gsub_ref, o_ref,
                        qbd_ref, acc_ref, sa_ref, sb_ref, *stat_refs, bounded):
    i = pl.program_id(1)
    tq, tk = ATTN_TQ, ATTN_TK
    lam = sc_ref[N_HEADS]
    if bounded:
        slot_a, slot_b = (sa_ref, None), (sb_ref, None)
    else:
        m_ref, mca_ref, mcb_ref = stat_refs
        slot_a, slot_b = (sa_ref, mca_ref), (sb_ref, mcb_ref)
        m_ref[...] = jnp.full(m_ref.shape, NEG_INF, F32)

    _build_qbd(qt_ref, qbd_ref)
    acc_ref[...] = jnp.zeros(acc_ref.shape, F32)

    def stage(j, slot, tile):
        s_ref, mc_ref = slot
        row0 = pl.multiple_of(j * tk, tk)
        for h in range(N_HEADS):
            kh = kb_ref[0, pl.ds(row0, tk), h * HEAD_COLS:(h + 1) * HEAD_COLS]
            s = jnp.dot(kh, qbd_ref[h], preferred_element_type=F32)
            if tile is not FAR:
                s = s + bias_ref[h, tile]
            if bounded:
                s_ref[h] = jnp.exp2(s).astype(BF16)
            else:
                s_ref[h] = s
                mc_ref[h] = jnp.max(s, axis=0, keepdims=True)

    def consume(j, slot, tile):
        s_ref, mc_ref = slot
        for h in range(N_HEADS):
            if bounded:
                acc_ref[h] += jnp.dot(vt_ref[0, j, h], s_ref[h], preferred_element_type=F32)
                continue
            m_prev = m_ref[h]
            if tile is FAR:
                shift = sc_ref[h]
                m_new = jnp.maximum(m_prev, mc_ref[h] + shift)
                p = jnp.exp2(s_ref[h] - (m_new - shift))
            else:
                m_new = jnp.maximum(m_prev, mc_ref[h])
                p = jnp.exp2(s_ref[h] - m_new)
            alpha = jnp.exp2(m_prev - m_new)
            pv = jnp.dot(vt_ref[0, j, h], p.astype(BF16), preferred_element_type=F32)
            acc_ref[h] = alpha * acc_ref[h] + pv
            m_ref[h] = m_new

    n_far = jnp.maximum(i - 1, 0)
    stage(i, slot_a, DIAG)

    @pl.when(i == 0)
    def _():
        consume(i, slot_a, DIAG)

    @pl.when(i >= 1)
    def _():
        consume(i, slot_a, DIAG)
        stage(i - 1, slot_b, PREV)

    @pl.when(i == 1)
    def _():
        consume(i - 1, slot_b, PREV)

    @pl.when(i >= 2)
    def _():
        consume(i - 1, slot_b, PREV)
        if bounded:
            for h in range(N_HEADS):
                acc_ref[h] = acc_ref[h] * sc_ref[N_HEADS + 1 + h]
        stage(0, slot_a, FAR)

    n_pairs = jnp.maximum(n_far - 1, 0) // 2

    def pair_body(jj, carry):
        f = 2 * jj
        consume(f, slot_a, FAR)
        stage(f + 1, slot_b, FAR)
        consume(f + 1, slot_b, FAR)
        stage(f + 2, slot_a, FAR)
        return carry

    lax.fori_loop(0, n_pairs, pair_body, 0)
    f_last = 2 * n_pairs
    left = n_far - f_last

    @pl.when(left == 1)
    def _():
        consume(f_last, slot_a, FAR)

    @pl.when(left == 2)
    def _():
        consume(f_last, slot_a, FAR)
        stage(f_last + 1, slot_b, FAR)

    @pl.when(left == 2)
    def _():
        consume(f_last + 1, slot_b, FAR)

    _attn_finalize(acc_ref, lam, gsub_ref, o_ref)


def _attn_prompt(scalars, score_bound, qt, kb, vt, bias_t, gsub_row):
    b, nq, cols, tq = qt.shape
    t = kb.shape[1]
    nk = vt.shape[1]
    qbd = pltpu.VMEM((N_HEADS, HEAD_COLS, 2 * tq), BF16)
    acc = pltpu.VMEM((N_HEADS, VT_ROWS, 2 * tq), F32)
    stat = pltpu.VMEM((N_HEADS, 1, 2 * tq), F32)
    scores = pltpu.VMEM((N_HEADS, ATTN_TK, 2 * tq), F32)
    probs = pltpu.VMEM((N_HEADS, ATTN_TK, 2 * tq), BF16)

    def call(bounded, scratch, name):
        return pl.pallas_call(
            functools.partial(_attn_prompt_kernel, bounded=bounded),
            grid=(b, nq),
            in_specs=[pl.BlockSpec(memory_space=pltpu.SMEM),
                      pl.BlockSpec((1, 1, cols, tq), lambda bi, i: (bi, i, 0, 0)),
                      pl.BlockSpec((1, t, cols), lambda bi, i: (bi, 0, 0)),
                      pl.BlockSpec((1, nk, N_HEADS, VT_ROWS, ATTN_TK),
                                   lambda bi, i: (bi, 0, 0, 0, 0)),
                      _const_spec(bias_t.shape), _const_spec((1, V_DIM))],
            out_specs=pl.BlockSpec((1, tq, cols), lambda bi, i: (bi, i, 0)),
            out_shape=jax.ShapeDtypeStruct((b, t, cols), BF16),
            scratch_shapes=scratch,
            compiler_params=_cparams(("parallel", "arbitrary")),
            name=name,
        )(scalars, qt, kb, vt, bias_t, gsub_row)

    return lax.cond(
        score_bound <= MAX_UNSTABILISED_SCORE,
        lambda: call(True, [qbd, acc, probs, probs], "attn_prompt_bounded"),
        lambda: call(False, [qbd, acc, scores, scores, stat, stat, stat], "attn_prompt"))


def _attn_sample_kernel(sc_ref, q_ref, kt_ref, vc_ref, ktn_ref, vn_ref, bc_ref, bn_ref, gsub_ref,
                        o_ref, qbd_ref, m_ref, l_ref, acc_ref):
    j = pl.program_id(1)
    tq, cols = q_ref.shape[1], q_ref.shape[2]
    lam = sc_ref[N_HEADS]

    @pl.when(j == 0)
    def _():
        q = q_ref[0].astype(F32)
        col = lax.broadcasted_iota(jnp.int32, (tq, cols), 1)
        for hm in range(cols // HEAD_DIM):
            mine = (col >= hm * HEAD_DIM) & (col < (hm + 1) * HEAD_DIM)
            qbd_ref[hm * tq:(hm + 1) * tq, :] = jnp.where(mine, q, 0.0)
        m_ref[...] = jnp.full(m_ref.shape, NEG_INF, F32)
        l_ref[...] = jnp.zeros(l_ref.shape, F32)
        acc_ref[...] = jnp.zeros(acc_ref.shape, F32)

    def update(kt, v_of_head, bias):
        s = jnp.dot(qbd_ref[...].astype(BF16), kt, preferred_element_type=F32) + bias
        m_prev = m_ref[...]
        m_new = jnp.maximum(m_prev, jnp.max(s, axis=-1, keepdims=True))
        alpha = jnp.exp2(m_prev - m_new)
        p = jnp.exp2(s - m_new)
        l_ref[...] = alpha * l_ref[...] + jnp.sum(p, axis=-1, keepdims=True)
        pb = p.astype(BF16)
        for h in range(N_HEADS):
            rows = slice(h * 2 * tq, (h + 1) * 2 * tq)
            acc_ref[rows, :] = alpha[rows] * acc_ref[rows, :] + jnp.dot(
                pb[rows], v_of_head(h), preferred_element_type=F32)
        m_ref[...] = m_new

    tk = kt_ref.shape[2]
    update(kt_ref[0].astype(BF16),
           lambda h: vc_ref[0, pl.ds(h, tk, stride=N_HEADS), :].astype(BF16), bc_ref[j])

    @pl.when(j == pl.num_programs(1) - 1)
    def _():
        update(ktn_ref[0], lambda h: vn_ref[0, :, h * V_DIM:(h + 1) * V_DIM], bn_ref[...])
        for h in range(N_HEADS):
            r0 = h * 2 * tq
            inv0 = 1.0 / l_ref[r0:r0 + tq, :]
            inv1 = 1.0 / l_ref[r0 + tq:r0 + 2 * tq, :]
            o = acc_ref[r0:r0 + tq, :] * inv0 - lam * (acc_ref[r0 + tq:r0 + 2 * tq, :] * inv1)
            o_ref[0, :, h * V_DIM:(h + 1) * V_DIM] = _rms(o, gsub_ref[...]).astype(BF16)


def _attn_sample(scalars, q, kt, v4, ktn, vn, bias_c, bias_n, gsub_row):
    b, tq, cols = q.shape
    p = kt.shape[2]
    nkc = p // CACHE_TK
    rows = 2 * N_HEADS * tq
    return pl.pallas_call(
        _attn_sample_kernel,
        grid=(b, nkc),
        in_specs=[pl.BlockSpec(memory_space=pltpu.SMEM),
                  pl.BlockSpec((1, tq, cols), lambda bi, j: (bi, 0, 0)),
                  pl.BlockSpec((1, cols, CACHE_TK), lambda bi, j: (bi, 0, j)),
                  pl.BlockSpec((1, CACHE_TK * N_HEADS, V_DIM), lambda bi, j: (bi, j, 0)),
                  pl.BlockSpec((1, cols, NEW_KEY_PAD), lambda bi, j: (bi, 0, 0)),
                  pl.BlockSpec((1, NEW_KEY_PAD, cols), lambda bi, j: (bi, 0, 0)),
                  _const_spec(bias_c.shape), _const_spec(bias_n.shape), _const_spec((1, V_DIM))],
        out_specs=pl.BlockSpec((1, tq, cols), lambda bi, j: (bi, 0, 0)),
        out_shape=jax.ShapeDtypeStruct((b, tq, cols), BF16),
        scratch_shapes=[pltpu.VMEM((rows, cols), F32),
                        pltpu.VMEM((rows, 1), F32),
                        pltpu.VMEM((rows, 1), F32),
                        pltpu.VMEM((rows, V_DIM), F32)],
        compiler_params=_cparams(("parallel", "arbitrary")),
        name="attn_sample",
    )(scalars, q, kt, v4, ktn, vn, bias_c, bias_n, gsub_row)


def _mix_kernel(x_ref, o_ref, c_ref, hist_ref, wconv_ref, bconv_ref, lng_ref, lnb_ref,
                woo_ref, woc_ref, gcross_ref, wmq_ref, gmq_ref, mk_ref, mv_ref, wmo_ref,
                out_ref, ext_ref, xs_ref, cv_ref, ca_ref):
    bb, tt, d = x_ref.shape
    rows = bb * tt
    n_mem = mk_ref.shape[1] // MEM_HEADS
    rc = min(CONV_ROWS, tt)
    lead = CONV_HALO - (CONV_K - 1)

    for b in range(bb):
        ext_ref[b, :CONV_HALO, :] = hist_ref[b, 0]
        ext_ref[b, CONV_HALO:, :] = c_ref[b]
    span = xs_ref.shape[2]
    for b in range(bb):
        for r in range(1, 8):
            xs_ref[r - 1, b] = ext_ref[b, r:r + span, :]
    for b in range(bb):
        for r0 in range(0, tt, rc):
            acc = jnp.zeros((rc, c_ref.shape[2]), F32) + bconv_ref[...]
            for k in range(CONV_K):
                a, r = divmod(k + lead, 8)
                lo = r0 + 8 * a
                src = ext_ref[b, lo:lo + rc, :] if r == 0 else xs_ref[r - 1, b, lo:lo + rc, :]
                acc = acc + wconv_ref[k:k + 1, :] * src
            mu = jnp.mean(acc, axis=-1, keepdims=True)
            xc = acc - mu
            var = jnp.mean(xc * xc, axis=-1, keepdims=True)
            y = xc * lax.rsqrt(var + EPS) * lng_ref[...] + lnb_ref[...]
            cv_ref[b * tt + r0:b * tt + r0 + rc, :] = (y * jax.nn.sigmoid(y)).astype(BF16)

    x = x_ref[...].reshape(rows, d)
    o = o_ref[...].reshape(rows, o_ref.shape[2])
    x1 = (x + jnp.dot(o, woo_ref[...], preferred_element_type=F32)
          + jnp.dot(cv_ref[...], woc_ref[...], preferred_element_type=F32))

    hc = _rms(x1, gcross_ref[...]).astype(BF16)
    qm = jnp.dot(hc, wmq_ref[...], preferred_element_type=F32)
    for h in range(MEM_HEADS):
        cols = slice(h * MEM_HEAD_DIM, (h + 1) * MEM_HEAD_DIM)
        qn = _rms(qm[:, cols], gmq_ref[...]).astype(BF16)
        for b in range(bb):
            mem_rows = pl.ds(h, n_mem, stride=MEM_HEADS)
            mk = mk_ref[b, mem_rows, :].astype(BF16)
            mv = mv_ref[b, mem_rows, :].astype(BF16)
            s = lax.dot_general(qn[b * tt:(b + 1) * tt], mk, (((1,), (1,)), ((), ())),
                                preferred_element_type=F32)
            p = jnp.exp2(s - jnp.max(s, axis=-1, keepdims=True))
            l = jnp.sum(p, axis=-1, keepdims=True)
            oh = jnp.dot(p.astype(BF16), mv, preferred_element_type=F32) / l
            ca_ref[b * tt:(b + 1) * tt, cols] = oh.astype(BF16)
    x2 = x1 + jnp.dot(ca_ref[...], wmo_ref[...], preferred_element_type=F32)
    out_ref[...] = x2.reshape(bb, tt, d)


def _mix(x, o, c, hist, wconv, bconv, lng, lnb, woo, woc, gcross, wmq, gmq_row, mk, mv, wmo,
         *, bb, tt):
    b, t, d = x.shape
    cw = c.shape[2]
    mem_rows, mhd = mk.shape[1], mk.shape[2]
    mw = wmq.shape[1]
    rows = bb * tt
    tile = lambda width: pl.BlockSpec((bb, tt, width), lambda bi, ti: (bi, ti, 0))
    return pl.pallas_call(
        _mix_kernel,
        grid=(b // bb, t // tt),
        in_specs=[tile(d), tile(cw), tile(cw),
                  pl.BlockSpec((bb, 1, CONV_HALO, cw), lambda bi, ti: (bi, ti, 0, 0)),
                  _const_spec(wconv.shape), _const_spec((1, cw)), _const_spec((1, cw)),
                  _const_spec((1, cw)), _const_spec(woo.shape), _const_spec(woc.shape),
                  _const_spec((1, d)), _const_spec(wmq.shape), _const_spec((1, MEM_HEAD_DIM)),
                  pl.BlockSpec((bb, mem_rows, mhd), lambda bi, ti: (bi, 0, 0)),
                  pl.BlockSpec((bb, mem_rows, mhd), lambda bi, ti: (bi, 0, 0)),
                  _const_spec(wmo.shape)],
        out_specs=tile(d),
        out_shape=jax.ShapeDtypeStruct((b, t, d), F32),
        scratch_shapes=[pltpu.VMEM((bb, CONV_HALO + tt, cw), F32),
                        pltpu.VMEM((7, bb, CONV_HALO - 8 + tt, cw), F32),
                        pltpu.VMEM((rows, cw), BF16),
                        pltpu.VMEM((rows, mw), BF16)],
        compiler_params=_cparams(("parallel", "parallel")),
        name="mix",
    )(x, o, c, hist, wconv, bconv, lng, lnb, woo, woc, gcross, wmq, gmq_row, mk, mv, wmo)


def _mlp_kernel(x_ref, g_ref, w1_ref, w2_ref, out_ref):
    x = x_ref[...]
    hf = _rms(x, g_ref[...]).astype(BF16)
    u = jnp.maximum(jnp.dot(hf, w1_ref[...], preferred_element_type=F32), 0.0)
    out_ref[...] = x + jnp.dot((u * u).astype(BF16), w2_ref[...], preferred_element_type=F32)


def _mlp(x2d, g_ffn, w1, w2):
    n, d = x2d.shape
    tm = min(ROW_TILE, n)
    return pl.pallas_call(
        _mlp_kernel,
        grid=(n // tm,),
        in_specs=[pl.BlockSpec((tm, d), lambda i: (i, 0)), _const_spec((1, d)),
                  _const_spec(w1.shape), _const_spec(w2.shape)],
        out_specs=pl.BlockSpec((tm, d), lambda i: (i, 0)),
        out_shape=jax.ShapeDtypeStruct((n, d), F32),
        compiler_params=_cparams(("parallel",)),
        name="mlp",
    )(x2d, g_ffn.reshape(1, d), w1, w2)


def _rel_bucket(rel):
    half = N_BUCKETS // 2
    max_exact = half // 2
    ret = jnp.where(rel > 0, half, 0)
    n = jnp.abs(rel)
    nf = jnp.maximum(n, 1).astype(jnp.float32)
    large = max_exact + (jnp.log(nf / max_exact) / math.log(MAX_DISTANCE / max_exact)
                         * (half - max_exact)).astype(jnp.int32)
    large = jnp.minimum(large, half - 1)
    return ret + jnp.where(n < max_exact, n, large)


def _masked_bias(rel_table, q_pos, k_pos):
    bucket = _rel_bucket(k_pos[None, :] - q_pos[:, None])[None]
    table = rel_table.astype(F32) * LOG2E
    bias = jnp.zeros((rel_table.shape[1],) + bucket.shape[1:], F32)
    for bkt in range(N_BUCKETS):
        bias = jnp.where(bucket == bkt, table[bkt][:, None, None], bias)
    mask = (k_pos[None, :] // CHUNK) <= (q_pos[:, None] // CHUNK)
    return jnp.where(mask[None], bias, NEG_INF)


def _far_bucket_is_saturated(min_distance):
    half = N_BUCKETS // 2
    max_exact = half // 2
    large = max_exact + int(np.log(min_distance / max_exact) / math.log(MAX_DISTANCE / max_exact)
                            * (half - max_exact) * (1 - 1e-6))
    return large >= half - 1


def _layer(x, k_past, v_past, c_past, mk, mv, w, lam, lam_init):
    b, t, d = x.shape
    prompt = k_past is None
    cols = N_HEADS * HEAD_COLS
    gsub_row = (w["g_sub"] * (1.0 - lam_init)).reshape(1, V_DIM)

    if prompt:
        assert t % ROW_TILE == 0 and ROW_TILE % ATTN_TQ == 0 and ATTN_TQ == ATTN_TK
        assert ATTN_TK % CHUNK == 0 and _far_bucket_is_saturated(ATTN_TK + 1)
        qt, kb, vt, kt, v4, c = _inproj(x, w["g_mix"], w["w_in"], w["gq_row"], w["gk_row"],
                                        w["seg"], emit_t=True)
        k_out = jnp.transpose(kt.reshape(b, N_HEADS, 2, HEAD_DIM, t), (0, 4, 1, 2, 3))
        v_out = v4.reshape(b, t, N_HEADS, V_DIM)
        q_pos = ATTN_TQ + jnp.arange(ATTN_TQ, dtype=jnp.int32)
        near = _masked_bias(w["rel_table"], q_pos, jnp.arange(2 * ATTN_TK, dtype=jnp.int32))
        near = jnp.stack([near[:, :, :ATTN_TK], near[:, :, ATTN_TK:]], axis=1)
        near_t = jnp.swapaxes(near, 2, 3)
        bias_t = jnp.concatenate([near_t, near_t], axis=3)
        far = w["rel_table"][_rel_bucket(jnp.int32(-(ATTN_TK + 1)))].astype(F32) * LOG2E
        scalars = jnp.concatenate([far, lam.reshape(1), jnp.exp2(-far)]).astype(F32)
        score_bound = (HEAD_DIM * jnp.max(jnp.abs(w["gq_row"])) * jnp.max(jnp.abs(w["gk_row"]))
                       + LOG2E * jnp.max(jnp.abs(w["rel_table"])))
        o = _attn_prompt(scalars, score_bound, qt, kb, vt, bias_t, gsub_row)
        bb, tt = 1, ROW_TILE
    else:
        p = k_past.shape[2]
        assert p % CACHE_TK == 0 and t <= NEW_KEY_PAD and t % 16 == 0
        qb, k, v, c = _inproj(x.reshape(1, b * t, d), w["g_mix"], w["w_in"], w["gq_row"],
                              w["gk_row"], w["seg"], emit_t=False)
        qb, k, v, c = (a.reshape(b, t, cols) for a in (qb, k, v, c))
        k_out = k.reshape(b, t, N_HEADS, 2, HEAD_DIM)
        v_out = v.reshape(b, t, N_HEADS, V_DIM)
        ktn = jnp.pad(jnp.swapaxes(k, 1, 2).astype(BF16), ((0, 0), (0, 0), (0, NEW_KEY_PAD - t)))
        vn = jnp.pad(v.astype(BF16), ((0, 0), (0, NEW_KEY_PAD - t), (0, 0)))
        q_pos = p + jnp.arange(t, dtype=jnp.int32)
        bias = _masked_bias(w["rel_table"], q_pos, jnp.arange(p + NEW_KEY_PAD, dtype=jnp.int32))
        bias = jnp.where(jnp.arange(p + NEW_KEY_PAD) < p + t, bias, NEG_INF)
        rows = 2 * N_HEADS * t
        bias = jnp.broadcast_to(bias[:, None], (N_HEADS, 2, t, p + NEW_KEY_PAD)).reshape(rows, -1)
        bias_c = jnp.swapaxes(bias[:, :p].reshape(rows, p // CACHE_TK, CACHE_TK), 0, 1)
        scalars = jnp.concatenate([jnp.zeros((N_HEADS,), F32), lam.reshape(1)]).astype(F32)
        o = _attn_sample(scalars, qb, k_past, v_past, ktn, vn, bias_c, bias[:, p:], gsub_row)
        bb, tt = SAMPLE_BATCH_TILE, t
        assert b % bb == 0

    nt = t // tt
    first = jnp.pad(c_past, ((0, 0), (CONV_HALO - (CONV_K - 1), 0), (0, 0)))[:, None]
    if nt > 1:
        tails = c.reshape(b, nt, tt, c.shape[2])[:, :-1, tt - CONV_HALO:, :]
        hist = jnp.concatenate([first, tails], axis=1)
    else:
        hist = first
    x2 = _mix(x, o, c, hist, w["w_conv"], w["b_conv"], w["ln_g"], w["ln_b"], w["w_out_o"],
              w["w_out_c"], w["g_cross"], w["w_mq"], w["gmq_row"], mk, mv, w["w_mo"], bb=bb, tt=tt)
    y = _mlp(x2.reshape(b * t, d), w["g_ffn"], w["w_ff1"], w["w_ff2"]).reshape(b, t, d)
    if t >= CONV_K - 1:
        c_hist_tail = c[:, t - (CONV_K - 1):]
    else:
        c_hist_tail = jnp.concatenate([c_past[:, t:], c], axis=1)
    return y, k_out, v_out, c_hist_tail


def kernel(x_prompt, x_sample, cache_k, cache_v, cache_conv, cache_mem_k, cache_mem_v, mem_prompt,
           rel_table, g_mix, w_in, g_q, g_k, lam_vec, g_sub, w_conv, b_conv, ln_g, ln_b, w_out,
           g_cross, g_mem, w_mq, w_mk, w_mv, g_mq, g_mk, w_mo, g_ffn, w_ff1, w_ff2):
    depth = g_mix.shape[0]
    assert depth == 1
    b, t, d = x_prompt.shape
    bs, ts, _ = x_sample.shape
    cols = N_HEADS * HEAD_COLS
    cw = w_conv.shape[2]
    attn_w = N_HEADS * V_DIM
    l = 0
    lam_init = 0.8 - 0.6 * math.exp(-0.3 * l)
    lp = lam_vec[l].astype(F32)
    lam = jnp.exp(jnp.sum(lp[0] * lp[1])) - jnp.exp(jnp.sum(lp[2] * lp[3])) + lam_init

    seg = jnp.kron(jnp.eye(cols // HEAD_DIM, dtype=F32),
                   jnp.full((HEAD_DIM, HEAD_DIM), 1.0 / HEAD_DIM, F32)).astype(BF16)
    n_maps = cols // HEAD_DIM
    w = dict(
        rel_table=rel_table, g_mix=g_mix[l], w_in=w_in[l].astype(BF16), seg=seg,
        gq_row=jnp.tile(g_q[l] * (HEAD_DIM ** -0.5 * LOG2E), n_maps).reshape(1, cols),
        gk_row=jnp.tile(g_k[l], n_maps).reshape(1, cols),
        g_sub=g_sub[l],
        w_conv=jnp.pad(w_conv[l], ((0, CONV_HALO - CONV_K), (0, 0))),
        b_conv=b_conv[l].reshape(1, cw), ln_g=ln_g[l].reshape(1, cw), ln_b=ln_b[l].reshape(1, cw),
        w_out_o=w_out[l][:attn_w].astype(BF16), w_out_c=w_out[l][attn_w:].astype(BF16),
        g_cross=g_cross[l].reshape(1, d), w_mq=w_mq[l].astype(BF16),
        gmq_row=(g_mq[l] * (MEM_HEAD_DIM ** -0.5 * LOG2E)).reshape(1, MEM_HEAD_DIM),
        w_mo=w_mo[l].astype(BF16), g_ffn=g_ffn[l],
        w_ff1=w_ff1[l].astype(BF16), w_ff2=w_ff2[l].astype(BF16),
    )

    n_mem = mem_prompt.shape[1]
    mk_p, mv_p = _memkv(mem_prompt.reshape(b * n_mem, d), g_mem[l], w_mk[l], w_mv[l], g_mk[l])
    mk_p = mk_p.reshape(b, n_mem * MEM_HEADS, MEM_HEAD_DIM)
    mv_p = mv_p.reshape(b, n_mem * MEM_HEADS, MEM_HEAD_DIM)

    zero_conv = jnp.zeros((b, CONV_K - 1, cw), F32)
    yp, kp, vp, cp = _layer(x_prompt, None, None, zero_conv, mk_p, mv_p, w, lam, lam_init)

    p = cache_k.shape[2]
    k_past = jnp.transpose(cache_k[l], (0, 2, 3, 4, 1)).reshape(bs, cols, p)
    v_past = cache_v[l].reshape(bs, p * N_HEADS, V_DIM)
    ys, kn, vn, cn = _layer(x_sample, k_past, v_past, cache_conv[l],
                            cache_mem_k[l].reshape(bs, n_mem * MEM_HEADS, MEM_HEAD_DIM),
                            cache_mem_v[l].reshape(bs, n_mem * MEM_HEADS, MEM_HEAD_DIM),
                            w, lam, lam_init)

    return (yp, ys, kp[None], vp[None], cp[None],
            mk_p.reshape(1, b, n_mem, MEM_HEADS, MEM_HEAD_DIM),
            mv_p.reshape(1, b, n_mem, MEM_HEADS, MEM_HEAD_DIM),
            kn[None], vn[None], cn[None])
```

```python
import functools
import math

import jax
import jax.numpy as jnp
import numpy as np
from jax import lax
from jax.experimental import pallas as pl
from jax.experimental.pallas import tpu as pltpu

F32 = jnp.float32
BF16 = jnp.bfloat16

CHUNK = 64
N_HEADS = 4
HEAD_DIM = 64
V_DIM = 2 * HEAD_DIM
HEAD_COLS = 2 * HEAD_DIM
CONV_K = 31
CONV_HALO = 32
N_BUCKETS = 32
MAX_DISTANCE = 128
MEM_HEADS = 4
MEM_HEAD_DIM = 128
EPS = 1e-6
NEG_INF = -1e30
LOG2E = math.log2(math.e)

ATTN_TQ = 256
ATTN_TK = 256
VT_ROWS = V_DIM + 16
MAX_UNSTABILISED_SCORE = 100.0
CACHE_TK = 2048
NEW_KEY_PAD = 128
ROW_TILE = 512
CONV_ROWS = 64
SAMPLE_BATCH_TILE = 8
VMEM_LIMIT = 56 * 1024 * 1024


def _cparams(sem):
    return pltpu.CompilerParams(dimension_semantics=sem, vmem_limit_bytes=VMEM_LIMIT)


def _rms(x, g):
    ms = jnp.mean(x * x, axis=-1, keepdims=True)
    return x * lax.rsqrt(ms + EPS) * g


def _const_spec(shape):
    return pl.BlockSpec(shape, lambda *_: (0,) * len(shape), pipeline_mode=pl.Buffered(1))


def _memkv_kernel(mem_ref, g_ref, wk_ref, wv_ref, gk_ref, mk_ref, mv_ref):
    m = _rms(mem_ref[...], g_ref[...]).astype(BF16)
    zk = jnp.dot(m, wk_ref[...], preferred_element_type=F32)
    zv = jnp.dot(m, wv_ref[...], preferred_element_type=F32)
    tm = mem_ref.shape[0]
    for h in range(MEM_HEADS):
        sl = slice(h * MEM_HEAD_DIM, (h + 1) * MEM_HEAD_DIM)
        rows = pl.ds(h, tm, stride=MEM_HEADS)
        mk_ref[rows, :] = _rms(zk[:, sl], gk_ref[...])
        mv_ref[rows, :] = zv[:, sl]


def _memkv(mem2d, g_mem, w_mk, w_mv, g_mk):
    n, d = mem2d.shape
    w = w_mk.shape[1]
    tm = min(ROW_TILE, n)
    return pl.pallas_call(
        _memkv_kernel,
        grid=(n // tm,),
        in_specs=[pl.BlockSpec((tm, d), lambda i: (i, 0)), _const_spec((1, d)),
                  _const_spec((d, w)), _const_spec((d, w)), _const_spec((1, MEM_HEAD_DIM))],
        out_specs=[pl.BlockSpec((tm * MEM_HEADS, MEM_HEAD_DIM), lambda i: (i, 0))] * 2,
        out_shape=[jax.ShapeDtypeStruct((n * MEM_HEADS, MEM_HEAD_DIM), F32)] * 2,
        compiler_params=_cparams(("parallel",)),
        name="memkv",
    )(mem2d, g_mem.reshape(1, d), w_mk.astype(BF16), w_mv.astype(BF16),
      g_mk.reshape(1, MEM_HEAD_DIM))


def _inproj_kernel(x_ref, g_ref, w_ref, gq_ref, gk_ref, *refs, qk_cols, emit_t):
    if emit_t:
        qt_ref, kb_ref, vt_ref, kt_ref, v4_ref, c_ref = refs
    else:
        seg_ref, qb_ref, k_ref, v_ref, c_ref = refs

    def mapnorm(z, g):
        ms = jnp.dot((z * z).astype(BF16), seg_ref[...], preferred_element_type=F32)
        return z * lax.rsqrt(ms + EPS) * g

    def mapnorm_t(z, g_t):
        zt = z.T
        z3 = zt.reshape(zt.shape[0] // HEAD_DIM, HEAD_DIM, zt.shape[1])
        ms = jnp.mean(z3 * z3, axis=1, keepdims=True)
        return (z3 * lax.rsqrt(ms + EPS)).reshape(zt.shape) * g_t

    h = _rms(x_ref[0], g_ref[...]).astype(BF16)

    def proj(lo, hi):
        return jnp.dot(h, w_ref[:, lo:hi], preferred_element_type=F32)

    c0 = qk_cols
    zq = proj(0, c0)
    zk = proj(c0, 2 * c0)
    v = proj(2 * c0, 3 * c0)
    a = proj(3 * c0, 4 * c0)
    gate = proj(4 * c0, 5 * c0)
    c_ref[0] = a * jax.nn.sigmoid(gate)
    if emit_t:
        kt = mapnorm_t(zk, gk_ref[...])
        kt_ref[0] = kt
        kb_ref[0] = kt.T.astype(BF16)
        tm = x_ref.shape[1]
        for hh in range(N_HEADS):
            v4_ref[0, pl.ds(hh, tm, stride=N_HEADS), :] = v[:, hh * V_DIM:(hh + 1) * V_DIM]
        qt = mapnorm_t(zq, gq_ref[...]).astype(BF16)
        vt = v.T.astype(BF16)
        for j in range(qt_ref.shape[1]):
            qt_ref[0, j] = qt[:, j * ATTN_TQ:(j + 1) * ATTN_TQ]
        ones_row = (lax.broadcasted_iota(jnp.int32, (VT_ROWS - V_DIM, ATTN_TK), 0) == 0).astype(BF16)
        for j in range(vt_ref.shape[1]):
            for hh in range(N_HEADS):
                vt_ref[0, j, hh, :V_DIM, :] = vt[hh * V_DIM:(hh + 1) * V_DIM,
                                                 j * ATTN_TK:(j + 1) * ATTN_TK]
                vt_ref[0, j, hh, V_DIM:, :] = ones_row
    else:
        k_ref[0] = mapnorm(zk, gk_ref[...])
        v_ref[0] = v
        qb_ref[0] = mapnorm(zq, gq_ref[...]).astype(BF16)


def _inproj(x, g_mix, w_in_bf, gq_row, gk_row, seg, *, emit_t):
    b, t, d = x.shape
    cols = gq_row.shape[1]
    tm = min(ROW_TILE, t)
    nt = t // tm
    row = lambda width: pl.BlockSpec((1, tm, width), lambda bi, ti: (bi, ti, 0))
    f32_out = jax.ShapeDtypeStruct((b, t, cols), F32)
    if emit_t:
        nq, nk = tm // ATTN_TQ, tm // ATTN_TK
        out_shape = [jax.ShapeDtypeStruct((b, t // ATTN_TQ, cols, ATTN_TQ), BF16),
                     jax.ShapeDtypeStruct((b, t, cols), BF16),
                     jax.ShapeDtypeStruct((b, t // ATTN_TK, N_HEADS, VT_ROWS, ATTN_TK), BF16),
                     jax.ShapeDtypeStruct((b, cols, t), F32),
                     jax.ShapeDtypeStruct((b, t * N_HEADS, V_DIM), F32),
                     f32_out]
        out_specs = [pl.BlockSpec((1, nq, cols, ATTN_TQ), lambda bi, ti: (bi, ti, 0, 0)),
                     row(cols),
                     pl.BlockSpec((1, nk, N_HEADS, VT_ROWS, ATTN_TK),
                                  lambda bi, ti: (bi, ti, 0, 0, 0)),
                     pl.BlockSpec((1, cols, tm), lambda bi, ti: (bi, 0, ti)),
                     pl.BlockSpec((1, tm * N_HEADS, V_DIM), lambda bi, ti: (bi, ti, 0)),
                     row(cols)]
        gains = [jnp.broadcast_to(g.reshape(cols, 1), (cols, tm)) for g in (gq_row, gk_row)]
        gain_specs = [_const_spec((cols, tm))] * 2
    else:
        out_shape = [jax.ShapeDtypeStruct((b, t, cols), BF16)] + [f32_out] * 3
        out_specs = [row(cols)] * 4
        gains = [gq_row, gk_row, seg]
        gain_specs = [_const_spec((1, cols)), _const_spec((1, cols)), _const_spec((cols, cols))]
    return pl.pallas_call(
        functools.partial(_inproj_kernel, qk_cols=cols, emit_t=emit_t),
        grid=(b, nt),
        in_specs=[row(d), _const_spec((1, d)), _const_spec(w_in_bf.shape)] + gain_specs,
        out_specs=out_specs,
        out_shape=out_shape,
        compiler_params=_cparams(("parallel", "parallel")),
        name="inproj",
    )(x, g_mix.reshape(1, d), w_in_bf, *gains)


PREV, DIAG, FAR = 0, 1, None


def _build_qbd(qt_ref, qbd_ref):
    tq = ATTN_TQ
    upper = lax.broadcasted_iota(jnp.int32, (HEAD_COLS, tq), 0) < HEAD_DIM
    for h in range(N_HEADS):
        qh = qt_ref[0, 0, h * HEAD_COLS:(h + 1) * HEAD_COLS, :]
        zero = jnp.zeros_like(qh)
        qbd_ref[h, :, :tq] = jnp.where(upper, qh, zero)
        qbd_ref[h, :, tq:] = jnp.where(upper, zero, qh)


def _attn_finalize(acc_ref, lam, gsub_ref, o_ref):
    tq = ATTN_TQ
    for h in range(N_HEADS):
        acc = acc_ref[h, :V_DIM, :]
        r = 1.0 / acc_ref[h, V_DIM:V_DIM + 1, :]
        ot = acc[:, :tq] * r[:, :tq] - lam * (acc[:, tq:] * r[:, tq:])
        ms = jnp.mean(ot * ot, axis=0, keepdims=True)
        ot = ot * lax.rsqrt(ms + EPS)
        o_ref[0, :, h * V_DIM:(h + 1) * V_DIM] = (ot.T * gsub_ref[...]).astype(BF16)


def _attn_prompt_kernel(sc_ref, qt_ref, kb_ref, vt_ref, bias_ref, gsub_ref, o_ref,
                        qbd_ref, acc_ref, sa_ref, sb_ref, *stat_refs, bounded):
    i = pl.program_id(1)
    tq, tk = ATTN_TQ, ATTN_TK
    lam = sc_ref[N_HEADS]
    if bounded:
        slot_a, slot_b = (sa_ref, None), (sb_ref, None)
    else:
        m_ref, mca_ref, mcb_ref = stat_refs
        slot_a, slot_b = (sa_ref, mca_ref), (sb_ref, mcb_ref)
        m_ref[...] = jnp.full(m_ref.shape, NEG_INF, F32)

    _build_qbd(qt_ref, qbd_ref)
    acc_ref[...] = jnp.zeros(acc_ref.shape, F32)

    def stage(j, slot, tile):
        s_ref, mc_ref = slot
        row0 = pl.multiple_of(j * tk, tk)
        for h in range(N_HEADS):
            kh = kb_ref[0, pl.ds(row0, tk), h * HEAD_COLS:(h + 1) * HEAD_COLS]
            s = jnp.dot(kh, qbd_ref[h], preferred_element_type=F32)
            if tile is not FAR:
                s = s + bias_ref[h, tile]
            if bounded:
                s_ref[h] = jnp.exp2(s).astype(BF16)
            else:
                s_ref[h] = s
                mc_ref[h] = jnp.max(s, axis=0, keepdims=True)

    def consume(j, slot, tile):
        s_ref, mc_ref = slot
        for h in range(N_HEADS):
            if bounded:
                acc_ref[h] += jnp.dot(vt_ref[0, j, h], s_ref[h], preferred_element_type=F32)
                continue
            m_prev = m_ref[h]
            if tile is FAR:
                shift = sc_ref[h]
                m_new = jnp.maximum(m_prev, mc_ref[h] + shift)
                p = jnp.exp2(s_ref[h] - (m_new - shift))
            else:
                m_new = jnp.maximum(m_prev, mc_ref[h])
                p = jnp.exp2(s_ref[h] - m_new)
            alpha = jnp.exp2(m_prev - m_new)
            pv = jnp.dot(vt_ref[0, j, h], p.astype(BF16), preferred_element_type=F32)
            acc_ref[h] = alpha * acc_ref[h] + pv
            m_ref[h] = m_new

    n_far = jnp.maximum(i - 1, 0)
    stage(i, slot_a, DIAG)

    @pl.when(i == 0)
    def _():
        consume(i, slot_a, DIAG)

    @pl.when(i >= 1)
    def _():
        consume(i, slot_a, DIAG)
        stage(i - 1, slot_b, PREV)

    @pl.when(i == 1)
    def _():
        consume(i - 1, slot_b, PREV)

    @pl.when(i >= 2)
    def _():
        consume(i - 1, slot_b, PREV)
        if bounded:
            for h in range(N_HEADS):
                acc_ref[h] = acc_ref[h] * sc_ref[N_HEADS + 1 + h]
        stage(0, slot_a, FAR)

    n_pairs = jnp.maximum(n_far - 1, 0) // 2

    def pair_body(jj, carry):
        f = 2 * jj
        consume(f, slot_a, FAR)
        stage(f + 1, slot_b, FAR)
        consume(f + 1, slot_b, FAR)
        stage(f + 2, slot_a, FAR)
        return carry

    lax.fori_loop(0, n_pairs, pair_body, 0)
    f_last = 2 * n_pairs
    left = n_far - f_last

    @pl.when(left == 1)
    def _():
        consume(f_last, slot_a, FAR)

    @pl.when(left == 2)
    def _():
        consume(f_last, slot_a, FAR)
        stage(f_last + 1, slot_b, FAR)

    @pl.when(left == 2)
    def _():
        consume(f_last + 1, slot_b, FAR)

    _attn_finalize(acc_ref, lam, gsub_ref, o_ref)


def _attn_prompt(scalars, score_bound, qt, kb, vt, bias_t, gsub_row):
    b, nq, cols, tq = qt.shape
    t = kb.shape[1]
    nk = vt.shape[1]
    qbd = pltpu.VMEM((N_HEADS, HEAD_COLS, 2 * tq), BF16)
    acc = pltpu.VMEM((N_HEADS, VT_ROWS, 2 * tq), F32)
    stat = pltpu.VMEM((N_HEADS, 1, 2 * tq), F32)
    scores = pltpu.VMEM((N_HEADS, ATTN_TK, 2 * tq), F32)
    probs = pltpu.VMEM((N_HEADS, ATTN_TK, 2 * tq), BF16)

    def call(bounded, scratch, name):
        return pl.pallas_call(
            functools.partial(_attn_prompt_kernel, bounded=bounded),
            grid=(b, nq),
            in_specs=[pl.BlockSpec(memory_space=pltpu.SMEM),
                      pl.BlockSpec((1, 1, cols, tq), lambda bi, i: (bi, i, 0, 0)),
                      pl.BlockSpec((1, t, cols), lambda bi, i: (bi, 0, 0)),
                      pl.BlockSpec((1, nk, N_HEADS, VT_ROWS, ATTN_TK),
                                   lambda bi, i: (bi, 0, 0, 0, 0)),
                      _const_spec(bias_t.shape), _const_spec((1, V_DIM))],
            out_specs=pl.BlockSpec((1, tq, cols), lambda bi, i: (bi, i, 0)),
            out_shape=jax.ShapeDtypeStruct((b, t, cols), BF16),
            scratch_shapes=scratch,
            compiler_params=_cparams(("parallel", "arbitrary")),
            name=name,
        )(scalars, qt, kb, vt, bias_t, gsub_row)

    return lax.cond(
        score_bound <= MAX_UNSTABILISED_SCORE,
        lambda: call(True, [qbd, acc, probs, probs], "attn_prompt_bounded"),
        lambda: call(False, [qbd, acc, scores, scores, stat, stat, stat], "attn_prompt"))


def _attn_sample_kernel(sc_ref, q_ref, kt_ref, vc_ref, ktn_ref, vn_ref, bc_ref, bn_ref, gsub_ref,
                        o_ref, qbd_ref, m_ref, l_ref, acc_ref):
    j = pl.program_id(1)
    tq, cols = q_ref.shape[1], q_ref.shape[2]
    lam = sc_ref[N_HEADS]

    @pl.when(j == 0)
    def _():
        q = q_ref[0].astype(F32)
        col = lax.broadcasted_iota(jnp.int32, (tq, cols), 1)
        for hm in range(cols // HEAD_DIM):
            mine = (col >= hm * HEAD_DIM) & (col < (hm + 1) * HEAD_DIM)
            qbd_ref[hm * tq:(hm + 1) * tq, :] = jnp.where(mine, q, 0.0)
        m_ref[...] = jnp.full(m_ref.shape, NEG_INF, F32)
        l_ref[...] = jnp.zeros(l_ref.shape, F32)
        acc_ref[...] = jnp.zeros(acc_ref.shape, F32)

    def update(kt, v_of_head, bias):
        s = jnp.dot(qbd_ref[...].astype(BF16), kt, preferred_element_type=F32) + bias
        m_prev = m_ref[...]
        m_new = jnp.maximum(m_prev, jnp.max(s, axis=-1, keepdims=True))
        alpha = jnp.exp2(m_prev - m_new)
        p = jnp.exp2(s - m_new)
        l_ref[...] = alpha * l_ref[...] + jnp.sum(p, axis=-1, keepdims=True)
        pb = p.astype(BF16)
        for h in range(N_HEADS):
            rows = slice(h * 2 * tq, (h + 1) * 2 * tq)
            acc_ref[rows, :] = alpha[rows] * acc_ref[rows, :] + jnp.dot(
                pb[rows], v_of_head(h), preferred_element_type=F32)
        m_ref[...] = m_new

    tk = kt_ref.shape[2]
    update(kt_ref[0].astype(BF16),
           lambda h: vc_ref[0, pl.ds(h, tk, stride=N_HEADS), :].astype(BF16), bc_ref[j])

    @pl.when(j == pl.num_programs(1) - 1)
    def _():
        update(ktn_ref[0], lambda h: vn_ref[0, :, h * V_DIM:(h + 1) * V_DIM], bn_ref[...])
        for h in range(N_HEADS):
            r0 = h * 2 * tq
            inv0 = 1.0 / l_ref[r0:r0 + tq, :]
            inv1 = 1.0 / l_ref[r0 + tq:r0 + 2 * tq, :]
            o = acc_ref[r0:r0 + tq, :] * inv0 - lam * (acc_ref[r0 + tq:r0 + 2 * tq, :] * inv1)
            o_ref[0, :, h * V_DIM:(h + 1) * V_DIM] = _rms(o, gsub_ref[...]).astype(BF16)


def _attn_sample(scalars, q, kt, v4, ktn, vn, bias_c, bias_n, gsub_row):
    b, tq, cols = q.shape
    p = kt.shape[2]
    nkc = p // CACHE_TK
    rows = 2 * N_HEADS * tq
    return pl.pallas_call(
        _attn_sample_kernel,
        grid=(b, nkc),
        in_specs=[pl.BlockSpec(memory_space=pltpu.SMEM),
                  pl.BlockSpec((1, tq, cols), lambda bi, j: (bi, 0, 0)),
                  pl.BlockSpec((1, cols, CACHE_TK), lambda bi, j: (bi, 0, j)),
                  pl.BlockSpec((1, CACHE_TK * N_HEADS, V_DIM), lambda bi, j: (bi, j, 0)),
                  pl.BlockSpec((1, cols, NEW_KEY_PAD), lambda bi, j: (bi, 0, 0)),
                  pl.BlockSpec((1, NEW_KEY_PAD, cols), lambda bi, j: (bi, 0, 0)),
                  _const_spec(bias_c.shape), _const_spec(bias_n.shape), _const_spec((1, V_DIM))],
        out_specs=pl.BlockSpec((1, tq, cols), lambda bi, j: (bi, 0, 0)),
        out_shape=jax.ShapeDtypeStruct((b, tq, cols), BF16),
        scratch_shapes=[pltpu.VMEM((rows, cols), F32),
                        pltpu.VMEM((rows, 1), F32),
                        pltpu.VMEM((rows, 1), F32),
                        pltpu.VMEM((rows, V_DIM), F32)],
        compiler_params=_cparams(("parallel", "arbitrary")),
        name="attn_sample",
    )(scalars, q, kt, v4, ktn, vn, bias_c, bias_n, gsub_row)


def _mix_kernel(x_ref, o_ref, c_ref, hist_ref, wconv_ref, bconv_ref, lng_ref, lnb_ref,
                woo_ref, woc_ref, gcross_ref, wmq_ref, gmq_ref, mk_ref, mv_ref, wmo_ref,
                out_ref, ext_ref, xs_ref, cv_ref, ca_ref):
    bb, tt, d = x_ref.shape
    rows = bb * tt
    n_mem = mk_ref.shape[1] // MEM_HEADS
    rc = min(CONV_ROWS, tt)
    lead = CONV_HALO - (CONV_K - 1)

    for b in range(bb):
        ext_ref[b, :CONV_HALO, :] = hist_ref[b, 0]
        ext_ref[b, CONV_HALO:, :] = c_ref[b]
    span = xs_ref.shape[2]
    for b in range(bb):
        for r in range(1, 8):
            xs_ref[r - 1, b] = ext_ref[b, r:r + span, :]
    for b in range(bb):
        for r0 in range(0, tt, rc):
            acc = jnp.zeros((rc, c_ref.shape[2]), F32) + bconv_ref[...]
            for k in range(CONV_K):
                a, r = divmod(k + lead, 8)
                lo = r0 + 8 * a
                src = ext_ref[b, lo:lo + rc, :] if r == 0 else xs_ref[r - 1, b, lo:lo + rc, :]
                acc = acc + wconv_ref[k:k + 1, :] * src
            mu = jnp.mean(acc, axis=-1, keepdims=True)
            xc = acc - mu
            var = jnp.mean(xc * xc, axis=-1, keepdims=True)
            y = xc * lax.rsqrt(var + EPS) * lng_ref[...] + lnb_ref[...]
            cv_ref[b * tt + r0:b * tt + r0 + rc, :] = (y * jax.nn.sigmoid(y)).astype(BF16)

    x = x_ref[...].reshape(rows, d)
    o = o_ref[...].reshape(rows, o_ref.shape[2])
    x1 = (x + jnp.dot(o, woo_ref[...], preferred_element_type=F32)
          + jnp.dot(cv_ref[...], woc_ref[...], preferred_element_type=F32))

    hc = _rms(x1, gcross_ref[...]).astype(BF16)
    qm = jnp.dot(hc, wmq_ref[...], preferred_element_type=F32)
    for h in range(MEM_HEADS):
        cols = slice(h * MEM_HEAD_DIM, (h + 1) * MEM_HEAD_DIM)
        qn = _rms(qm[:, cols], gmq_ref[...]).astype(BF16)
        for b in range(bb):
            mem_rows = pl.ds(h, n_mem, stride=MEM_HEADS)
            mk = mk_ref[b, mem_rows, :].astype(BF16)
            mv = mv_ref[b, mem_rows, :].astype(BF16)
            s = lax.dot_general(qn[b * tt:(b + 1) * tt], mk, (((1,), (1,)), ((), ())),
                                preferred_element_type=F32)
            p = jnp.exp2(s - jnp.max(s, axis=-1, keepdims=True))
            l = jnp.sum(p, axis=-1, keepdims=True)
            oh = jnp.dot(p.astype(BF16), mv, preferred_element_type=F32) / l
            ca_ref[b * tt:(b + 1) * tt, cols] = oh.astype(BF16)
    x2 = x1 + jnp.dot(ca_ref[...], wmo_ref[...], preferred_element_type=F32)
    out_ref[...] = x2.reshape(bb, tt, d)


def _mix(x, o, c, hist, wconv, bconv, lng, lnb, woo, woc, gcross, wmq, gmq_row, mk, mv, wmo,
         *, bb, tt):
    b, t, d = x.shape
    cw = c.shape[2]
    mem_rows, mhd = mk.shape[1], mk.shape[2]
    mw = wmq.shape[1]
    rows = bb * tt
    tile = lambda width: pl.BlockSpec((bb, tt, width), lambda bi, ti: (bi, ti, 0))
    return pl.pallas_call(
        _mix_kernel,
        grid=(b // bb, t // tt),
        in_specs=[tile(d), tile(cw), tile(cw),
                  pl.BlockSpec((bb, 1, CONV_HALO, cw), lambda bi, ti: (bi, ti, 0, 0)),
                  _const_spec(wconv.shape), _const_spec((1, cw)), _const_spec((1, cw)),
                  _const_spec((1, cw)), _const_spec(woo.shape), _const_spec(woc.shape),
                  _const_spec((1, d)), _const_spec(wmq.shape), _const_spec((1, MEM_HEAD_DIM)),
                  pl.BlockSpec((bb, mem_rows, mhd), lambda bi, ti: (bi, 0, 0)),
                  pl.BlockSpec((bb, mem_rows, mhd), lambda bi, ti: (bi, 0, 0)),
                  _const_spec(wmo.shape)],
        out_specs=tile(d),
        out_shape=jax.ShapeDtypeStruct((b, t, d), F32),
        scratch_shapes=[pltpu.VMEM((bb, CONV_HALO + tt, cw), F32),
                        pltpu.VMEM((7, bb, CONV_HALO - 8 + tt, cw), F32),
                        pltpu.VMEM((rows, cw), BF16),
                        pltpu.VMEM((rows, mw), BF16)],
        compiler_params=_cparams(("parallel", "parallel")),
        name="mix",
    )(x, o, c, hist, wconv, bconv, lng, lnb, woo, woc, gcross, wmq, gmq_row, mk, mv, wmo)


def _mlp_kernel(x_ref, g_ref, w1_ref, w2_ref, out_ref):
    x = x_ref[...]
    hf = _rms(x, g_ref[...]).astype(BF16)
    u = jnp.maximum(jnp.dot(hf, w1_ref[...], preferred_element_type=F32), 0.0)
    out_ref[...] = x + jnp.dot((u * u).astype(BF16), w2_ref[...], preferred_element_type=F32)


def _mlp(x2d, g_ffn, w1, w2):
    n, d = x2d.shape
    tm = min(ROW_TILE, n)
    return pl.pallas_call(
        _mlp_kernel,
        grid=(n // tm,),
        in_specs=[pl.BlockSpec((tm, d), lambda i: (i, 0)), _const_spec((1, d)),
                  _const_spec(w1.shape), _const_spec(w2.shape)],
        out_specs=pl.BlockSpec((tm, d), lambda i: (i, 0)),
        out_shape=jax.ShapeDtypeStruct((n, d), F32),
        compiler_params=_cparams(("parallel",)),
        name="mlp",
    )(x2d, g_ffn.reshape(1, d), w1, w2)


def _rel_bucket(rel):
    half = N_BUCKETS // 2
    max_exact = half // 2
    ret = jnp.where(rel > 0, half, 0)
    n = jnp.abs(rel)
    nf = jnp.maximum(n, 1).astype(jnp.float32)
    large = max_exact + (jnp.log(nf / max_exact) / math.log(MAX_DISTANCE / max_exact)
                         * (half - max_exact)).astype(jnp.int32)
    large = jnp.minimum(large, half - 1)
    return ret + jnp.where(n < max_exact, n, large)


def _masked_bias(rel_table, q_pos, k_pos):
    bucket = _rel_bucket(k_pos[None, :] - q_pos[:, None])[None]
    table = rel_table.astype(F32) * LOG2E
    bias = jnp.zeros((rel_table.shape[1],) + bucket.shape[1:], F32)
    for bkt in range(N_BUCKETS):
        bias = jnp.where(bucket == bkt, table[bkt][:, None, None], bias)
    mask = (k_pos[None, :] // CHUNK) <= (q_pos[:, None] // CHUNK)
    return jnp.where(mask[None], bias, NEG_INF)


def _far_bucket_is_saturated(min_distance):
    half = N_BUCKETS // 2
    max_exact = half // 2
    large = max_exact + int(np.log(min_distance / max_exact) / math.log(MAX_DISTANCE / max_exact)
                            * (half - max_exact) * (1 - 1e-6))
    return large >= half - 1


def _layer(x, k_past, v_past, c_past, mk, mv, w, lam, lam_init):
    b, t, d = x.shape
    prompt = k_past is None
    cols = N_HEADS * HEAD_COLS
    gsub_row = (w["g_sub"] * (1.0 - lam_init)).reshape(1, V_DIM)

    if prompt:
        assert t % ROW_TILE == 0 and ROW_TILE % ATTN_TQ == 0 and ATTN_TQ == ATTN_TK
        assert ATTN_TK % CHUNK == 0 and _far_bucket_is_saturated(ATTN_TK + 1)
        qt, kb, vt, kt, v4, c = _inproj(x, w["g_mix"], w["w_in"], w["gq_row"], w["gk_row"],
                                        w["seg"], emit_t=True)
        k_out = jnp.transpose(kt.reshape(b, N_HEADS, 2, HEAD_DIM, t), (0, 4, 1, 2, 3))
        v_out = v4.reshape(b, t, N_HEADS, V_DIM)
        q_pos = ATTN_TQ + jnp.arange(ATTN_TQ, dtype=jnp.int32)
        near = _masked_bias(w["rel_table"], q_pos, jnp.arange(2 * ATTN_TK, dtype=jnp.int32))
        near = jnp.stack([near[:, :, :ATTN_TK], near[:, :, ATTN_TK:]], axis=1)
        near_t = jnp.swapaxes(near, 2, 3)
        bias_t = jnp.concatenate([near_t, near_t], axis=3)
        far = w["rel_table"][_rel_bucket(jnp.int32(-(ATTN_TK + 1)))].astype(F32) * LOG2E
        scalars = jnp.concatenate([far, lam.reshape(1), jnp.exp2(-far)]).astype(F32)
        score_bound = (HEAD_DIM * jnp.max(jnp.abs(w["gq_row"])) * jnp.max(jnp.abs(w["gk_row"]))
                       + LOG2E * jnp.max(jnp.abs(w["rel_table"])))
        o = _attn_prompt(scalars, score_bound, qt, kb, vt, bias_t, gsub_row)
        bb, tt = 1, ROW_TILE
    else:
        p = k_past.shape[2]
        assert p % CACHE_TK == 0 and t <= NEW_KEY_PAD and t % 16 == 0
        qb, k, v, c = _inproj(x.reshape(1, b * t, d), w["g_mix"], w["w_in"], w["gq_row"],
                              w["gk_row"], w["seg"], emit_t=False)
        qb, k, v, c = (a.reshape(b, t, cols) for a in (qb, k, v, c))
        k_out = k.reshape(b, t, N_HEADS, 2, HEAD_DIM)
        v_out = v.reshape(b, t, N_HEADS, V_DIM)
        ktn = jnp.pad(jnp.swapaxes(k, 1, 2).astype(BF16), ((0, 0), (0, 0), (0, NEW_KEY_PAD - t)))
        vn = jnp.pad(v.astype(BF16), ((0, 0), (0, NEW_KEY_PAD - t), (0, 0)))
        q_pos = p + jnp.arange(t, dtype=jnp.int32)
        bias = _masked_bias(w["rel_table"], q_pos, jnp.arange(p + NEW_KEY_PAD, dtype=jnp.int32))
        bias = jnp.where(jnp.arange(p + NEW_KEY_PAD) < p + t, bias, NEG_INF)
        rows = 2 * N_HEADS * t
        bias = jnp.broadcast_to(bias[:, None], (N_HEADS, 2, t, p + NEW_KEY_PAD)).reshape(rows, -1)
        bias_c = jnp.swapaxes(bias[:, :p].reshape(rows, p // CACHE_TK, CACHE_TK), 0, 1)
        scalars = jnp.concatenate([jnp.zeros((N_HEADS,), F32), lam.reshape(1)]).astype(F32)
        o = _attn_sample(scalars, qb, k_past, v_past, ktn, vn, bias_c, bias[:, p:], gsub_row)
        bb, tt = SAMPLE_BATCH_TILE, t
        assert b % bb == 0

    nt = t // tt
    first = jnp.pad(c_past, ((0, 0), (CONV_HALO - (CONV_K - 1), 0), (0, 0)))[:, None]
    if nt > 1:
        tails = c.reshape(b, nt, tt, c.shape[2])[:, :-1, tt - CONV_HALO:, :]
        hist = jnp.concatenate([first, tails], axis=1)
    else:
        hist = first
    x2 = _mix(x, o, c, hist, w["w_conv"], w["b_conv"], w["ln_g"], w["ln_b"], w["w_out_o"],
              w["w_out_c"], w["g_cross"], w["w_mq"], w["gmq_row"], mk, mv, w["w_mo"], bb=bb, tt=tt)
    y = _mlp(x2.reshape(b * t, d), w["g_ffn"], w["w_ff1"], w["w_ff2"]).reshape(b, t, d)
    if t >= CONV_K - 1:
        c_hist_tail = c[:, t - (CONV_K - 1):]
    else:
        c_hist_tail = jnp.concatenate([c_past[:, t:], c], axis=1)
    return y, k_out, v_out, c_hist_tail


def kernel(x_prompt, x_sample, cache_k, cache_v, cache_conv, cache_mem_k, cache_mem_v, mem_prompt,
           rel_table, g_mix, w_in, g_q, g_k, lam_vec, g_sub, w_conv, b_conv, ln_g, ln_b, w_out,
           g_cross, g_mem, w_mq, w_mk, w_mv, g_mq, g_mk, w_mo, g_ffn, w_ff1, w_ff2):
    depth = g_mix.shape[0]
    assert depth == 1
    b, t, d = x_prompt.shape
    bs, ts, _ = x_sample.shape
    cols = N_HEADS * HEAD_COLS
    cw = w_conv.shape[2]
    attn_w = N_HEADS * V_DIM
    l = 0
    lam_init = 0.8 - 0.6 * math.exp(-0.3 * l)
    lp = lam_vec[l].astype(F32)
    lam = jnp.exp(jnp.sum(lp[0] * lp[1])) - jnp.exp(jnp.sum(lp[2] * lp[3])) + lam_init

    seg = jnp.kron(jnp.eye(cols // HEAD_DIM, dtype=F32),
                   jnp.full((HEAD_DIM, HEAD_DIM), 1.0 / HEAD_DIM, F32)).astype(BF16)
    n_maps = cols // HEAD_DIM
    w = dict(
        rel_table=rel_table, g_mix=g_mix[l], w_in=w_in[l].astype(BF16), seg=seg,
        gq_row=jnp.tile(g_q[l] * (HEAD_DIM ** -0.5 * LOG2E), n_maps).reshape(1, cols),
        gk_row=jnp.tile(g_k[l], n_maps).reshape(1, cols),
        g_sub=g_sub[l],
        w_conv=jnp.pad(w_conv[l], ((0, CONV_HALO - CONV_K), (0, 0))),
        b_conv=b_conv[l].reshape(1, cw), ln_g=ln_g[l].reshape(1, cw), ln_b=ln_b[l].reshape(1, cw),
        w_out_o=w_out[l][:attn_w].astype(BF16), w_out_c=w_out[l][attn_w:].astype(BF16),
        g_cross=g_cross[l].reshape(1, d), w_mq=w_mq[l].astype(BF16),
        gmq_row=(g_mq[l] * (MEM_HEAD_DIM ** -0.5 * LOG2E)).reshape(1, MEM_HEAD_DIM),
        w_mo=w_mo[l].astype(BF16), g_ffn=g_ffn[l],
        w_ff1=w_ff1[l].astype(BF16), w_ff2=w_ff2[l].astype(BF16),
    )

    n_mem = mem_prompt.shape[1]
    mk_p, mv_p = _memkv(mem_prompt.reshape(b * n_mem, d), g_mem[l], w_mk[l], w_mv[l], g_mk[l])
    mk_p = mk_p.reshape(b, n_mem * MEM_HEADS, MEM_HEAD_DIM)
    mv_p = mv_p.reshape(b, n_mem * MEM_HEADS, MEM_HEAD_DIM)

    zero_conv = jnp.zeros((b, CONV_K - 1, cw), F32)
    yp, kp, vp, cp = _layer(x_prompt, None, None, zero_conv, mk_p, mv_p, w, lam, lam_init)

    p = cache_k.shape[2]
    k_past = jnp.transpose(cache_k[l], (0, 2, 3, 4, 1)).reshape(bs, cols, p)
    v_past = cache_v[l].reshape(bs, p * N_HEADS, V_DIM)
    ys, kn, vn, cn = _layer(x_sample, k_past, v_past, cache_conv[l],
                            cache_mem_k[l].reshape(bs, n_mem * MEM_HEADS, MEM_HEAD_DIM),
                            cache_mem_v[l].reshape(bs, n_mem * MEM_HEADS, MEM_HEAD_DIM),
                            w, lam, lam_init)

    return (yp, ys, kp[None], vp[None], cp[None],
            mk_p.reshape(1, b, n_mem, MEM_HEADS, MEM_HEAD_DIM),
            mv_p.reshape(1, b, n_mem, MEM_HEADS, MEM_HEAD_DIM),
            kn[None], vn[None], cn[None])
```

```python
import functools
import math

import jax
import jax.numpy as jnp
import numpy as np
from jax import lax
from jax.experimental import pallas as pl
from jax.experimental.pallas import tpu as pltpu

F32 = jnp.float32
BF16 = jnp.bfloat16

CHUNK = 64
N_HEADS = 4
HEAD_DIM = 64
V_DIM = 2 * HEAD_DIM
HEAD_COLS = 2 * HEAD_DIM
CONV_K = 31
CONV_HALO = 32
N_BUCKETS = 32
MAX_DISTANCE = 128
MEM_HEADS = 4
MEM_HEAD_DIM = 128
EPS = 1e-6
NEG_INF = -1e30
LOG2E = math.log2(math.e)

ATTN_TQ = 256
ATTN_TK = 256
VT_ROWS = V_DIM + 16
MAX_UNSTABILISED_SCORE = 100.0
CACHE_TK = 4096
NEW_KEY_PAD = 128
ROW_TILE = 512
CONV_ROWS = 64
SAMPLE_BATCH_TILE = 8
VMEM_LIMIT = 56 * 1024 * 1024


def _cparams(sem):
    return pltpu.CompilerParams(dimension_semantics=sem, vmem_limit_bytes=VMEM_LIMIT)


def _rms(x, g):
    ms = jnp.mean(x * x, axis=-1, keepdims=True)
    return x * lax.rsqrt(ms + EPS) * g


def _const_spec(shape):
    return pl.BlockSpec(shape, lambda *_: (0,) * len(shape), pipeline_mode=pl.Buffered(1))


def _memkv_kernel(mem_ref, g_ref, wk_ref, wv_ref, gk_ref, mk_ref, mv_ref):
    m = _rms(mem_ref[...], g_ref[...]).astype(BF16)
    zk = jnp.dot(m, wk_ref[...], preferred_element_type=F32)
    zv = jnp.dot(m, wv_ref[...], preferred_element_type=F32)
    tm = mem_ref.shape[0]
    for h in range(MEM_HEADS):
        sl = slice(h * MEM_HEAD_DIM, (h + 1) * MEM_HEAD_DIM)
        rows = pl.ds(h, tm, stride=MEM_HEADS)
        mk_ref[rows, :] = _rms(zk[:, sl], gk_ref[...])
        mv_ref[rows, :] = zv[:, sl]


def _memkv(mem2d, g_mem, w_mk, w_mv, g_mk):
    n, d = mem2d.shape
    w = w_mk.shape[1]
    tm = min(ROW_TILE, n)
    return pl.pallas_call(
        _memkv_kernel,
        grid=(n // tm,),
        in_specs=[pl.BlockSpec((tm, d), lambda i: (i, 0)), _const_spec((1, d)),
                  _const_spec((d, w)), _const_spec((d, w)), _const_spec((1, MEM_HEAD_DIM))],
        out_specs=[pl.BlockSpec((tm * MEM_HEADS, MEM_HEAD_DIM), lambda i: (i, 0))] * 2,
        out_shape=[jax.ShapeDtypeStruct((n * MEM_HEADS, MEM_HEAD_DIM), F32)] * 2,
        compiler_params=_cparams(("parallel",)),
        name="memkv",
    )(mem2d, g_mem.reshape(1, d), w_mk.astype(BF16), w_mv.astype(BF16),
      g_mk.reshape(1, MEM_HEAD_DIM))


def _inproj_kernel(x_ref, g_ref, w_ref, gq_ref, gk_ref, *refs, qk_cols, emit_t):
    if emit_t:
        qt_ref, kb_ref, vt_ref, kt_ref, v4_ref, c_ref = refs
    else:
        seg_ref, qb_ref, k_ref, v_ref, c_ref = refs

    def mapnorm(z, g):
        ms = jnp.dot((z * z).astype(BF16), seg_ref[...], preferred_element_type=F32)
        return z * lax.rsqrt(ms + EPS) * g

    def mapnorm_t(z, g_t):
        zt = z.T
        z3 = zt.reshape(zt.shape[0] // HEAD_DIM, HEAD_DIM, zt.shape[1])
        ms = jnp.mean(z3 * z3, axis=1, keepdims=True)
        return (z3 * lax.rsqrt(ms + EPS)).reshape(zt.shape) * g_t

    h = _rms(x_ref[0], g_ref[...]).astype(BF16)

    def proj(lo, hi):
        return jnp.dot(h, w_ref[:, lo:hi], preferred_element_type=F32)

    c0 = qk_cols
    zq = proj(0, c0)
    zk = proj(c0, 2 * c0)
    v = proj(2 * c0, 3 * c0)
    a = proj(3 * c0, 4 * c0)
    gate = proj(4 * c0, 5 * c0)
    c_ref[0] = a * jax.nn.sigmoid(gate)
    if emit_t:
        kt = mapnorm_t(zk, gk_ref[...])
        kt_ref[0] = kt
        kb_ref[0] = kt.T.astype(BF16)
        tm = x_ref.shape[1]
        for hh in range(N_HEADS):
            v4_ref[0, pl.ds(hh, tm, stride=N_HEADS), :] = v[:, hh * V_DIM:(hh + 1) * V_DIM]
        qt = mapnorm_t(zq, gq_ref[...]).astype(BF16)
        vt = v.T.astype(BF16)
        for j in range(qt_ref.shape[1]):
            qt_ref[0, j] = qt[:, j * ATTN_TQ:(j + 1) * ATTN_TQ]
        ones_row = (lax.broadcasted_iota(jnp.int32, (VT_ROWS - V_DIM, ATTN_TK), 0) == 0).astype(BF16)
        for j in range(vt_ref.shape[1]):
            for hh in range(N_HEADS):
                vt_ref[0, j, hh, :V_DIM, :] = vt[hh * V_DIM:(hh + 1) * V_DIM,
                                                 j * ATTN_TK:(j + 1) * ATTN_TK]
                vt_ref[0, j, hh, V_DIM:, :] = ones_row
    else:
        k_ref[0] = mapnorm(zk, gk_ref[...])
        v_ref[0] = v
        qb_ref[0] = mapnorm(zq, gq_ref[...]).astype(BF16)


def _inproj(x, g_mix, w_in_bf, gq_row, gk_row, seg, *, emit_t):
    b, t, d = x.shape
    cols = gq_row.shape[1]
    tm = min(ROW_TILE, t)
    nt = t // tm
    row = lambda width: pl.BlockSpec((1, tm, width), lambda bi, ti: (bi, ti, 0))
    f32_out = jax.ShapeDtypeStruct((b, t, cols), F32)
    if emit_t:
        nq, nk = tm // ATTN_TQ, tm // ATTN_TK
        out_shape = [jax.ShapeDtypeStruct((b, t // ATTN_TQ, cols, ATTN_TQ), BF16),
                     jax.ShapeDtypeStruct((b, t, cols), BF16),
                     jax.ShapeDtypeStruct((b, t // ATTN_TK, N_HEADS, VT_ROWS, ATTN_TK), BF16),
                     jax.ShapeDtypeStruct((b, cols, t), F32),
                     jax.ShapeDtypeStruct((b, t * N_HEADS, V_DIM), F32),
                     f32_out]
        out_specs = [pl.BlockSpec((1, nq, cols, ATTN_TQ), lambda bi, ti: (bi, ti, 0, 0)),
                     row(cols),
                     pl.BlockSpec((1, nk, N_HEADS, VT_ROWS, ATTN_TK),
                                  lambda bi, ti: (bi, ti, 0, 0, 0)),
                     pl.BlockSpec((1, cols, tm), lambda bi, ti: (bi, 0, ti)),
                     pl.BlockSpec((1, tm * N_HEADS, V_DIM), lambda bi, ti: (bi, ti, 0)),
                     row(cols)]
        gains = [jnp.broadcast_to(g.reshape(cols, 1), (cols, tm)) for g in (gq_row, gk_row)]
        gain_specs = [_const_spec((cols, tm))] * 2
    else:
        out_shape = [jax.ShapeDtypeStruct((b, t, cols), BF16)] + [f32_out] * 3
        out_specs = [row(cols)] * 4
        gains = [gq_row, gk_row, seg]
        gain_specs = [_const_spec((1, cols)), _const_spec((1, cols)), _const_spec((cols, cols))]
    return pl.pallas_call(
        functools.partial(_inproj_kernel, qk_cols=cols, emit_t=emit_t),
        grid=(b, nt),
        in_specs=[row(d), _const_spec((1, d)), _const_spec(w_in_bf.shape)] + gain_specs,
        out_specs=out_specs,
        out_shape=out_shape,
        compiler_params=_cparams(("parallel", "parallel")),
        name="inproj",
    )(x, g_mix.reshape(1, d), w_in_bf, *gains)


PREV, DIAG, FAR = 0, 1, None


def _build_qbd(qt_ref, qbd_ref):
    tq = ATTN_TQ
    upper = lax.broadcasted_iota(jnp.int32, (HEAD_COLS, tq), 0) < HEAD_DIM
    for h in range(N_HEADS):
        qh = qt_ref[0, 0, h * HEAD_COLS:(h + 1) * HEAD_COLS, :]
        zero = jnp.zeros_like(qh)
        qbd_ref[h, :, :tq] = jnp.where(upper, qh, zero)
        qbd_ref[h, :, tq:] = jnp.where(upper, zero, qh)


def _attn_finalize(acc_ref, denom, lam, gsub_ref, o_ref):
    tq = ATTN_TQ
    for h in range(N_HEADS):
        acc = acc_ref[h, :V_DIM, :]
        r = 1.0 / denom(h)
        ot = acc[:, :tq] * r[:, :tq] - lam * (acc[:, tq:] * r[:, tq:])
        ms = jnp.mean(ot * ot, axis=0, keepdims=True)
        ot = ot * lax.rsqrt(ms + EPS)
        o_ref[0, :, h * V_DIM:(h + 1) * V_DIM] = (ot.T * gsub_ref[...]).astype(BF16)


def _attn_prompt_kernel(sc_ref, qt_ref, kb_ref, vt_ref, bias_ref, gsub_ref, o_ref,
                        qbd_ref, acc_ref, sa_ref, sb_ref, *stat_refs, bounded):
    i = pl.program_id(1)
    tq, tk = ATTN_TQ, ATTN_TK
    lam = sc_ref[N_HEADS]
    if bounded:
        lsum_ref, = stat_refs
        slot_a, slot_b = (sa_ref, None), (sb_ref, None)
        lsum_ref[...] = jnp.zeros(lsum_ref.shape, F32)
    else:
        m_ref, mca_ref, mcb_ref = stat_refs
        slot_a, slot_b = (sa_ref, mca_ref), (sb_ref, mcb_ref)
        m_ref[...] = jnp.full(m_ref.shape, NEG_INF, F32)

    _build_qbd(qt_ref, qbd_ref)
    acc_ref[...] = jnp.zeros(acc_ref.shape, F32)

    def stage(j, slot, tile):
        s_ref, mc_ref = slot
        row0 = pl.multiple_of(j * tk, tk)
        for h in range(N_HEADS):
            kh = kb_ref[0, pl.ds(row0, tk), h * HEAD_COLS:(h + 1) * HEAD_COLS]
            s = jnp.dot(kh, qbd_ref[h], preferred_element_type=F32)
            if tile is not FAR:
                s = s + bias_ref[h, tile]
            if bounded:
                p = jnp.exp2(s)
                s_ref[h] = p.astype(BF16)
                lsum_ref[h] += jnp.sum(p.reshape(tk // 8, 8, 2 * tq), axis=0)
            else:
                s_ref[h] = s
                mc_ref[h] = jnp.max(s, axis=0, keepdims=True)

    def consume(j, slot, tile):
        s_ref, mc_ref = slot
        for h in range(N_HEADS):
            if bounded:
                acc_ref[h, :V_DIM, :] += jnp.dot(vt_ref[0, j, h, :V_DIM, :], s_ref[h],
                                                 preferred_element_type=F32)
                continue
            m_prev = m_ref[h]
            if tile is FAR:
                shift = sc_ref[h]
                m_new = jnp.maximum(m_prev, mc_ref[h] + shift)
                p = jnp.exp2(s_ref[h] - (m_new - shift))
            else:
                m_new = jnp.maximum(m_prev, mc_ref[h])
                p = jnp.exp2(s_ref[h] - m_new)
            alpha = jnp.exp2(m_prev - m_new)
            pv = jnp.dot(vt_ref[0, j, h], p.astype(BF16), preferred_element_type=F32)
            acc_ref[h] = alpha * acc_ref[h] + pv
            m_ref[h] = m_new

    n_far = jnp.maximum(i - 1, 0)
    stage(i, slot_a, DIAG)

    @pl.when(i == 0)
    def _():
        consume(i, slot_a, DIAG)

    @pl.when(i >= 1)
    def _():
        consume(i, slot_a, DIAG)
        stage(i - 1, slot_b, PREV)

    @pl.when(i == 1)
    def _():
        consume(i - 1, slot_b, PREV)

    @pl.when(i >= 2)
    def _():
        consume(i - 1, slot_b, PREV)
        if bounded:
            for h in range(N_HEADS):
                acc_ref[h] = acc_ref[h] * sc_ref[N_HEADS + 1 + h]
                lsum_ref[h] = lsum_ref[h] * sc_ref[N_HEADS + 1 + h]
        stage(0, slot_a, FAR)

    n_pairs = jnp.maximum(n_far - 1, 0) // 2

    def pair_body(jj, carry):
        f = 2 * jj
        consume(f, slot_a, FAR)
        stage(f + 1, slot_b, FAR)
        consume(f + 1, slot_b, FAR)
        stage(f + 2, slot_a, FAR)
        return carry

    lax.fori_loop(0, n_pairs, pair_body, 0)
    f_last = 2 * n_pairs
    left = n_far - f_last

    @pl.when(left == 1)
    def _():
        consume(f_last, slot_a, FAR)

    @pl.when(left == 2)
    def _():
        consume(f_last, slot_a, FAR)
        stage(f_last + 1, slot_b, FAR)

    @pl.when(left == 2)
    def _():
        consume(f_last + 1, slot_b, FAR)

    if bounded:
        denom = lambda h: jnp.sum(lsum_ref[h], axis=0, keepdims=True)
    else:
        denom = lambda h: acc_ref[h, V_DIM:V_DIM + 1, :]
    _attn_finalize(acc_ref, denom, lam, gsub_ref, o_ref)


def _attn_prompt(scalars, score_bound, qt, kb, vt, bias_t, gsub_row):
    b, nq, cols, tq = qt.shape
    t = kb.shape[1]
    nk = vt.shape[1]
    qbd = pltpu.VMEM((N_HEADS, HEAD_COLS, 2 * tq), BF16)
    acc = pltpu.VMEM((N_HEADS, VT_ROWS, 2 * tq), F32)
    stat = pltpu.VMEM((N_HEADS, 1, 2 * tq), F32)
    scores = pltpu.VMEM((N_HEADS, ATTN_TK, 2 * tq), F32)
    probs = pltpu.VMEM((N_HEADS, ATTN_TK, 2 * tq), BF16)
    colsum = pltpu.VMEM((N_HEADS, 8, 2 * tq), F32)

    def call(bounded, scratch, name):
        return pl.pallas_call(
            functools.partial(_attn_prompt_kernel, bounded=bounded),
            grid=(b, nq),
            in_specs=[pl.BlockSpec(memory_space=pltpu.SMEM),
                      pl.BlockSpec((1, 1, cols, tq), lambda bi, i: (bi, i, 0, 0)),
                      pl.BlockSpec((1, t, cols), lambda bi, i: (bi, 0, 0)),
                      pl.BlockSpec((1, nk, N_HEADS, VT_ROWS, ATTN_TK),
                                   lambda bi, i: (bi, 0, 0, 0, 0)),
                      _const_spec(bias_t.shape), _const_spec((1, V_DIM))],
            out_specs=pl.BlockSpec((1, tq, cols), lambda bi, i: (bi, i, 0)),
            out_shape=jax.ShapeDtypeStruct((b, t, cols), BF16),
            scratch_shapes=scratch,
            compiler_params=_cparams(("parallel", "arbitrary")),
            name=name,
        )(scalars, qt, kb, vt, bias_t, gsub_row)

    return lax.cond(
        score_bound <= MAX_UNSTABILISED_SCORE,
        lambda: call(True, [qbd, acc, probs, probs, colsum], "attn_prompt_bounded"),
        lambda: call(False, [qbd, acc, scores, scores, stat, stat, stat], "attn_prompt"))


def _attn_sample_kernel(sc_ref, q_ref, kt_ref, vc_ref, ktn_ref, vn_ref, bc_ref, bn_ref, gsub_ref,
                        o_ref, qbd_ref, m_ref, l_ref, acc_ref):
    j = pl.program_id(1)
    tq, cols = q_ref.shape[1], q_ref.shape[2]
    lam = sc_ref[N_HEADS]

    @pl.when(j == 0)
    def _():
        q = q_ref[0].astype(F32)
        col = lax.broadcasted_iota(jnp.int32, (tq, cols), 1)
        for hm in range(cols // HEAD_DIM):
            mine = (col >= hm * HEAD_DIM) & (col < (hm + 1) * HEAD_DIM)
            qbd_ref[hm * tq:(hm + 1) * tq, :] = jnp.where(mine, q, 0.0)
        m_ref[...] = jnp.full(m_ref.shape, NEG_INF, F32)
        l_ref[...] = jnp.zeros(l_ref.shape, F32)
        acc_ref[...] = jnp.zeros(acc_ref.shape, F32)

    def update(kt, v_of_head, bias):
        s = jnp.dot(qbd_ref[...].astype(BF16), kt, preferred_element_type=F32) + bias
        m_prev = m_ref[...]
        m_new = jnp.maximum(m_prev, jnp.max(s, axis=-1, keepdims=True))
        alpha = jnp.exp2(m_prev - m_new)
        p = jnp.exp2(s - m_new)
        l_ref[...] = alpha * l_ref[...] + jnp.sum(p, axis=-1, keepdims=True)
        pb = p.astype(BF16)
        for h in range(N_HEADS):
            rows = slice(h * 2 * tq, (h + 1) * 2 * tq)
            acc_ref[rows, :] = alpha[rows] * acc_ref[rows, :] + jnp.dot(
                pb[rows], v_of_head(h), preferred_element_type=F32)
        m_ref[...] = m_new

    tk = kt_ref.shape[2]
    update(kt_ref[0].astype(BF16),
           lambda h: vc_ref[0, pl.ds(h, tk, stride=N_HEADS), :].astype(BF16), bc_ref[j])

    @pl.when(j == pl.num_programs(1) - 1)
    def _():
        update(ktn_ref[0], lambda h: vn_ref[0, :, h * V_DIM:(h + 1) * V_DIM], bn_ref[...])
        for h in range(N_HEADS):
            r0 = h * 2 * tq
            inv0 = 1.0 / l_ref[r0:r0 + tq, :]
            inv1 = 1.0 / l_ref[r0 + tq:r0 + 2 * tq, :]
            o = acc_ref[r0:r0 + tq, :] * inv0 - lam * (acc_ref[r0 + tq:r0 + 2 * tq, :] * inv1)
            o_ref[0, :, h * V_DIM:(h + 1) * V_DIM] = _rms(o, gsub_ref[...]).astype(BF16)


def _attn_sample(scalars, q, kt, v4, ktn, vn, bias_c, bias_n, gsub_row):
    b, tq, cols = q.shape
    p = kt.shape[2]
    nkc = p // CACHE_TK
    rows = 2 * N_HEADS * tq
    return pl.pallas_call(
        _attn_sample_kernel,
        grid=(b, nkc),
        in_specs=[pl.BlockSpec(memory_space=pltpu.SMEM),
                  pl.BlockSpec((1, tq, cols), lambda bi, j: (bi, 0, 0)),
                  pl.BlockSpec((1, cols, CACHE_TK), lambda bi, j: (bi, 0, j)),
                  pl.BlockSpec((1, CACHE_TK * N_HEADS, V_DIM), lambda bi, j: (bi, j, 0)),
                  pl.BlockSpec((1, cols, NEW_KEY_PAD), lambda bi, j: (bi, 0, 0)),
                  pl.BlockSpec((1, NEW_KEY_PAD, cols), lambda bi, j: (bi, 0, 0)),
                  _const_spec(bias_c.shape), _const_spec(bias_n.shape), _const_spec((1, V_DIM))],
        out_specs=pl.BlockSpec((1, tq, cols), lambda bi, j: (bi, 0, 0)),
        out_shape=jax.ShapeDtypeStruct((b, tq, cols), BF16),
        scratch_shapes=[pltpu.VMEM((rows, cols), F32),
                        pltpu.VMEM((rows, 1), F32),
                        pltpu.VMEM((rows, 1), F32),
                        pltpu.VMEM((rows, V_DIM), F32)],
        compiler_params=_cparams(("parallel", "arbitrary")),
        name="attn_sample",
    )(scalars, q, kt, v4, ktn, vn, bias_c, bias_n, gsub_row)


def _mix_kernel(x_ref, o_ref, c_ref, hist_ref, wconv_ref, bconv_ref, lng_ref, lnb_ref,
                woo_ref, woc_ref, gcross_ref, wmq_ref, gmq_ref, mk_ref, mv_ref, wmo_ref,
                out_ref, ext_ref, xs_ref, cv_ref, ca_ref):
    bb, tt, d = x_ref.shape
    rows = bb * tt
    n_mem = mk_ref.shape[1] // MEM_HEADS
    rc = min(CONV_ROWS, tt)
    lead = CONV_HALO - (CONV_K - 1)

    for b in range(bb):
        ext_ref[b, :CONV_HALO, :] = hist_ref[b, 0]
        ext_ref[b, CONV_HALO:, :] = c_ref[b]
    span = xs_ref.shape[2]
    for b in range(bb):
        for r in range(1, 8):
            xs_ref[r - 1, b] = ext_ref[b, r:r + span, :]
    for b in range(bb):
        for r0 in range(0, tt, rc):
            acc = jnp.zeros((rc, c_ref.shape[2]), F32) + bconv_ref[...]
            for k in range(CONV_K):
                a, r = divmod(k + lead, 8)
                lo = r0 + 8 * a
                src = ext_ref[b, lo:lo + rc, :] if r == 0 else xs_ref[r - 1, b, lo:lo + rc, :]
                acc = acc + wconv_ref[k:k + 1, :] * src
            mu = jnp.mean(acc, axis=-1, keepdims=True)
            xc = acc - mu
            var = jnp.mean(xc * xc, axis=-1, keepdims=True)
            y = xc * lax.rsqrt(var + EPS) * lng_ref[...] + lnb_ref[...]
            cv_ref[b * tt + r0:b * tt + r0 + rc, :] = (y * jax.nn.sigmoid(y)).astype(BF16)

    x = x_ref[...].reshape(rows, d)
    o = o_ref[...].reshape(rows, o_ref.shape[2])
    x1 = (x + jnp.dot(o, woo_ref[...], preferred_element_type=F32)
          + jnp.dot(cv_ref[...], woc_ref[...], preferred_element_type=F32))

    hc = _rms(x1, gcross_ref[...]).astype(BF16)
    qm = jnp.dot(hc, wmq_ref[...], preferred_element_type=F32)
    for h in range(MEM_HEADS):
        cols = slice(h * MEM_HEAD_DIM, (h + 1) * MEM_HEAD_DIM)
        qn = _rms(qm[:, cols], gmq_ref[...]).astype(BF16)
        for b in range(bb):
            mem_rows = pl.ds(h, n_mem, stride=MEM_HEADS)
            mk = mk_ref[b, mem_rows, :].astype(BF16)
            mv = mv_ref[b, mem_rows, :].astype(BF16)
            s = lax.dot_general(qn[b * tt:(b + 1) * tt], mk, (((1,), (1,)), ((), ())),
                                preferred_element_type=F32)
            p = jnp.exp2(s - jnp.max(s, axis=-1, keepdims=True))
            l = jnp.sum(p, axis=-1, keepdims=True)
            oh = jnp.dot(p.astype(BF16), mv, preferred_element_type=F32) / l
            ca_ref[b * tt:(b + 1) * tt, cols] = oh.astype(BF16)
    x2 = x1 + jnp.dot(ca_ref[...], wmo_ref[...], preferred_element_type=F32)
    out_ref[...] = x2.reshape(bb, tt, d)


def _mix(x, o, c, hist, wconv, bconv, lng, lnb, woo, woc, gcross, wmq, gmq_row, mk, mv, wmo,
         *, bb, tt):
    b, t, d = x.shape
    cw = c.shape[2]
    mem_rows, mhd = mk.shape[1], mk.shape[2]
    mw = wmq.shape[1]
    rows = bb * tt
    tile = lambda width: pl.BlockSpec((bb, tt, width), lambda bi, ti: (bi, ti, 0))
    return pl.pallas_call(
        _mix_kernel,
        grid=(b // bb, t // tt),
        in_specs=[tile(d), tile(cw), tile(cw),
                  pl.BlockSpec((bb, 1, CONV_HALO, cw), lambda bi, ti: (bi, ti, 0, 0)),
                  _const_spec(wconv.shape), _const_spec((1, cw)), _const_spec((1, cw)),
                  _const_spec((1, cw)), _const_spec(woo.shape), _const_spec(woc.shape),
                  _const_spec((1, d)), _const_spec(wmq.shape), _const_spec((1, MEM_HEAD_DIM)),
                  pl.BlockSpec((bb, mem_rows, mhd), lambda bi, ti: (bi, 0, 0)),
                  pl.BlockSpec((bb, mem_rows, mhd), lambda bi, ti: (bi, 0, 0)),
                  _const_spec(wmo.shape)],
        out_specs=tile(d),
        out_shape=jax.ShapeDtypeStruct((b, t, d), F32),
        scratch_shapes=[pltpu.VMEM((bb, CONV_HALO + tt, cw), F32),
                        pltpu.VMEM((7, bb, CONV_HALO - 8 + tt, cw), F32),
                        pltpu.VMEM((rows, cw), BF16),
                        pltpu.VMEM((rows, mw), BF16)],
        compiler_params=_cparams(("parallel", "parallel")),
        name="mix",
    )(x, o, c, hist, wconv, bconv, lng, lnb, woo, woc, gcross, wmq, gmq_row, mk, mv, wmo)


def _mlp_kernel(x_ref, g_ref, w1_ref, w2_ref, out_ref):
    x = x_ref[...]
    hf = _rms(x, g_ref[...]).astype(BF16)
    u = jnp.maximum(jnp.dot(hf, w1_ref[...], preferred_element_type=F32), 0.0)
    out_ref[...] = x + jnp.dot((u * u).astype(BF16), w2_ref[...], preferred_element_type=F32)


def _mlp(x2d, g_ffn, w1, w2):
    n, d = x2d.shape
    tm = min(ROW_TILE, n)
    return pl.pallas_call(
        _mlp_kernel,
        grid=(n // tm,),
        in_specs=[pl.BlockSpec((tm, d), lambda i: (i, 0)), _const_spec((1, d)),
                  _const_spec(w1.shape), _const_spec(w2.shape)],
        out_specs=pl.BlockSpec((tm, d), lambda i: (i, 0)),
        out_shape=jax.ShapeDtypeStruct((n, d), F32),
        compiler_params=_cparams(("parallel",)),
        name="mlp",
    )(x2d, g_ffn.reshape(1, d), w1, w2)


def _rel_bucket(rel):
    half = N_BUCKETS // 2
    max_exact = half // 2
    ret = jnp.where(rel > 0, half, 0)
    n = jnp.abs(rel)
    nf = jnp.maximum(n, 1).astype(jnp.float32)
    large = max_exact + (jnp.log(nf / max_exact) / math.log(MAX_DISTANCE / max_exact)
                         * (half - max_exact)).astype(jnp.int32)
    large = jnp.minimum(large, half - 1)
    return ret + jnp.where(n < max_exact, n, large)


def _masked_bias(rel_table, q_pos, k_pos):
    bucket = _rel_bucket(k_pos[None, :] - q_pos[:, None])[None]
    table = rel_table.astype(F32) * LOG2E
    bias = jnp.zeros((rel_table.shape[1],) + bucket.shape[1:], F32)
    for bkt in range(N_BUCKETS):
        bias = jnp.where(bucket == bkt, table[bkt][:, None, None], bias)
    mask = (k_pos[None, :] // CHUNK) <= (q_pos[:, None] // CHUNK)
    return jnp.where(mask[None], bias, NEG_INF)


def _far_bucket_is_saturated(min_distance):
    half = N_BUCKETS // 2
    max_exact = half // 2
    large = max_exact + int(np.log(min_distance / max_exact) / math.log(MAX_DISTANCE / max_exact)
                            * (half - max_exact) * (1 - 1e-6))
    return large >= half - 1


def _layer(x, k_past, v_past, c_past, mk, mv, w, lam, lam_init):
    b, t, d = x.shape
    prompt = k_past is None
    cols = N_HEADS * HEAD_COLS
    gsub_row = (w["g_sub"] * (1.0 - lam_init)).reshape(1, V_DIM)

    if prompt:
        assert t % ROW_TILE == 0 and ROW_TILE % ATTN_TQ == 0 and ATTN_TQ == ATTN_TK
        assert ATTN_TK % CHUNK == 0 and _far_bucket_is_saturated(ATTN_TK + 1)
        qt, kb, vt, kt, v4, c = _inproj(x, w["g_mix"], w["w_in"], w["gq_row"], w["gk_row"],
                                        w["seg"], emit_t=True)
        k_out = jnp.transpose(kt.reshape(b, N_HEADS, 2, HEAD_DIM, t), (0, 4, 1, 2, 3))
        v_out = v4.reshape(b, t, N_HEADS, V_DIM)
        q_pos = ATTN_TQ + jnp.arange(ATTN_TQ, dtype=jnp.int32)
        near = _masked_bias(w["rel_table"], q_pos, jnp.arange(2 * ATTN_TK, dtype=jnp.int32))
        near = jnp.stack([near[:, :, :ATTN_TK], near[:, :, ATTN_TK:]], axis=1)
        near_t = jnp.swapaxes(near, 2, 3)
        bias_t = jnp.concatenate([near_t, near_t], axis=3)
        far = w["rel_table"][_rel_bucket(jnp.int32(-(ATTN_TK + 1)))].astype(F32) * LOG2E
        scalars = jnp.concatenate([far, lam.reshape(1), jnp.exp2(-far)]).astype(F32)
        score_bound = (HEAD_DIM * jnp.max(jnp.abs(w["gq_row"])) * jnp.max(jnp.abs(w["gk_row"]))
                       + LOG2E * jnp.max(jnp.abs(w["rel_table"])))
        o = _attn_prompt(scalars, score_bound, qt, kb, vt, bias_t, gsub_row)
        bb, tt = 1, ROW_TILE
    else:
        p = k_past.shape[2]
        assert p % CACHE_TK == 0 and t <= NEW_KEY_PAD and t % 16 == 0
        qb, k, v, c = _inproj(x.reshape(1, b * t, d), w["g_mix"], w["w_in"], w["gq_row"],
                              w["gk_row"], w["seg"], emit_t=False)
        qb, k, v, c = (a.reshape(b, t, cols) for a in (qb, k, v, c))
        k_out = k.reshape(b, t, N_HEADS, 2, HEAD_DIM)
        v_out = v.reshape(b, t, N_HEADS, V_DIM)
        ktn = jnp.pad(jnp.swapaxes(k, 1, 2).astype(BF16), ((0, 0), (0, 0), (0, NEW_KEY_PAD - t)))
        vn = jnp.pad(v.astype(BF16), ((0, 0), (0, NEW_KEY_PAD - t), (0, 0)))
        q_pos = p + jnp.arange(t, dtype=jnp.int32)
        bias = _masked_bias(w["rel_table"], q_pos, jnp.arange(p + NEW_KEY_PAD, dtype=jnp.int32))
        bias = jnp.where(jnp.arange(p + NEW_KEY_PAD) < p + t, bias, NEG_INF)
        rows = 2 * N_HEADS * t
        bias = jnp.broadcast_to(bias[:, None], (N_HEADS, 2, t, p + NEW_KEY_PAD)).reshape(rows, -1)
        bias_c = jnp.swapaxes(bias[:, :p].reshape(rows, p // CACHE_TK, CACHE_TK), 0, 1)
        scalars = jnp.concatenate([jnp.zeros((N_HEADS,), F32), lam.reshape(1)]).astype(F32)
        o = _attn_sample(scalars, qb, k_past, v_past, ktn, vn, bias_c, bias[:, p:], gsub_row)
        bb, tt = SAMPLE_BATCH_TILE, t
        assert b % bb == 0

    nt = t // tt
    first = jnp.pad(c_past, ((0, 0), (CONV_HALO - (CONV_K - 1), 0), (0, 0)))[:, None]
    if nt > 1:
        tails = c.reshape(b, nt, tt, c.shape[2])[:, :-1, tt - CONV_HALO:, :]
        hist = jnp.concatenate([first, tails], axis=1)
    else:
        hist = first
    x2 = _mix(x, o, c, hist, w["w_conv"], w["b_conv"], w["ln_g"], w["ln_b"], w["w_out_o"],
              w["w_out_c"], w["g_cross"], w["w_mq"], w["gmq_row"], mk, mv, w["w_mo"], bb=bb, tt=tt)
    y = _mlp(x2.reshape(b * t, d), w["g_ffn"], w["w_ff1"], w["w_ff2"]).reshape(b, t, d)
    if t >= CONV_K - 1:
        c_hist_tail = c[:, t - (CONV_K - 1):]
    else:
        c_hist_tail = jnp.concatenate([c_past[:, t:], c], axis=1)
    return y, k_out, v_out, c_hist_tail


def kernel(x_prompt, x_sample, cache_k, cache_v, cache_conv, cache_mem_k, cache_mem_v, mem_prompt,
           rel_table, g_mix, w_in, g_q, g_k, lam_vec, g_sub, w_conv, b_conv, ln_g, ln_b, w_out,
           g_cross, g_mem, w_mq, w_mk, w_mv, g_mq, g_mk, w_mo, g_ffn, w_ff1, w_ff2):
    depth = g_mix.shape[0]
    assert depth == 1
    b, t, d = x_prompt.shape
    bs, ts, _ = x_sample.shape
    cols = N_HEADS * HEAD_COLS
    cw = w_conv.shape[2]
    attn_w = N_HEADS * V_DIM
    l = 0
    lam_init = 0.8 - 0.6 * math.exp(-0.3 * l)
    lp = lam_vec[l].astype(F32)
    lam = jnp.exp(jnp.sum(lp[0] * lp[1])) - jnp.exp(jnp.sum(lp[2] * lp[3])) + lam_init

    seg = jnp.kron(jnp.eye(cols // HEAD_DIM, dtype=F32),
                   jnp.full((HEAD_DIM, HEAD_DIM), 1.0 / HEAD_DIM, F32)).astype(BF16)
    n_maps = cols // HEAD_DIM
    w = dict(
        rel_table=rel_table, g_mix=g_mix[l], w_in=w_in[l].astype(BF16), seg=seg,
        gq_row=jnp.tile(g_q[l] * (HEAD_DIM ** -0.5 * LOG2E), n_maps).reshape(1, cols),
        gk_row=jnp.tile(g_k[l], n_maps).reshape(1, cols),
        g_sub=g_sub[l],
        w_conv=jnp.pad(w_conv[l], ((0, CONV_HALO - CONV_K), (0, 0))),
        b_conv=b_conv[l].reshape(1, cw), ln_g=ln_g[l].reshape(1, cw), ln_b=ln_b[l].reshape(1, cw),
        w_out_o=w_out[l][:attn_w].astype(BF16), w_out_c=w_out[l][attn_w:].astype(BF16),
        g_cross=g_cross[l].reshape(1, d), w_mq=w_mq[l].astype(BF16),
        gmq_row=(g_mq[l] * (MEM_HEAD_DIM ** -0.5 * LOG2E)).reshape(1, MEM_HEAD_DIM),
        w_mo=w_mo[l].astype(BF16), g_ffn=g_ffn[l],
        w_ff1=w_ff1[l].astype(BF16), w_ff2=w_ff2[l].astype(BF16),
    )

    n_mem = mem_prompt.shape[1]
    mk_p, mv_p = _memkv(mem_prompt.reshape(b * n_mem, d), g_mem[l], w_mk[l], w_mv[l], g_mk[l])
    mk_p = mk_p.reshape(b, n_mem * MEM_HEADS, MEM_HEAD_DIM)
    mv_p = mv_p.reshape(b, n_mem * MEM_HEADS, MEM_HEAD_DIM)

    zero_conv = jnp.zeros((b, CONV_K - 1, cw), F32)
    yp, kp, vp, cp = _layer(x_prompt, None, None, zero_conv, mk_p, mv_p, w, lam, lam_init)

    p = cache_k.shape[2]
    k_past = jnp.transpose(cache_k[l], (0, 2, 3, 4, 1)).reshape(bs, cols, p)
    v_past = cache_v[l].reshape(bs, p * N_HEADS, V_DIM)
    ys, kn, vn, cn = _layer(x_sample, k_past, v_past, cache_conv[l],
                            cache_mem_k[l].reshape(bs, n_mem * MEM_HEADS, MEM_HEAD_DIM),
                            cache_mem_v[l].reshape(bs, n_mem * MEM_HEADS, MEM_HEAD_DIM),
                            w, lam, lam_init)

    return (yp, ys, kp[None], vp[None], cp[None],
            mk_p.reshape(1, b, n_mem, MEM_HEADS, MEM_HEAD_DIM),
            mv_p.reshape(1, b, n_mem, MEM_HEADS, MEM_HEAD_DIM),
            kn[None], vn[None], cn[None])
```

```python
import functools
import math

import jax
import jax.numpy as jnp
import numpy as np
from jax import lax
from jax.experimental import pallas as pl
from jax.experimental.pallas import tpu as pltpu

F32 = jnp.float32
BF16 = jnp.bfloat16

CHUNK = 64
N_HEADS = 4
HEAD_DIM = 64
V_DIM = 2 * HEAD_DIM
HEAD_COLS = 2 * HEAD_DIM
CONV_K = 31
CONV_HALO = 32
N_BUCKETS = 32
MAX_DISTANCE = 128
MEM_HEADS = 4
MEM_HEAD_DIM = 128
EPS = 1e-6
NEG_INF = -1e30
LOG2E = math.log2(math.e)

ATTN_TQ = 512
ATTN_TK = 256
VT_ROWS = V_DIM + 16
MAX_UNSTABILISED_SCORE = 100.0
CACHE_TK = 4096
NEW_KEY_PAD = 128
ROW_TILE = 512
CONV_ROWS = 64
SAMPLE_BATCH_TILE = 8
VMEM_LIMIT = 56 * 1024 * 1024


def _cparams(sem):
    return pltpu.CompilerParams(dimension_semantics=sem, vmem_limit_bytes=VMEM_LIMIT)


def _rms(x, g):
    ms = jnp.mean(x * x, axis=-1, keepdims=True)
    return x * lax.rsqrt(ms + EPS) * g


def _const_spec(shape):
    return pl.BlockSpec(shape, lambda *_: (0,) * len(shape), pipeline_mode=pl.Buffered(1))


def _memkv_kernel(mem_ref, g_ref, wk_ref, wv_ref, gk_ref, mk_ref, mv_ref):
    m = _rms(mem_ref[...], g_ref[...]).astype(BF16)
    zk = jnp.dot(m, wk_ref[...], preferred_element_type=F32)
    zv = jnp.dot(m, wv_ref[...], preferred_element_type=F32)
    tm = mem_ref.shape[0]
    for h in range(MEM_HEADS):
        sl = slice(h * MEM_HEAD_DIM, (h + 1) * MEM_HEAD_DIM)
        rows = pl.ds(h, tm, stride=MEM_HEADS)
        mk_ref[rows, :] = _rms(zk[:, sl], gk_ref[...])
        mv_ref[rows, :] = zv[:, sl]


def _memkv(mem2d, g_mem, w_mk, w_mv, g_mk):
    n, d = mem2d.shape
    w = w_mk.shape[1]
    tm = min(ROW_TILE, n)
    return pl.pallas_call(
        _memkv_kernel,
        grid=(n // tm,),
        in_specs=[pl.BlockSpec((tm, d), lambda i: (i, 0)), _const_spec((1, d)),
                  _const_spec((d, w)), _const_spec((d, w)), _const_spec((1, MEM_HEAD_DIM))],
        out_specs=[pl.BlockSpec((tm * MEM_HEADS, MEM_HEAD_DIM), lambda i: (i, 0))] * 2,
        out_shape=[jax.ShapeDtypeStruct((n * MEM_HEADS, MEM_HEAD_DIM), F32)] * 2,
        compiler_params=_cparams(("parallel",)),
        name="memkv",
    )(mem2d, g_mem.reshape(1, d), w_mk.astype(BF16), w_mv.astype(BF16),
      g_mk.reshape(1, MEM_HEAD_DIM))


def _inproj_kernel(x_ref, g_ref, w_ref, gq_ref, gk_ref, *refs, qk_cols, emit_t):
    if emit_t:
        qt_ref, kb_ref, vt_ref, kt_ref, v4_ref, c_ref = refs
    else:
        seg_ref, qb_ref, k_ref, v_ref, c_ref = refs

    def mapnorm(z, g):
        ms = jnp.dot((z * z).astype(BF16), seg_ref[...], preferred_element_type=F32)
        return z * lax.rsqrt(ms + EPS) * g

    def mapnorm_t(z, g_t):
        zt = z.T
        z3 = zt.reshape(zt.shape[0] // HEAD_DIM, HEAD_DIM, zt.shape[1])
        ms = jnp.mean(z3 * z3, axis=1, keepdims=True)
        return (z3 * lax.rsqrt(ms + EPS)).reshape(zt.shape) * g_t

    h = _rms(x_ref[0], g_ref[...]).astype(BF16)

    def proj(lo, hi):
        return jnp.dot(h, w_ref[:, lo:hi], preferred_element_type=F32)

    c0 = qk_cols
    zq = proj(0, c0)
    zk = proj(c0, 2 * c0)
    v = proj(2 * c0, 3 * c0)
    a = proj(3 * c0, 4 * c0)
    gate = proj(4 * c0, 5 * c0)
    c_ref[0] = a * jax.nn.sigmoid(gate)
    if emit_t:
        kt = mapnorm_t(zk, gk_ref[...])
        kt_ref[0] = kt
        kb_ref[0] = kt.T.astype(BF16)
        tm = x_ref.shape[1]
        for hh in range(N_HEADS):
            v4_ref[0, pl.ds(hh, tm, stride=N_HEADS), :] = v[:, hh * V_DIM:(hh + 1) * V_DIM]
        qt = mapnorm_t(zq, gq_ref[...]).astype(BF16)
        vt = v.T.astype(BF16)
        for j in range(qt_ref.shape[1]):
            qt_ref[0, j] = qt[:, j * ATTN_TQ:(j + 1) * ATTN_TQ]
        ones_row = (lax.broadcasted_iota(jnp.int32, (VT_ROWS - V_DIM, ATTN_TK), 0) == 0).astype(BF16)
        for j in range(vt_ref.shape[1]):
            for hh in range(N_HEADS):
                vt_ref[0, j, hh, :V_DIM, :] = vt[hh * V_DIM:(hh + 1) * V_DIM,
                                                 j * ATTN_TK:(j + 1) * ATTN_TK]
                vt_ref[0, j, hh, V_DIM:, :] = ones_row
    else:
        k_ref[0] = mapnorm(zk, gk_ref[...])
        v_ref[0] = v
        qb_ref[0] = mapnorm(zq, gq_ref[...]).astype(BF16)


def _inproj(x, g_mix, w_in_bf, gq_row, gk_row, seg, *, emit_t):
    b, t, d = x.shape
    cols = gq_row.shape[1]
    tm = min(ROW_TILE, t)
    nt = t // tm
    row = lambda width: pl.BlockSpec((1, tm, width), lambda bi, ti: (bi, ti, 0))
    f32_out = jax.ShapeDtypeStruct((b, t, cols), F32)
    if emit_t:
        nq, nk = tm // ATTN_TQ, tm // ATTN_TK
        out_shape = [jax.ShapeDtypeStruct((b, t // ATTN_TQ, cols, ATTN_TQ), BF16),
                     jax.ShapeDtypeStruct((b, t, cols), BF16),
                     jax.ShapeDtypeStruct((b, t // ATTN_TK, N_HEADS, VT_ROWS, ATTN_TK), BF16),
                     jax.ShapeDtypeStruct((b, cols, t), F32),
                     jax.ShapeDtypeStruct((b, t * N_HEADS, V_DIM), F32),
                     f32_out]
        out_specs = [pl.BlockSpec((1, nq, cols, ATTN_TQ), lambda bi, ti: (bi, ti, 0, 0)),
                     row(cols),
                     pl.BlockSpec((1, nk, N_HEADS, VT_ROWS, ATTN_TK),
                                  lambda bi, ti: (bi, ti, 0, 0, 0)),
                     pl.BlockSpec((1, cols, tm), lambda bi, ti: (bi, 0, ti)),
                     pl.BlockSpec((1, tm * N_HEADS, V_DIM), lambda bi, ti: (bi, ti, 0)),
                     row(cols)]
        gains = [jnp.broadcast_to(g.reshape(cols, 1), (cols, tm)) for g in (gq_row, gk_row)]
        gain_specs = [_const_spec((cols, tm))] * 2
    else:
        out_shape = [jax.ShapeDtypeStruct((b, t, cols), BF16)] + [f32_out] * 3
        out_specs = [row(cols)] * 4
        gains = [gq_row, gk_row, seg]
        gain_specs = [_const_spec((1, cols)), _const_spec((1, cols)), _const_spec((cols, cols))]
    return pl.pallas_call(
        functools.partial(_inproj_kernel, qk_cols=cols, emit_t=emit_t),
        grid=(b, nt),
        in_specs=[row(d), _const_spec((1, d)), _const_spec(w_in_bf.shape)] + gain_specs,
        out_specs=out_specs,
        out_shape=out_shape,
        compiler_params=_cparams(("parallel", "parallel")),
        name="inproj",
    )(x, g_mix.reshape(1, d), w_in_bf, *gains)


PREV, DIAG0, DIAG1, FAR = 0, 1, 2, None


def _build_qbd(qt_ref, qbd_ref):
    tq = ATTN_TQ
    upper = lax.broadcasted_iota(jnp.int32, (HEAD_COLS, tq), 0) < HEAD_DIM
    for h in range(N_HEADS):
        qh = qt_ref[0, 0, h * HEAD_COLS:(h + 1) * HEAD_COLS, :]
        zero = jnp.zeros_like(qh)
        qbd_ref[h, :, :tq] = jnp.where(upper, qh, zero)
        qbd_ref[h, :, tq:] = jnp.where(upper, zero, qh)


def _attn_finalize(acc_ref, denom, lam, gsub_ref, o_ref):
    tq = ATTN_TQ
    for h in range(N_HEADS):
        acc = acc_ref[h, :V_DIM, :]
        r = 1.0 / denom(h)
        ot = acc[:, :tq] * r[:, :tq] - lam * (acc[:, tq:] * r[:, tq:])
        ms = jnp.mean(ot * ot, axis=0, keepdims=True)
        ot = ot * lax.rsqrt(ms + EPS)
        o_ref[0, :, h * V_DIM:(h + 1) * V_DIM] = (ot.T * gsub_ref[...]).astype(BF16)


def _attn_prompt_kernel(sc_ref, qt_ref, kb_ref, vt_ref, bias_ref, gsub_ref, o_ref,
                        qbd_ref, acc_ref, sa_ref, sb_ref, *stat_refs, bounded):
    i = pl.program_id(1)
    tq, tk = ATTN_TQ, ATTN_TK
    lam = sc_ref[N_HEADS]
    if bounded:
        lsum_ref, = stat_refs
        slot_a, slot_b = (sa_ref, None), (sb_ref, None)
        lsum_ref[...] = jnp.zeros(lsum_ref.shape, F32)
    else:
        m_ref, mca_ref, mcb_ref = stat_refs
        slot_a, slot_b = (sa_ref, mca_ref), (sb_ref, mcb_ref)
        m_ref[...] = jnp.full(m_ref.shape, NEG_INF, F32)

    _build_qbd(qt_ref, qbd_ref)
    acc_ref[...] = jnp.zeros(acc_ref.shape, F32)

    def stage(j, slot, tile):
        s_ref, mc_ref = slot
        row0 = pl.multiple_of(j * tk, tk)
        for h in range(N_HEADS):
            kh = kb_ref[0, pl.ds(row0, tk), h * HEAD_COLS:(h + 1) * HEAD_COLS]
            s = jnp.dot(kh, qbd_ref[h], preferred_element_type=F32)
            if tile is not FAR:
                s = s + bias_ref[h, tile]
            if bounded:
                p = jnp.exp2(s)
                s_ref[h] = p.astype(BF16)
                lsum_ref[h] += jnp.sum(p.reshape(tk // 8, 8, 2 * tq), axis=0)
            else:
                s_ref[h] = s
                mc_ref[h] = jnp.max(s, axis=0, keepdims=True)

    def consume(j, slot, tile):
        s_ref, mc_ref = slot
        for h in range(N_HEADS):
            if bounded:
                acc_ref[h, :V_DIM, :] += jnp.dot(vt_ref[0, j, h, :V_DIM, :], s_ref[h],
                                                 preferred_element_type=F32)
                continue
            m_prev = m_ref[h]
            if tile is FAR:
                shift = sc_ref[h]
                m_new = jnp.maximum(m_prev, mc_ref[h] + shift)
                p = jnp.exp2(s_ref[h] - (m_new - shift))
            else:
                m_new = jnp.maximum(m_prev, mc_ref[h])
                p = jnp.exp2(s_ref[h] - m_new)
            alpha = jnp.exp2(m_prev - m_new)
            pv = jnp.dot(vt_ref[0, j, h], p.astype(BF16), preferred_element_type=F32)
            acc_ref[h] = alpha * acc_ref[h] + pv
            m_ref[h] = m_new

    stage(2 * i + 1, slot_a, DIAG1)
    consume(2 * i + 1, slot_a, DIAG1)
    stage(2 * i, slot_b, DIAG0)

    @pl.when(i == 0)
    def _():
        consume(2 * i, slot_b, DIAG0)

    @pl.when(i >= 1)
    def _():
        consume(2 * i, slot_b, DIAG0)
        stage(2 * i - 1, slot_a, PREV)
        consume(2 * i - 1, slot_a, PREV)
        if bounded:
            for h in range(N_HEADS):
                acc_ref[h] = acc_ref[h] * sc_ref[N_HEADS + 1 + h]
                lsum_ref[h] = lsum_ref[h] * sc_ref[N_HEADS + 1 + h]
        stage(0, slot_b, FAR)

    def pair_body(jj, carry):
        f = 2 * jj
        consume(f, slot_b, FAR)
        stage(f + 1, slot_a, FAR)
        consume(f + 1, slot_a, FAR)
        stage(f + 2, slot_b, FAR)
        return carry

    lax.fori_loop(0, jnp.maximum(i - 1, 0), pair_body, 0)

    @pl.when(i >= 1)
    def _():
        consume(2 * i - 2, slot_b, FAR)

    if bounded:
        denom = lambda h: jnp.sum(lsum_ref[h], axis=0, keepdims=True)
    else:
        denom = lambda h: acc_ref[h, V_DIM:V_DIM + 1, :]
    _attn_finalize(acc_ref, denom, lam, gsub_ref, o_ref)


def _attn_prompt(scalars, score_bound, qt, kb, vt, bias_t, gsub_row):
    b, nq, cols, tq = qt.shape
    t = kb.shape[1]
    nk = vt.shape[1]
    qbd = pltpu.VMEM((N_HEADS, HEAD_COLS, 2 * tq), BF16)
    acc = pltpu.VMEM((N_HEADS, VT_ROWS, 2 * tq), F32)
    stat = pltpu.VMEM((N_HEADS, 1, 2 * tq), F32)
    scores = pltpu.VMEM((N_HEADS, ATTN_TK, 2 * tq), F32)
    probs = pltpu.VMEM((N_HEADS, ATTN_TK, 2 * tq), BF16)
    colsum = pltpu.VMEM((N_HEADS, 8, 2 * tq), F32)

    def call(bounded, scratch, name):
        return pl.pallas_call(
            functools.partial(_attn_prompt_kernel, bounded=bounded),
            grid=(b, nq),
            in_specs=[pl.BlockSpec(memory_space=pltpu.SMEM),
                      pl.BlockSpec((1, 1, cols, tq), lambda bi, i: (bi, i, 0, 0)),
                      pl.BlockSpec((1, t, cols), lambda bi, i: (bi, 0, 0)),
                      pl.BlockSpec((1, nk, N_HEADS, VT_ROWS, ATTN_TK),
                                   lambda bi, i: (bi, 0, 0, 0, 0)),
                      _const_spec(bias_t.shape), _const_spec((1, V_DIM))],
            out_specs=pl.BlockSpec((1, tq, cols), lambda bi, i: (bi, i, 0)),
            out_shape=jax.ShapeDtypeStruct((b, t, cols), BF16),
            scratch_shapes=scratch,
            compiler_params=_cparams(("parallel", "arbitrary")),
            name=name,
        )(scalars, qt, kb, vt, bias_t, gsub_row)

    return lax.cond(
        score_bound <= MAX_UNSTABILISED_SCORE,
        lambda: call(True, [qbd, acc, probs, probs, colsum], "attn_prompt_bounded"),
        lambda: call(False, [qbd, acc, scores, scores, stat, stat, stat], "attn_prompt"))


def _attn_sample_kernel(sc_ref, q_ref, kt_ref, vc_ref, ktn_ref, vn_ref, bc_ref, bn_ref, gsub_ref,
                        o_ref, qbd_ref, m_ref, l_ref, acc_ref):
    j = pl.program_id(1)
    tq, cols = q_ref.shape[1], q_ref.shape[2]
    lam = sc_ref[N_HEADS]

    @pl.when(j == 0)
    def _():
        q = q_ref[0].astype(F32)
        col = lax.broadcasted_iota(jnp.int32, (tq, cols), 1)
        for hm in range(cols // HEAD_DIM):
            mine = (col >= hm * HEAD_DIM) & (col < (hm + 1) * HEAD_DIM)
            qbd_ref[hm * tq:(hm + 1) * tq, :] = jnp.where(mine, q, 0.0)
        m_ref[...] = jnp.full(m_ref.shape, NEG_INF, F32)
        l_ref[...] = jnp.zeros(l_ref.shape, F32)
        acc_ref[...] = jnp.zeros(acc_ref.shape, F32)

    def update(kt, v_of_head, bias):
        s = jnp.dot(qbd_ref[...].astype(BF16), kt, preferred_element_type=F32) + bias
        m_prev = m_ref[...]
        m_new = jnp.maximum(m_prev, jnp.max(s, axis=-1, keepdims=True))
        alpha = jnp.exp2(m_prev - m_new)
        p = jnp.exp2(s - m_new)
        l_ref[...] = alpha * l_ref[...] + jnp.sum(p, axis=-1, keepdims=True)
        pb = p.astype(BF16)
        for h in range(N_HEADS):
            rows = slice(h * 2 * tq, (h + 1) * 2 * tq)
            acc_ref[rows, :] = alpha[rows] * acc_ref[rows, :] + jnp.dot(
                pb[rows], v_of_head(h), preferred_element_type=F32)
        m_ref[...] = m_new

    tk = kt_ref.shape[2]
    update(kt_ref[0].astype(BF16),
           lambda h: vc_ref[0, pl.ds(h, tk, stride=N_HEADS), :].astype(BF16), bc_ref[j])

    @pl.when(j == pl.num_programs(1) - 1)
    def _():
        update(ktn_ref[0], lambda h: vn_ref[0, :, h * V_DIM:(h + 1) * V_DIM], bn_ref[...])
        for h in range(N_HEADS):
            r0 = h * 2 * tq
            inv0 = 1.0 / l_ref[r0:r0 + tq, :]
            inv1 = 1.0 / l_ref[r0 + tq:r0 + 2 * tq, :]
            o = acc_ref[r0:r0 + tq, :] * inv0 - lam * (acc_ref[r0 + tq:r0 + 2 * tq, :] * inv1)
            o_ref[0, :, h * V_DIM:(h + 1) * V_DIM] = _rms(o, gsub_ref[...]).astype(BF16)


def _attn_sample(scalars, q, kt, v4, ktn, vn, bias_c, bias_n, gsub_row):
    b, tq, cols = q.shape
    p = kt.shape[2]
    nkc = p // CACHE_TK
    rows = 2 * N_HEADS * tq
    return pl.pallas_call(
        _attn_sample_kernel,
        grid=(b, nkc),
        in_specs=[pl.BlockSpec(memory_space=pltpu.SMEM),
                  pl.BlockSpec((1, tq, cols), lambda bi, j: (bi, 0, 0)),
                  pl.BlockSpec((1, cols, CACHE_TK), lambda bi, j: (bi, 0, j)),
                  pl.BlockSpec((1, CACHE_TK * N_HEADS, V_DIM), lambda bi, j: (bi, j, 0)),
                  pl.BlockSpec((1, cols, NEW_KEY_PAD), lambda bi, j: (bi, 0, 0)),
                  pl.BlockSpec((1, NEW_KEY_PAD, cols), lambda bi, j: (bi, 0, 0)),
                  _const_spec(bias_c.shape), _const_spec(bias_n.shape), _const_spec((1, V_DIM))],
        out_specs=pl.BlockSpec((1, tq, cols), lambda bi, j: (bi, 0, 0)),
        out_shape=jax.ShapeDtypeStruct((b, tq, cols), BF16),
        scratch_shapes=[pltpu.VMEM((rows, cols), F32),
                        pltpu.VMEM((rows, 1), F32),
                        pltpu.VMEM((rows, 1), F32),
                        pltpu.VMEM((rows, V_DIM), F32)],
        compiler_params=_cparams(("parallel", "arbitrary")),
        name="attn_sample",
    )(scalars, q, kt, v4, ktn, vn, bias_c, bias_n, gsub_row)


def _mix_kernel(x_ref, o_ref, c_ref, hist_ref, wconv_ref, bconv_ref, lng_ref, lnb_ref,
                woo_ref, woc_ref, gcross_ref, wmq_ref, gmq_ref, mk_ref, mv_ref, wmo_ref,
                out_ref, ext_ref, xs_ref, cv_ref, ca_ref):
    bb, tt, d = x_ref.shape
    rows = bb * tt
    n_mem = mk_ref.shape[1] // MEM_HEADS
    rc = min(CONV_ROWS, tt)
    lead = CONV_HALO - (CONV_K - 1)

    for b in range(bb):
        ext_ref[b, :CONV_HALO, :] = hist_ref[b, 0]
        ext_ref[b, CONV_HALO:, :] = c_ref[b]
    span = xs_ref.shape[2]
    for b in range(bb):
        for r in range(1, 8):
            xs_ref[r - 1, b] = ext_ref[b, r:r + span, :]
    for b in range(bb):
        for r0 in range(0, tt, rc):
            acc = jnp.zeros((rc, c_ref.shape[2]), F32) + bconv_ref[...]
            for k in range(CONV_K):
                a, r = divmod(k + lead, 8)
                lo = r0 + 8 * a
                src = ext_ref[b, lo:lo + rc, :] if r == 0 else xs_ref[r - 1, b, lo:lo + rc, :]
                acc = acc + wconv_ref[k:k + 1, :] * src
            mu = jnp.mean(acc, axis=-1, keepdims=True)
            xc = acc - mu
            var = jnp.mean(xc * xc, axis=-1, keepdims=True)
            y = xc * lax.rsqrt(var + EPS) * lng_ref[...] + lnb_ref[...]
            cv_ref[b * tt + r0:b * tt + r0 + rc, :] = (y * jax.nn.sigmoid(y)).astype(BF16)

    x = x_ref[...].reshape(rows, d)
    o = o_ref[...].reshape(rows, o_ref.shape[2])
    x1 = (x + jnp.dot(o, woo_ref[...], preferred_element_type=F32)
          + jnp.dot(cv_ref[...], woc_ref[...], preferred_element_type=F32))

    hc = _rms(x1, gcross_ref[...]).astype(BF16)
    qm = jnp.dot(hc, wmq_ref[...], preferred_element_type=F32)
    for h in range(MEM_HEADS):
        cols = slice(h * MEM_HEAD_DIM, (h + 1) * MEM_HEAD_DIM)
        qn = _rms(qm[:, cols], gmq_ref[...]).astype(BF16)
        for b in range(bb):
            mem_rows = pl.ds(h, n_mem, stride=MEM_HEADS)
            mk = mk_ref[b, mem_rows, :].astype(BF16)
            mv = mv_ref[b, mem_rows, :].astype(BF16)
            s = lax.dot_general(qn[b * tt:(b + 1) * tt], mk, (((1,), (1,)), ((), ())),
                                preferred_element_type=F32)
            p = jnp.exp2(s - jnp.max(s, axis=-1, keepdims=True))
            l = jnp.sum(p, axis=-1, keepdims=True)
            oh = jnp.dot(p.astype(BF16), mv, preferred_element_type=F32) / l
            ca_ref[b * tt:(b + 1) * tt, cols] = oh.astype(BF16)
    x2 = x1 + jnp.dot(ca_ref[...], wmo_ref[...], preferred_element_type=F32)
    out_ref[...] = x2.reshape(bb, tt, d)


def _mix(x, o, c, hist, wconv, bconv, lng, lnb, woo, woc, gcross, wmq, gmq_row, mk, mv, wmo,
         *, bb, tt):
    b, t, d = x.shape
    cw = c.shape[2]
    mem_rows, mhd = mk.shape[1], mk.shape[2]
    mw = wmq.shape[1]
    rows = bb * tt
    tile = lambda width: pl.BlockSpec((bb, tt, width), lambda bi, ti: (bi, ti, 0))
    return pl.pallas_call(
        _mix_kernel,
        grid=(b // bb, t // tt),
        in_specs=[tile(d), tile(cw), tile(cw),
                  pl.BlockSpec((bb, 1, CONV_HALO, cw), lambda bi, ti: (bi, ti, 0, 0)),
                  _const_spec(wconv.shape), _const_spec((1, cw)), _const_spec((1, cw)),
                  _const_spec((1, cw)), _const_spec(woo.shape), _const_spec(woc.shape),
                  _const_spec((1, d)), _const_spec(wmq.shape), _const_spec((1, MEM_HEAD_DIM)),
                  pl.BlockSpec((bb, mem_rows, mhd), lambda bi, ti: (bi, 0, 0)),
                  pl.BlockSpec((bb, mem_rows, mhd), lambda bi, ti: (bi, 0, 0)),
                  _const_spec(wmo.shape)],
        out_specs=tile(d),
        out_shape=jax.ShapeDtypeStruct((b, t, d), F32),
        scratch_shapes=[pltpu.VMEM((bb, CONV_HALO + tt, cw), F32),
                        pltpu.VMEM((7, bb, CONV_HALO - 8 + tt, cw), F32),
                        pltpu.VMEM((rows, cw), BF16),
                        pltpu.VMEM((rows, mw), BF16)],
        compiler_params=_cparams(("parallel", "parallel")),
        name="mix",
    )(x, o, c, hist, wconv, bconv, lng, lnb, woo, woc, gcross, wmq, gmq_row, mk, mv, wmo)


def _mlp_kernel(x_ref, g_ref, w1_ref, w2_ref, out_ref):
    x = x_ref[...]
    hf = _rms(x, g_ref[...]).astype(BF16)
    u = jnp.maximum(jnp.dot(hf, w1_ref[...], preferred_element_type=F32), 0.0)
    out_ref[...] = x + jnp.dot((u * u).astype(BF16), w2_ref[...], preferred_element_type=F32)


def _mlp(x2d, g_ffn, w1, w2):
    n, d = x2d.shape
    tm = min(ROW_TILE, n)
    return pl.pallas_call(
        _mlp_kernel,
        grid=(n // tm,),
        in_specs=[pl.BlockSpec((tm, d), lambda i: (i, 0)), _const_spec((1, d)),
                  _const_spec(w1.shape), _const_spec(w2.shape)],
        out_specs=pl.BlockSpec((tm, d), lambda i: (i, 0)),
        out_shape=jax.ShapeDtypeStruct((n, d), F32),
        compiler_params=_cparams(("parallel",)),
        name="mlp",
    )(x2d, g_ffn.reshape(1, d), w1, w2)


def _rel_bucket(rel):
    half = N_BUCKETS // 2
    max_exact = half // 2
    ret = jnp.where(rel > 0, half, 0)
    n = jnp.abs(rel)
    nf = jnp.maximum(n, 1).astype(jnp.float32)
    large = max_exact + (jnp.log(nf / max_exact) / math.log(MAX_DISTANCE / max_exact)
                         * (half - max_exact)).astype(jnp.int32)
    large = jnp.minimum(large, half - 1)
    return ret + jnp.where(n < max_exact, n, large)


def _masked_bias(rel_table, q_pos, k_pos):
    bucket = _rel_bucket(k_pos[None, :] - q_pos[:, None])[None]
    table = rel_table.astype(F32) * LOG2E
    bias = jnp.zeros((rel_table.shape[1],) + bucket.shape[1:], F32)
    for bkt in range(N_BUCKETS):
        bias = jnp.where(bucket == bkt, table[bkt][:, None, None], bias)
    mask = (k_pos[None, :] // CHUNK) <= (q_pos[:, None] // CHUNK)
    return jnp.where(mask[None], bias, NEG_INF)


def _far_bucket_is_saturated(min_distance):
    half = N_BUCKETS // 2
    max_exact = half // 2
    large = max_exact + int(np.log(min_distance / max_exact) / math.log(MAX_DISTANCE / max_exact)
                            * (half - max_exact) * (1 - 1e-6))
    return large >= half - 1


def _layer(x, k_past, v_past, c_past, mk, mv, w, lam, lam_init):
    b, t, d = x.shape
    prompt = k_past is None
    cols = N_HEADS * HEAD_COLS
    gsub_row = (w["g_sub"] * (1.0 - lam_init)).reshape(1, V_DIM)

    if prompt:
        assert t % ROW_TILE == 0 and ROW_TILE % ATTN_TQ == 0 and ATTN_TQ == 2 * ATTN_TK
        assert ATTN_TK % CHUNK == 0 and _far_bucket_is_saturated(ATTN_TK + 1)
        qt, kb, vt, kt, v4, c = _inproj(x, w["g_mix"], w["w_in"], w["gq_row"], w["gk_row"],
                                        w["seg"], emit_t=True)
        k_out = jnp.transpose(kt.reshape(b, N_HEADS, 2, HEAD_DIM, t), (0, 4, 1, 2, 3))
        v_out = v4.reshape(b, t, N_HEADS, V_DIM)
        q_pos = ATTN_TQ + jnp.arange(ATTN_TQ, dtype=jnp.int32)
        k_pos = ATTN_TK + jnp.arange(3 * ATTN_TK, dtype=jnp.int32)
        near = _masked_bias(w["rel_table"], q_pos, k_pos)
        near = jnp.stack([near[:, :, n * ATTN_TK:(n + 1) * ATTN_TK] for n in range(3)], axis=1)
        near_t = jnp.swapaxes(near, 2, 3)
        bias_t = jnp.concatenate([near_t, near_t], axis=3)
        far = w["rel_table"][_rel_bucket(jnp.int32(-(ATTN_TK + 1)))].astype(F32) * LOG2E
        scalars = jnp.concatenate([far, lam.reshape(1), jnp.exp2(-far)]).astype(F32)
        score_bound = (HEAD_DIM * jnp.max(jnp.abs(w["gq_row"])) * jnp.max(jnp.abs(w["gk_row"]))
                       + LOG2E * jnp.max(jnp.abs(w["rel_table"])))
        o = _attn_prompt(scalars, score_bound, qt, kb, vt, bias_t, gsub_row)
        bb, tt = 1, ROW_TILE
    else:
        p = k_past.shape[2]
        assert p % CACHE_TK == 0 and t <= NEW_KEY_PAD and t % 16 == 0
        qb, k, v, c = _inproj(x.reshape(1, b * t, d), w["g_mix"], w["w_in"], w["gq_row"],
                              w["gk_row"], w["seg"], emit_t=False)
        qb, k, v, c = (a.reshape(b, t, cols) for a in (qb, k, v, c))
        k_out = k.reshape(b, t, N_HEADS, 2, HEAD_DIM)
        v_out = v.reshape(b, t, N_HEADS, V_DIM)
        ktn = jnp.pad(jnp.swapaxes(k, 1, 2).astype(BF16), ((0, 0), (0, 0), (0, NEW_KEY_PAD - t)))
        vn = jnp.pad(v.astype(BF16), ((0, 0), (0, NEW_KEY_PAD - t), (0, 0)))
        q_pos = p + jnp.arange(t, dtype=jnp.int32)
        bias = _masked_bias(w["rel_table"], q_pos, jnp.arange(p + NEW_KEY_PAD, dtype=jnp.int32))
        bias = jnp.where(jnp.arange(p + NEW_KEY_PAD) < p + t, bias, NEG_INF)
        rows = 2 * N_HEADS * t
        bias = jnp.broadcast_to(bias[:, None], (N_HEADS, 2, t, p + NEW_KEY_PAD)).reshape(rows, -1)
        bias_c = jnp.swapaxes(bias[:, :p].reshape(rows, p // CACHE_TK, CACHE_TK), 0, 1)
        scalars = jnp.concatenate([jnp.zeros((N_HEADS,), F32), lam.reshape(1)]).astype(F32)
        o = _attn_sample(scalars, qb, k_past, v_past, ktn, vn, bias_c, bias[:, p:], gsub_row)
        bb, tt = SAMPLE_BATCH_TILE, t
        assert b % bb == 0

    nt = t // tt
    first = jnp.pad(c_past, ((0, 0), (CONV_HALO - (CONV_K - 1), 0), (0, 0)))[:, None]
    if nt > 1:
        tails = c.reshape(b, nt, tt, c.shape[2])[:, :-1, tt - CONV_HALO:, :]
        hist = jnp.concatenate([first, tails], axis=1)
    else:
        hist = first
    x2 = _mix(x, o, c, hist, w["w_conv"], w["b_conv"], w["ln_g"], w["ln_b"], w["w_out_o"],
              w["w_out_c"], w["g_cross"], w["w_mq"], w["gmq_row"], mk, mv, w["w_mo"], bb=bb, tt=tt)
    y = _mlp(x2.reshape(b * t, d), w["g_ffn"], w["w_ff1"], w["w_ff2"]).reshape(b, t, d)
    if t >= CONV_K - 1:
        c_hist_tail = c[:, t - (CONV_K - 1):]
    else:
        c_hist_tail = jnp.concatenate([c_past[:, t:], c], axis=1)
    return y, k_out, v_out, c_hist_tail


def kernel(x_prompt, x_sample, cache_k, cache_v, cache_conv, cache_mem_k, cache_mem_v, mem_prompt,
           rel_table, g_mix, w_in, g_q, g_k, lam_vec, g_sub, w_conv, b_conv, ln_g, ln_b, w_out,
           g_cross, g_mem, w_mq, w_mk, w_mv, g_mq, g_mk, w_mo, g_ffn, w_ff1, w_ff2):
    depth = g_mix.shape[0]
    assert depth == 1
    b, t, d = x_prompt.shape
    bs, ts, _ = x_sample.shape
    cols = N_HEADS * HEAD_COLS
    cw = w_conv.shape[2]
    attn_w = N_HEADS * V_DIM
    l = 0
    lam_init = 0.8 - 0.6 * math.exp(-0.3 * l)
    lp = lam_vec[l].astype(F32)
    lam = jnp.exp(jnp.sum(lp[0] * lp[1])) - jnp.exp(jnp.sum(lp[2] * lp[3])) + lam_init

    seg = jnp.kron(jnp.eye(cols // HEAD_DIM, dtype=F32),
                   jnp.full((HEAD_DIM, HEAD_DIM), 1.0 / HEAD_DIM, F32)).astype(BF16)
    n_maps = cols // HEAD_DIM
    w = dict(
        rel_table=rel_table, g_mix=g_mix[l], w_in=w_in[l].astype(BF16), seg=seg,
        gq_row=jnp.tile(g_q[l] * (HEAD_DIM ** -0.5 * LOG2E), n_maps).reshape(1, cols),
        gk_row=jnp.tile(g_k[l], n_maps).reshape(1, cols),
        g_sub=g_sub[l],
        w_conv=jnp.pad(w_conv[l], ((0, CONV_HALO - CONV_K), (0, 0))),
        b_conv=b_conv[l].reshape(1, cw), ln_g=ln_g[l].reshape(1, cw), ln_b=ln_b[l].reshape(1, cw),
        w_out_o=w_out[l][:attn_w].astype(BF16), w_out_c=w_out[l][attn_w:].astype(BF16),
        g_cross=g_cross[l].reshape(1, d), w_mq=w_mq[l].astype(BF16),
        gmq_row=(g_mq[l] * (MEM_HEAD_DIM ** -0.5 * LOG2E)).reshape(1, MEM_HEAD_DIM),
        w_mo=w_mo[l].astype(BF16), g_ffn=g_ffn[l],
        w_ff1=w_ff1[l].astype(BF16), w_ff2=w_ff2[l].astype(BF16),
    )

    n_mem = mem_prompt.shape[1]
    mk_p, mv_p = _memkv(mem_prompt.reshape(b * n_mem, d), g_mem[l], w_mk[l], w_mv[l], g_mk[l])
    mk_p = mk_p.reshape(b, n_mem * MEM_HEADS, MEM_HEAD_DIM)
    mv_p = mv_p.reshape(b, n_mem * MEM_HEADS, MEM_HEAD_DIM)

    zero_conv = jnp.zeros((b, CONV_K - 1, cw), F32)
    yp, kp, vp, cp = _layer(x_prompt, None, None, zero_conv, mk_p, mv_p, w, lam, lam_init)

    p = cache_k.shape[2]
    k_past = jnp.transpose(cache_k[l], (0, 2, 3, 4, 1)).reshape(bs, cols, p)
    v_past = cache_v[l].reshape(bs, p * N_HEADS, V_DIM)
    ys, kn, vn, cn = _layer(x_sample, k_past, v_past, cache_conv[l],
                            cache_mem_k[l].reshape(bs, n_mem * MEM_HEADS, MEM_HEAD_DIM),
                            cache_mem_v[l].reshape(bs, n_mem * MEM_HEADS, MEM_HEAD_DIM),
                            w, lam, lam_init)

    return (yp, ys, kp[None], vp[None], cp[None],
            mk_p.reshape(1, b, n_mem, MEM_HEADS, MEM_HEAD_DIM),
            mv_p.reshape(1, b, n_mem, MEM_HEADS, MEM_HEAD_DIM),
            kn[None], vn[None], cn[None])
```

```python
import functools
import math

import jax
import jax.numpy as jnp
import numpy as np
from jax import lax
from jax.experimental import pallas as pl
from jax.experimental.pallas import tpu as pltpu

F32 = jnp.float32
BF16 = jnp.bfloat16

CHUNK = 64
N_HEADS = 4
HEAD_DIM = 64
V_DIM = 2 * HEAD_DIM
HEAD_COLS = 2 * HEAD_DIM
CONV_K = 31
CONV_HALO = 32
N_BUCKETS = 32
MAX_DISTANCE = 128
MEM_HEADS = 4
MEM_HEAD_DIM = 128
EPS = 1e-6
NEG_INF = -1e30
LOG2E = math.log2(math.e)

ATTN_TQ = 512
ATTN_TK = 256
VT_ROWS = V_DIM + 16
MAX_UNSTABILISED_SCORE = 100.0
CACHE_TK = 4096
NEW_KEY_PAD = 128
ROW_TILE = 512
CONV_ROWS = 64
SAMPLE_BATCH_TILE = 8
VMEM_LIMIT = 56 * 1024 * 1024


def _cparams(sem):
    return pltpu.CompilerParams(dimension_semantics=sem, vmem_limit_bytes=VMEM_LIMIT)


def _rms(x, g):
    ms = jnp.mean(x * x, axis=-1, keepdims=True)
    return x * lax.rsqrt(ms + EPS) * g


def _const_spec(shape):
    return pl.BlockSpec(shape, lambda *_: (0,) * len(shape), pipeline_mode=pl.Buffered(1))


def _memkv_kernel(mem_ref, g_ref, wk_ref, wv_ref, gk_ref, mk_ref, mv_ref):
    m = _rms(mem_ref[...], g_ref[...]).astype(BF16)
    zk = jnp.dot(m, wk_ref[...], preferred_element_type=F32)
    zv = jnp.dot(m, wv_ref[...], preferred_element_type=F32)
    tm = mem_ref.shape[0]
    for h in range(MEM_HEADS):
        sl = slice(h * MEM_HEAD_DIM, (h + 1) * MEM_HEAD_DIM)
        rows = pl.ds(h, tm, stride=MEM_HEADS)
        mk_ref[rows, :] = _rms(zk[:, sl], gk_ref[...])
        mv_ref[rows, :] = zv[:, sl]


def _memkv(mem2d, g_mem, w_mk, w_mv, g_mk):
    n, d = mem2d.shape
    w = w_mk.shape[1]
    tm = min(ROW_TILE, n)
    return pl.pallas_call(
        _memkv_kernel,
        grid=(n // tm,),
        in_specs=[pl.BlockSpec((tm, d), lambda i: (i, 0)), _const_spec((1, d)),
                  _const_spec((d, w)), _const_spec((d, w)), _const_spec((1, MEM_HEAD_DIM))],
        out_specs=[pl.BlockSpec((tm * MEM_HEADS, MEM_HEAD_DIM), lambda i: (i, 0))] * 2,
        out_shape=[jax.ShapeDtypeStruct((n * MEM_HEADS, MEM_HEAD_DIM), F32)] * 2,
        compiler_params=_cparams(("parallel",)),
        name="memkv",
    )(mem2d, g_mem.reshape(1, d), w_mk.astype(BF16), w_mv.astype(BF16),
      g_mk.reshape(1, MEM_HEAD_DIM))


def _inproj_kernel(x_ref, g_ref, w_ref, gq_ref, gk_ref, *refs, qk_cols, emit_t):
    if emit_t:
        qt_ref, kb_ref, vt_ref, kt_ref, v4_ref, c_ref = refs
    else:
        seg_ref, qb_ref, k_ref, v_ref, c_ref = refs

    def mapnorm(z, g):
        ms = jnp.dot((z * z).astype(BF16), seg_ref[...], preferred_element_type=F32)
        return z * lax.rsqrt(ms + EPS) * g

    def mapnorm_t(z, g_t):
        zt = z.T
        z3 = zt.reshape(zt.shape[0] // HEAD_DIM, HEAD_DIM, zt.shape[1])
        ms = jnp.mean(z3 * z3, axis=1, keepdims=True)
        return (z3 * lax.rsqrt(ms + EPS)).reshape(zt.shape) * g_t

    h = _rms(x_ref[0], g_ref[...]).astype(BF16)

    def proj(lo, hi):
        return jnp.dot(h, w_ref[:, lo:hi], preferred_element_type=F32)

    c0 = qk_cols
    zq = proj(0, c0)
    zk = proj(c0, 2 * c0)
    v = proj(2 * c0, 3 * c0)
    a = proj(3 * c0, 4 * c0)
    gate = proj(4 * c0, 5 * c0)
    c_ref[0] = a * jax.nn.sigmoid(gate)
    if emit_t:
        kt = mapnorm_t(zk, gk_ref[...])
        kt_ref[0] = kt
        kb_ref[0] = kt.T.astype(BF16)
        tm = x_ref.shape[1]
        for hh in range(N_HEADS):
            v4_ref[0, pl.ds(hh, tm, stride=N_HEADS), :] = v[:, hh * V_DIM:(hh + 1) * V_DIM]
        qt = mapnorm_t(zq, gq_ref[...]).astype(BF16)
        vt = v.T.astype(BF16)
        for j in range(qt_ref.shape[1]):
            qt_ref[0, j] = qt[:, j * ATTN_TQ:(j + 1) * ATTN_TQ]
        ones_row = (lax.broadcasted_iota(jnp.int32, (VT_ROWS - V_DIM, ATTN_TK), 0) == 0).astype(BF16)
        for j in range(vt_ref.shape[1]):
            for hh in range(N_HEADS):
                vt_ref[0, j, hh, :V_DIM, :] = vt[hh * V_DIM:(hh + 1) * V_DIM,
                                                 j * ATTN_TK:(j + 1) * ATTN_TK]
                vt_ref[0, j, hh, V_DIM:, :] = ones_row
    else:
        k_ref[0] = mapnorm(zk, gk_ref[...])
        v_ref[0] = v
        qb_ref[0] = mapnorm(zq, gq_ref[...]).astype(BF16)


def _inproj(x, g_mix, w_in_bf, gq_row, gk_row, seg, *, emit_t):
    b, t, d = x.shape
    cols = gq_row.shape[1]
    tm = min(ROW_TILE, t)
    nt = t // tm
    row = lambda width: pl.BlockSpec((1, tm, width), lambda bi, ti: (bi, ti, 0))
    f32_out = jax.ShapeDtypeStruct((b, t, cols), F32)
    if emit_t:
        nq, nk = tm // ATTN_TQ, tm // ATTN_TK
        out_shape = [jax.ShapeDtypeStruct((b, t // ATTN_TQ, cols, ATTN_TQ), BF16),
                     jax.ShapeDtypeStruct((b, t, cols), BF16),
                     jax.ShapeDtypeStruct((b, t // ATTN_TK, N_HEADS, VT_ROWS, ATTN_TK), BF16),
                     jax.ShapeDtypeStruct((b, cols, t), F32),
                     jax.ShapeDtypeStruct((b, t * N_HEADS, V_DIM), F32),
                     f32_out]
        out_specs = [pl.BlockSpec((1, nq, cols, ATTN_TQ), lambda bi, ti: (bi, ti, 0, 0)),
                     row(cols),
                     pl.BlockSpec((1, nk, N_HEADS, VT_ROWS, ATTN_TK),
                                  lambda bi, ti: (bi, ti, 0, 0, 0)),
                     pl.BlockSpec((1, cols, tm), lambda bi, ti: (bi, 0, ti)),
                     pl.BlockSpec((1, tm * N_HEADS, V_DIM), lambda bi, ti: (bi, ti, 0)),
                     row(cols)]
        gains = [jnp.broadcast_to(g.reshape(cols, 1), (cols, tm)) for g in (gq_row, gk_row)]
        gain_specs = [_const_spec((cols, tm))] * 2
    else:
        out_shape = [jax.ShapeDtypeStruct((b, t, cols), BF16)] + [f32_out] * 3
        out_specs = [row(cols)] * 4
        gains = [gq_row, gk_row, seg]
        gain_specs = [_const_spec((1, cols)), _const_spec((1, cols)), _const_spec((cols, cols))]
    return pl.pallas_call(
        functools.partial(_inproj_kernel, qk_cols=cols, emit_t=emit_t),
        grid=(b, nt),
        in_specs=[row(d), _const_spec((1, d)), _const_spec(w_in_bf.shape)] + gain_specs,
        out_specs=out_specs,
        out_shape=out_shape,
        compiler_params=_cparams(("parallel", "parallel")),
        name="inproj",
    )(x, g_mix.reshape(1, d), w_in_bf, *gains)


PREV, DIAG0, DIAG1, FAR = 0, 1, 2, None


def _build_qbd(qt_ref, qbd_ref):
    tq = ATTN_TQ
    upper = lax.broadcasted_iota(jnp.int32, (HEAD_COLS, tq), 0) < HEAD_DIM
    for h in range(N_HEADS):
        qh = qt_ref[0, 0, h * HEAD_COLS:(h + 1) * HEAD_COLS, :]
        zero = jnp.zeros_like(qh)
        qbd_ref[h, :, :tq] = jnp.where(upper, qh, zero)
        qbd_ref[h, :, tq:] = jnp.where(upper, zero, qh)


def _attn_finalize(acc_ref, denom, lam, gsub_ref, o_ref):
    tq = ATTN_TQ
    for h in range(N_HEADS):
        acc = acc_ref[h, :V_DIM, :]
        r = 1.0 / denom(h)
        ot = acc[:, :tq] * r[:, :tq] - lam * (acc[:, tq:] * r[:, tq:])
        ms = jnp.mean(ot * ot, axis=0, keepdims=True)
        ot = ot * lax.rsqrt(ms + EPS)
        o_ref[0, :, h * V_DIM:(h + 1) * V_DIM] = (ot.T * gsub_ref[...]).astype(BF16)


def _attn_prompt_kernel(sc_ref, qt_ref, kb_ref, vt_ref, bias_ref, gsub_ref, o_ref,
                        qbd_ref, acc_ref, sa_ref, sb_ref, *stat_refs, bounded):
    i = pl.program_id(1)
    tq, tk = ATTN_TQ, ATTN_TK
    lam = sc_ref[N_HEADS]
    if bounded:
        lsum_ref, = stat_refs
        slot_a, slot_b = (sa_ref, None), (sb_ref, None)
        lsum_ref[...] = jnp.zeros(lsum_ref.shape, F32)
    else:
        m_ref, mca_ref, mcb_ref = stat_refs
        slot_a, slot_b = (sa_ref, mca_ref), (sb_ref, mcb_ref)
        m_ref[...] = jnp.full(m_ref.shape, NEG_INF, F32)

    _build_qbd(qt_ref, qbd_ref)
    acc_ref[...] = jnp.zeros(acc_ref.shape, F32)

    def stage(j, slot, tile):
        s_ref, mc_ref = slot
        row0 = pl.multiple_of(j * tk, tk)
        for h in range(N_HEADS):
            kh = kb_ref[0, pl.ds(row0, tk), h * HEAD_COLS:(h + 1) * HEAD_COLS]
            s = jnp.dot(kh, qbd_ref[h], preferred_element_type=F32)
            if tile is not FAR:
                s = s + bias_ref[h, tile]
            if bounded:
                p = jnp.exp2(s)
                s_ref[h] = p.astype(BF16)
                lsum_ref[h] += jnp.sum(p.reshape(tk // 8, 8, 2 * tq), axis=0)
            else:
                s_ref[h] = s
                mc_ref[h] = jnp.max(s, axis=0, keepdims=True)

    def consume(j, slot, tile):
        s_ref, mc_ref = slot
        for h in range(N_HEADS):
            if bounded:
                acc_ref[h, :V_DIM, :] += jnp.dot(vt_ref[0, j, h, :V_DIM, :], s_ref[h],
                                                 preferred_element_type=F32)
                continue
            m_prev = m_ref[h]
            if tile is FAR:
                shift = sc_ref[h]
                m_new = jnp.maximum(m_prev, mc_ref[h] + shift)
                p = jnp.exp2(s_ref[h] - (m_new - shift))
            else:
                m_new = jnp.maximum(m_prev, mc_ref[h])
                p = jnp.exp2(s_ref[h] - m_new)
            alpha = jnp.exp2(m_prev - m_new)
            pv = jnp.dot(vt_ref[0, j, h], p.astype(BF16), preferred_element_type=F32)
            acc_ref[h] = alpha * acc_ref[h] + pv
            m_ref[h] = m_new

    stage(2 * i + 1, slot_a, DIAG1)
    consume(2 * i + 1, slot_a, DIAG1)
    stage(2 * i, slot_b, DIAG0)

    @pl.when(i == 0)
    def _():
        consume(2 * i, slot_b, DIAG0)

    @pl.when(i >= 1)
    def _():
        consume(2 * i, slot_b, DIAG0)
        stage(2 * i - 1, slot_a, PREV)
        consume(2 * i - 1, slot_a, PREV)
        if bounded:
            for h in range(N_HEADS):
                acc_ref[h] = acc_ref[h] * sc_ref[N_HEADS + 1 + h]
                lsum_ref[h] = lsum_ref[h] * sc_ref[N_HEADS + 1 + h]
        stage(0, slot_b, FAR)

    def pair_body(jj, carry):
        f = 2 * jj
        consume(f, slot_b, FAR)
        stage(f + 1, slot_a, FAR)
        consume(f + 1, slot_a, FAR)
        stage(f + 2, slot_b, FAR)
        return carry

    lax.fori_loop(0, jnp.maximum(i - 1, 0), pair_body, 0)

    @pl.when(i >= 1)
    def _():
        consume(2 * i - 2, slot_b, FAR)

    if bounded:
        denom = lambda h: jnp.sum(lsum_ref[h], axis=0, keepdims=True)
    else:
        denom = lambda h: acc_ref[h, V_DIM:V_DIM + 1, :]
    _attn_finalize(acc_ref, denom, lam, gsub_ref, o_ref)


def _attn_prompt(scalars, score_bound, qt, kb, vt, bias_t, gsub_row):
    b, nq, cols, tq = qt.shape
    t = kb.shape[1]
    nk = vt.shape[1]
    qbd = pltpu.VMEM((N_HEADS, HEAD_COLS, 2 * tq), BF16)
    acc = pltpu.VMEM((N_HEADS, VT_ROWS, 2 * tq), F32)
    stat = pltpu.VMEM((N_HEADS, 1, 2 * tq), F32)
    scores = pltpu.VMEM((N_HEADS, ATTN_TK, 2 * tq), F32)
    probs = pltpu.VMEM((N_HEADS, ATTN_TK, 2 * tq), BF16)
    colsum = pltpu.VMEM((N_HEADS, 8, 2 * tq), F32)

    def call(bounded, scratch, name):
        return pl.pallas_call(
            functools.partial(_attn_prompt_kernel, bounded=bounded),
            grid=(b, nq),
            in_specs=[pl.BlockSpec(memory_space=pltpu.SMEM),
                      pl.BlockSpec((1, 1, cols, tq), lambda bi, i: (bi, i, 0, 0)),
                      pl.BlockSpec((1, t, cols), lambda bi, i: (bi, 0, 0)),
                      pl.BlockSpec((1, nk, N_HEADS, VT_ROWS, ATTN_TK),
                                   lambda bi, i: (bi, 0, 0, 0, 0)),
                      _const_spec(bias_t.shape), _const_spec((1, V_DIM))],
            out_specs=pl.BlockSpec((1, tq, cols), lambda bi, i: (bi, i, 0)),
            out_shape=jax.ShapeDtypeStruct((b, t, cols), BF16),
            scratch_shapes=scratch,
            compiler_params=_cparams(("parallel", "arbitrary")),
            name=name,
        )(scalars, qt, kb, vt, bias_t, gsub_row)

    return lax.cond(
        score_bound <= MAX_UNSTABILISED_SCORE,
        lambda: call(True, [qbd, acc, probs, probs, colsum], "attn_prompt_bounded"),
        lambda: call(False, [qbd, acc, scores, scores, stat, stat, stat], "attn_prompt"))


def _attn_sample_kernel(sc_ref, q_ref, kt_ref, vc_ref, ktn_ref, vn_ref, bc_ref, bn_ref, gsub_ref,
                        o_ref, qbd_ref, m_ref, l_ref, acc_ref):
    j = pl.program_id(1)
    tq, cols = q_ref.shape[1], q_ref.shape[2]
    lam = sc_ref[N_HEADS]

    @pl.when(j == 0)
    def _():
        q = q_ref[0].astype(F32)
        col = lax.broadcasted_iota(jnp.int32, (tq, cols), 1)
        for hm in range(cols // HEAD_DIM):
            mine = (col >= hm * HEAD_DIM) & (col < (hm + 1) * HEAD_DIM)
            qbd_ref[hm * tq:(hm + 1) * tq, :] = jnp.where(mine, q, 0.0)
        m_ref[...] = jnp.full(m_ref.shape, NEG_INF, F32)
        l_ref[...] = jnp.zeros(l_ref.shape, F32)
        acc_ref[...] = jnp.zeros(acc_ref.shape, F32)

    def update(kt, v_of_head, bias):
        s = jnp.dot(qbd_ref[...].astype(BF16), kt, preferred_element_type=F32) + bias
        m_prev = m_ref[...]
        m_new = jnp.maximum(m_prev, jnp.max(s, axis=-1, keepdims=True))
        alpha = jnp.exp2(m_prev - m_new)
        p = jnp.exp2(s - m_new)
        l_ref[...] = alpha * l_ref[...] + jnp.sum(p, axis=-1, keepdims=True)
        pb = p.astype(BF16)
        for h in range(N_HEADS):
            rows = slice(h * 2 * tq, (h + 1) * 2 * tq)
            acc_ref[rows, :] = alpha[rows] * acc_ref[rows, :] + jnp.dot(
                pb[rows], v_of_head(h), preferred_element_type=F32)
        m_ref[...] = m_new

    tk = kt_ref.shape[2]
    update(kt_ref[0].astype(BF16),
           lambda h: vc_ref[0, pl.ds(h, tk, stride=N_HEADS), :].astype(BF16), bc_ref[j])

    @pl.when(j == pl.num_programs(1) - 1)
    def _():
        update(ktn_ref[0], lambda h: vn_ref[0, :, h * V_DIM:(h + 1) * V_DIM], bn_ref[...])
        for h in range(N_HEADS):
            r0 = h * 2 * tq
            inv0 = 1.0 / l_ref[r0:r0 + tq, :]
            inv1 = 1.0 / l_ref[r0 + tq:r0 + 2 * tq, :]
            o = acc_ref[r0:r0 + tq, :] * inv0 - lam * (acc_ref[r0 + tq:r0 + 2 * tq, :] * inv1)
            o_ref[0, :, h * V_DIM:(h + 1) * V_DIM] = _rms(o, gsub_ref[...]).astype(BF16)


def _attn_sample(scalars, q, kt, v4, ktn, vn, bias_c, bias_n, gsub_row):
    b, tq, cols = q.shape
    p = kt.shape[2]
    nkc = p // CACHE_TK
    rows = 2 * N_HEADS * tq
    return pl.pallas_call(
        _attn_sample_kernel,
        grid=(b, nkc),
        in_specs=[pl.BlockSpec(memory_space=pltpu.SMEM),
                  pl.BlockSpec((1, tq, cols), lambda bi, j: (bi, 0, 0)),
                  pl.BlockSpec((1, cols, CACHE_TK), lambda bi, j: (bi, 0, j)),
                  pl.BlockSpec((1, CACHE_TK * N_HEADS, V_DIM), lambda bi, j: (bi, j, 0)),
                  pl.BlockSpec((1, cols, NEW_KEY_PAD), lambda bi, j: (bi, 0, 0)),
                  pl.BlockSpec((1, NEW_KEY_PAD, cols), lambda bi, j: (bi, 0, 0)),
                  _const_spec(bias_c.shape), _const_spec(bias_n.shape), _const_spec((1, V_DIM))],
        out_specs=pl.BlockSpec((1, tq, cols), lambda bi, j: (bi, 0, 0)),
        out_shape=jax.ShapeDtypeStruct((b, tq, cols), BF16),
        scratch_shapes=[pltpu.VMEM((rows, cols), F32),
                        pltpu.VMEM((rows, 1), F32),
                        pltpu.VMEM((rows, 1), F32),
                        pltpu.VMEM((rows, V_DIM), F32)],
        compiler_params=_cparams(("parallel", "arbitrary")),
        name="attn_sample",
    )(scalars, q, kt, v4, ktn, vn, bias_c, bias_n, gsub_row)


def _mix_kernel(x_ref, o_ref, c_ref, hist_ref, wconv_ref, bconv_ref, lng_ref, lnb_ref,
                wout_ref, gcross_ref, wmq_ref, gmq_ref, mk_ref, mv_ref, wmo_ref,
                out_ref, ext_ref, xs_ref, cv_ref, ca_ref):
    bb, tt, d = x_ref.shape
    rows = bb * tt
    n_mem = mk_ref.shape[1] // MEM_HEADS
    rc = min(CONV_ROWS, tt)
    lead = CONV_HALO - (CONV_K - 1)

    for b in range(bb):
        ext_ref[b, :CONV_HALO, :] = hist_ref[b, 0]
        ext_ref[b, CONV_HALO:, :] = c_ref[b]
    span = xs_ref.shape[2]
    for b in range(bb):
        for r in range(1, 8):
            xs_ref[r - 1, b] = ext_ref[b, r:r + span, :]
    for b in range(bb):
        for r0 in range(0, tt, rc):
            acc = jnp.zeros((rc, c_ref.shape[2]), F32) + bconv_ref[...]
            for k in range(CONV_K):
                a, r = divmod(k + lead, 8)
                lo = r0 + 8 * a
                src = ext_ref[b, lo:lo + rc, :] if r == 0 else xs_ref[r - 1, b, lo:lo + rc, :]
                acc = acc + wconv_ref[k:k + 1, :] * src
            mu = jnp.mean(acc, axis=-1, keepdims=True)
            xc = acc - mu
            var = jnp.mean(xc * xc, axis=-1, keepdims=True)
            y = xc * lax.rsqrt(var + EPS) * lng_ref[...] + lnb_ref[...]
            cv_ref[b * tt + r0:b * tt + r0 + rc, :] = (y * jax.nn.sigmoid(y)).astype(BF16)

    x = x_ref[...].reshape(rows, d)
    o = o_ref[...].reshape(rows, o_ref.shape[2])
    aw = o.shape[1]
    x1 = (x + jnp.dot(o, wout_ref[:aw, :], preferred_element_type=F32)
          + jnp.dot(cv_ref[...], wout_ref[aw:, :], preferred_element_type=F32))

    hc = _rms(x1, gcross_ref[...]).astype(BF16)
    qm = jnp.dot(hc, wmq_ref[...], preferred_element_type=F32)
    for h in range(MEM_HEADS):
        cols = slice(h * MEM_HEAD_DIM, (h + 1) * MEM_HEAD_DIM)
        qn = _rms(qm[:, cols], gmq_ref[...]).astype(BF16)
        for b in range(bb):
            mem_rows = pl.ds(h, n_mem, stride=MEM_HEADS)
            mk = mk_ref[b, mem_rows, :].astype(BF16)
            mv = mv_ref[b, mem_rows, :].astype(BF16)
            s = lax.dot_general(qn[b * tt:(b + 1) * tt], mk, (((1,), (1,)), ((), ())),
                                preferred_element_type=F32)
            p = jnp.exp2(s - jnp.max(s, axis=-1, keepdims=True))
            l = jnp.sum(p, axis=-1, keepdims=True)
            oh = jnp.dot(p.astype(BF16), mv, preferred_element_type=F32) / l
            ca_ref[b * tt:(b + 1) * tt, cols] = oh.astype(BF16)
    x2 = x1 + jnp.dot(ca_ref[...], wmo_ref[...], preferred_element_type=F32)
    out_ref[...] = x2.reshape(bb, tt, d)


def _mix(x, o, c, hist, wconv, bconv, lng, lnb, wout, gcross, wmq, gmq_row, mk, mv, wmo,
         *, bb, tt):
    b, t, d = x.shape
    cw = c.shape[2]
    mem_rows, mhd = mk.shape[1], mk.shape[2]
    mw = wmq.shape[1]
    rows = bb * tt
    tile = lambda width: pl.BlockSpec((bb, tt, width), lambda bi, ti: (bi, ti, 0))
    return pl.pallas_call(
        _mix_kernel,
        grid=(b // bb, t // tt),
        in_specs=[tile(d), tile(cw), tile(cw),
                  pl.BlockSpec((bb, 1, CONV_HALO, cw), lambda bi, ti: (bi, ti, 0, 0)),
                  _const_spec(wconv.shape), _const_spec((1, cw)), _const_spec((1, cw)),
                  _const_spec((1, cw)), _const_spec(wout.shape),
                  _const_spec((1, d)), _const_spec(wmq.shape), _const_spec((1, MEM_HEAD_DIM)),
                  pl.BlockSpec((bb, mem_rows, mhd), lambda bi, ti: (bi, 0, 0)),
                  pl.BlockSpec((bb, mem_rows, mhd), lambda bi, ti: (bi, 0, 0)),
                  _const_spec(wmo.shape)],
        out_specs=tile(d),
        out_shape=jax.ShapeDtypeStruct((b, t, d), F32),
        scratch_shapes=[pltpu.VMEM((bb, CONV_HALO + tt, cw), F32),
                        pltpu.VMEM((7, bb, CONV_HALO - 8 + tt, cw), F32),
                        pltpu.VMEM((rows, cw), BF16),
                        pltpu.VMEM((rows, mw), BF16)],
        compiler_params=_cparams(("parallel", "parallel")),
        name="mix",
    )(x, o, c, hist, wconv, bconv, lng, lnb, wout, gcross, wmq, gmq_row, mk, mv, wmo)


def _mlp_kernel(x_ref, g_ref, w1_ref, w2_ref, out_ref):
    x = x_ref[...]
    hf = _rms(x, g_ref[...]).astype(BF16)
    u = jnp.maximum(jnp.dot(hf, w1_ref[...], preferred_element_type=F32), 0.0)
    out_ref[...] = x + jnp.dot((u * u).astype(BF16), w2_ref[...], preferred_element_type=F32)


def _mlp(x2d, g_ffn, w1, w2):
    n, d = x2d.shape
    tm = min(ROW_TILE, n)
    return pl.pallas_call(
        _mlp_kernel,
        grid=(n // tm,),
        in_specs=[pl.BlockSpec((tm, d), lambda i: (i, 0)), _const_spec((1, d)),
                  _const_spec(w1.shape), _const_spec(w2.shape)],
        out_specs=pl.BlockSpec((tm, d), lambda i: (i, 0)),
        out_shape=jax.ShapeDtypeStruct((n, d), F32),
        compiler_params=_cparams(("parallel",)),
        name="mlp",
    )(x2d, g_ffn.reshape(1, d), w1, w2)


def _rel_bucket(rel):
    half = N_BUCKETS // 2
    max_exact = half // 2
    ret = jnp.where(rel > 0, half, 0)
    n = jnp.abs(rel)
    nf = jnp.maximum(n, 1).astype(jnp.float32)
    large = max_exact + (jnp.log(nf / max_exact) / math.log(MAX_DISTANCE / max_exact)
                         * (half - max_exact)).astype(jnp.int32)
    large = jnp.minimum(large, half - 1)
    return ret + jnp.where(n < max_exact, n, large)


def _masked_bias(rel_table, q0, nq, k0, nk):
    rel = (k0 - q0) + jnp.arange(-(nq - 1), nk, dtype=jnp.int32)
    bucket = _rel_bucket(rel)[None]
    table = rel_table.astype(F32) * LOG2E
    vec = jnp.zeros((rel_table.shape[1], rel.shape[0]), F32)
    for bkt in range(N_BUCKETS):
        vec = jnp.where(bucket == bkt, table[bkt][:, None], vec)
    period = nq - 1 + nk
    skew = jnp.tile(vec, (1, nq + 1))[:, :nq * (period + 1)].reshape(-1, nq, period + 1)
    bias = skew[:, ::-1, :nk]
    q_pos = q0 + jnp.arange(nq, dtype=jnp.int32)
    k_pos = k0 + jnp.arange(nk, dtype=jnp.int32)
    mask = (k_pos[None, :] // CHUNK) <= (q_pos[:, None] // CHUNK)
    return jnp.where(mask[None], bias, NEG_INF)


def _far_bucket_is_saturated(min_distance):
    half = N_BUCKETS // 2
    max_exact = half // 2
    large = max_exact + int(np.log(min_distance / max_exact) / math.log(MAX_DISTANCE / max_exact)
                            * (half - max_exact) * (1 - 1e-6))
    return large >= half - 1


def _layer(x, k_past, v_past, c_past, mk, mv, w, lam, lam_init):
    b, t, d = x.shape
    prompt = k_past is None
    cols = N_HEADS * HEAD_COLS
    gsub_row = (w["g_sub"] * (1.0 - lam_init)).reshape(1, V_DIM)

    if prompt:
        assert t % ROW_TILE == 0 and ROW_TILE % ATTN_TQ == 0 and ATTN_TQ == 2 * ATTN_TK
        assert ATTN_TK % CHUNK == 0 and _far_bucket_is_saturated(ATTN_TK + 1)
        qt, kb, vt, kt, v4, c = _inproj(x, w["g_mix"], w["w_in"], w["gq_row"], w["gk_row"],
                                        w["seg"], emit_t=True)
        k_out = jnp.transpose(kt.reshape(b, N_HEADS, 2, HEAD_DIM, t), (0, 4, 1, 2, 3))
        v_out = v4.reshape(b, t, N_HEADS, V_DIM)
        near = _masked_bias(w["rel_table"], ATTN_TQ, ATTN_TQ, ATTN_TK, 3 * ATTN_TK)
        near = jnp.stack([near[:, :, n * ATTN_TK:(n + 1) * ATTN_TK] for n in range(3)], axis=1)
        near_t = jnp.swapaxes(near, 2, 3)
        bias_t = jnp.concatenate([near_t, near_t], axis=3)
        far = w["rel_table"][_rel_bucket(jnp.int32(-(ATTN_TK + 1)))].astype(F32) * LOG2E
        scalars = jnp.concatenate([far, lam.reshape(1), jnp.exp2(-far)]).astype(F32)
        score_bound = (HEAD_DIM * jnp.max(jnp.abs(w["gq_row"])) * jnp.max(jnp.abs(w["gk_row"]))
                       + LOG2E * jnp.max(jnp.abs(w["rel_table"])))
        o = _attn_prompt(scalars, score_bound, qt, kb, vt, bias_t, gsub_row)
        bb, tt = 1, ROW_TILE
    else:
        p = k_past.shape[2]
        assert p % CACHE_TK == 0 and t <= NEW_KEY_PAD and t % 16 == 0
        qb, k, v, c = _inproj(x.reshape(1, b * t, d), w["g_mix"], w["w_in"], w["gq_row"],
                              w["gk_row"], w["seg"], emit_t=False)
        qb, k, v, c = (a.reshape(b, t, cols) for a in (qb, k, v, c))
        k_out = k.reshape(b, t, N_HEADS, 2, HEAD_DIM)
        v_out = v.reshape(b, t, N_HEADS, V_DIM)
        ktn = jnp.pad(jnp.swapaxes(k, 1, 2).astype(BF16), ((0, 0), (0, 0), (0, NEW_KEY_PAD - t)))
        vn = jnp.pad(v.astype(BF16), ((0, 0), (0, NEW_KEY_PAD - t), (0, 0)))
        bias = _masked_bias(w["rel_table"], p, t, 0, p + NEW_KEY_PAD)
        bias = jnp.where(jnp.arange(p + NEW_KEY_PAD) < p + t, bias, NEG_INF)
        rows = 2 * N_HEADS * t
        bias = jnp.broadcast_to(bias[:, None], (N_HEADS, 2, t, p + NEW_KEY_PAD)).reshape(rows, -1)
        bias_c = jnp.swapaxes(bias[:, :p].reshape(rows, p // CACHE_TK, CACHE_TK), 0, 1)
        scalars = jnp.concatenate([jnp.zeros((N_HEADS,), F32), lam.reshape(1)]).astype(F32)
        o = _attn_sample(scalars, qb, k_past, v_past, ktn, vn, bias_c, bias[:, p:], gsub_row)
        bb, tt = SAMPLE_BATCH_TILE, t
        assert b % bb == 0

    nt = t // tt
    first = jnp.pad(c_past, ((0, 0), (CONV_HALO - (CONV_K - 1), 0), (0, 0)))[:, None]
    if nt > 1:
        tails = c.reshape(b, nt, tt, c.shape[2])[:, :-1, tt - CONV_HALO:, :]
        hist = jnp.concatenate([first, tails], axis=1)
    else:
        hist = first
    x2 = _mix(x, o, c, hist, w["w_conv"], w["b_conv"], w["ln_g"], w["ln_b"], w["w_out"],
              w["g_cross"], w["w_mq"], w["gmq_row"], mk, mv, w["w_mo"], bb=bb, tt=tt)
    y = _mlp(x2.reshape(b * t, d), w["g_ffn"], w["w_ff1"], w["w_ff2"]).reshape(b, t, d)
    if t >= CONV_K - 1:
        c_hist_tail = c[:, t - (CONV_K - 1):]
    else:
        c_hist_tail = jnp.concatenate([c_past[:, t:], c], axis=1)
    return y, k_out, v_out, c_hist_tail


def kernel(x_prompt, x_sample, cache_k, cache_v, cache_conv, cache_mem_k, cache_mem_v, mem_prompt,
           rel_table, g_mix, w_in, g_q, g_k, lam_vec, g_sub, w_conv, b_conv, ln_g, ln_b, w_out,
           g_cross, g_mem, w_mq, w_mk, w_mv, g_mq, g_mk, w_mo, g_ffn, w_ff1, w_ff2):
    depth = g_mix.shape[0]
    assert depth == 1
    b, t, d = x_prompt.shape
    bs, ts, _ = x_sample.shape
    cols = N_HEADS * HEAD_COLS
    cw = w_conv.shape[2]
    l = 0
    lam_init = 0.8 - 0.6 * math.exp(-0.3 * l)
    lp = lam_vec[l].astype(F32)
    lam = jnp.exp(jnp.sum(lp[0] * lp[1])) - jnp.exp(jnp.sum(lp[2] * lp[3])) + lam_init

    seg = jnp.kron(jnp.eye(cols // HEAD_DIM, dtype=F32),
                   jnp.full((HEAD_DIM, HEAD_DIM), 1.0 / HEAD_DIM, F32)).astype(BF16)
    n_maps = cols // HEAD_DIM
    w = dict(
        rel_table=rel_table, g_mix=g_mix[l], w_in=w_in[l].astype(BF16), seg=seg,
        gq_row=jnp.tile(g_q[l] * (HEAD_DIM ** -0.5 * LOG2E), n_maps).reshape(1, cols),
        gk_row=jnp.tile(g_k[l], n_maps).reshape(1, cols),
        g_sub=g_sub[l],
        w_conv=jnp.pad(w_conv[l], ((0, CONV_HALO - CONV_K), (0, 0))),
        b_conv=b_conv[l].reshape(1, cw), ln_g=ln_g[l].reshape(1, cw), ln_b=ln_b[l].reshape(1, cw),
        w_out=w_out[l].astype(BF16),
        g_cross=g_cross[l].reshape(1, d), w_mq=w_mq[l].astype(BF16),
        gmq_row=(g_mq[l] * (MEM_HEAD_DIM ** -0.5 * LOG2E)).reshape(1, MEM_HEAD_DIM),
        w_mo=w_mo[l].astype(BF16), g_ffn=g_ffn[l],
        w_ff1=w_ff1[l].astype(BF16), w_ff2=w_ff2[l].astype(BF16),
    )

    n_mem = mem_prompt.shape[1]
    mk_p, mv_p = _memkv(mem_prompt.reshape(b * n_mem, d), g_mem[l], w_mk[l], w_mv[l], g_mk[l])
    mk_p = mk_p.reshape(b, n_mem * MEM_HEADS, MEM_HEAD_DIM)
    mv_p = mv_p.reshape(b, n_mem * MEM_HEADS, MEM_HEAD_DIM)

    zero_conv = jnp.zeros((b, CONV_K - 1, cw), F32)
    yp, kp, vp, cp = _layer(x_prompt, None, None, zero_conv, mk_p, mv_p, w, lam, lam_init)

    p = cache_k.shape[2]
    k_past = jnp.transpose(cache_k[l], (0, 2, 3, 4, 1)).reshape(bs, cols, p)
    v_past = cache_v[l].reshape(bs, p * N_HEADS, V_DIM)
    ys, kn, vn, cn = _layer(x_sample, k_past, v_past, cache_conv[l],
                            cache_mem_k[l].reshape(bs, n_mem * MEM_HEADS, MEM_HEAD_DIM),
                            cache_mem_v[l].reshape(bs, n_mem * MEM_HEADS, MEM_HEAD_DIM),
                            w, lam, lam_init)

    return (yp, ys, kp[None], vp[None], cp[None],
            mk_p.reshape(1, b, n_mem, MEM_HEADS, MEM_HEAD_DIM),
            mv_p.reshape(1, b, n_mem, MEM_HEADS, MEM_HEAD_DIM),
            kn[None], vn[None], cn[None])
```

```python
import functools
import math

import jax
import jax.numpy as jnp
import numpy as np
from jax import lax
from jax.experimental import pallas as pl
from jax.experimental.pallas import tpu as pltpu

F32 = jnp.float32
BF16 = jnp.bfloat16

CHUNK = 64
N_HEADS = 4
HEAD_DIM = 64
V_DIM = 2 * HEAD_DIM
HEAD_COLS = 2 * HEAD_DIM
CONV_K = 31
CONV_HALO = 32
N_BUCKETS = 32
MAX_DISTANCE = 128
MEM_HEADS = 4
MEM_HEAD_DIM = 128
EPS = 1e-6
NEG_INF = -1e30
LOG2E = math.log2(math.e)
SUBLANES = 8

ATTN_TQ = 512
ATTN_TK = 256
VT_ROWS = V_DIM + 16
MAX_UNSTABILISED_SCORE = 100.0
CACHE_TK = 4096
NEW_KEY_PAD = 128
ROW_TILE = 512
CONV_ROWS = 64
SAMPLE_BATCH_TILE = 8
VMEM_LIMIT = 56 * 1024 * 1024


def _cparams(sem):
    return pltpu.CompilerParams(dimension_semantics=sem, vmem_limit_bytes=VMEM_LIMIT)


def _rms(x, g):
    ms = jnp.mean(x * x, axis=-1, keepdims=True)
    return x * lax.rsqrt(ms + EPS) * g


def _const_spec(shape):
    return pl.BlockSpec(shape, lambda *_: (0,) * len(shape), pipeline_mode=pl.Buffered(1))


def _memkv_kernel(mem_ref, g_ref, wk_ref, wv_ref, gk_ref, mk_ref, mv_ref):
    m = _rms(mem_ref[...], g_ref[...]).astype(BF16)
    zk = jnp.dot(m, wk_ref[...], preferred_element_type=F32)
    zv = jnp.dot(m, wv_ref[...], preferred_element_type=F32)
    tm = mem_ref.shape[0]
    for h in range(MEM_HEADS):
        sl = slice(h * MEM_HEAD_DIM, (h + 1) * MEM_HEAD_DIM)
        rows = pl.ds(h, tm, stride=MEM_HEADS)
        mk_ref[rows, :] = _rms(zk[:, sl], gk_ref[...])
        mv_ref[rows, :] = zv[:, sl]


def _memkv(mem2d, g_mem, w_mk, w_mv, g_mk):
    n, d = mem2d.shape
    w = w_mk.shape[1]
    tm = min(ROW_TILE, n)
    return pl.pallas_call(
        _memkv_kernel,
        grid=(n // tm,),
        in_specs=[pl.BlockSpec((tm, d), lambda i: (i, 0)), _const_spec((1, d)),
                  _const_spec((d, w)), _const_spec((d, w)), _const_spec((1, MEM_HEAD_DIM))],
        out_specs=[pl.BlockSpec((tm * MEM_HEADS, MEM_HEAD_DIM), lambda i: (i, 0))] * 2,
        out_shape=[jax.ShapeDtypeStruct((n * MEM_HEADS, MEM_HEAD_DIM), F32)] * 2,
        compiler_params=_cparams(("parallel",)),
        name="memkv",
    )(mem2d, g_mem.reshape(1, d), w_mk.astype(BF16), w_mv.astype(BF16),
      g_mk.reshape(1, MEM_HEAD_DIM))


def _inproj_kernel(x_ref, g_ref, w_ref, gq_ref, gk_ref, *refs, qk_cols, emit_t):
    if emit_t:
        qt_ref, kb_ref, vt_ref, kt_ref, v4_ref, c_ref = refs
    else:
        seg_ref, qb_ref, k_ref, v_ref, c_ref = refs

    def mapnorm(z, g):
        ms = jnp.dot((z * z).astype(BF16), seg_ref[...], preferred_element_type=F32)
        return z * lax.rsqrt(ms + EPS) * g

    def mapnorm_t(z, g_t):
        zt = z.T
        z3 = zt.reshape(zt.shape[0] // HEAD_DIM, HEAD_DIM, zt.shape[1])
        ms = jnp.mean(z3 * z3, axis=1, keepdims=True)
        return (z3 * lax.rsqrt(ms + EPS)).reshape(zt.shape) * g_t

    h = _rms(x_ref[0], g_ref[...]).astype(BF16)

    def proj(lo, hi):
        return jnp.dot(h, w_ref[:, lo:hi], preferred_element_type=F32)

    c0 = qk_cols
    zq = proj(0, c0)
    zk = proj(c0, 2 * c0)
    v = proj(2 * c0, 3 * c0)
    a = proj(3 * c0, 4 * c0)
    gate = proj(4 * c0, 5 * c0)
    c_ref[0] = a * jax.nn.sigmoid(gate)
    if emit_t:
        kt = mapnorm_t(zk, gk_ref[...])
        kt_ref[0] = kt
        kb_ref[0] = kt.T.astype(BF16)
        tm = x_ref.shape[1]
        for hh in range(N_HEADS):
            v4_ref[0, pl.ds(hh, tm, stride=N_HEADS), :] = v[:, hh * V_DIM:(hh + 1) * V_DIM]
        qt = mapnorm_t(zq, gq_ref[...]).astype(BF16)
        vt = v.T.astype(BF16)
        for j in range(qt_ref.shape[1]):
            qt_ref[0, j] = qt[:, j * ATTN_TQ:(j + 1) * ATTN_TQ]
        ones_row = (lax.broadcasted_iota(jnp.int32, (VT_ROWS - V_DIM, ATTN_TK), 0) == 0).astype(BF16)
        for j in range(vt_ref.shape[1]):
            for hh in range(N_HEADS):
                vt_ref[0, j, hh, :V_DIM, :] = vt[hh * V_DIM:(hh + 1) * V_DIM,
                                                 j * ATTN_TK:(j + 1) * ATTN_TK]
                vt_ref[0, j, hh, V_DIM:, :] = ones_row
    else:
        k_ref[0] = mapnorm(zk, gk_ref[...])
        v_ref[0] = v
        qb_ref[0] = mapnorm(zq, gq_ref[...]).astype(BF16)


def _inproj(x, g_mix, w_in_bf, gq_row, gk_row, seg, *, emit_t):
    b, t, d = x.shape
    cols = gq_row.shape[1]
    tm = min(ROW_TILE, t)
    nt = t // tm
    row = lambda width: pl.BlockSpec((1, tm, width), lambda bi, ti: (bi, ti, 0))
    f32_out = jax.ShapeDtypeStruct((b, t, cols), F32)
    if emit_t:
        nq, nk = tm // ATTN_TQ, tm // ATTN_TK
        out_shape = [jax.ShapeDtypeStruct((b, t // ATTN_TQ, cols, ATTN_TQ), BF16),
                     jax.ShapeDtypeStruct((b, t, cols), BF16),
                     jax.ShapeDtypeStruct((b, t // ATTN_TK, N_HEADS, VT_ROWS, ATTN_TK), BF16),
                     jax.ShapeDtypeStruct((b, cols, t), F32),
                     jax.ShapeDtypeStruct((b, t * N_HEADS, V_DIM), F32),
                     f32_out]
        out_specs = [pl.BlockSpec((1, nq, cols, ATTN_TQ), lambda bi, ti: (bi, ti, 0, 0)),
                     row(cols),
                     pl.BlockSpec((1, nk, N_HEADS, VT_ROWS, ATTN_TK),
                                  lambda bi, ti: (bi, ti, 0, 0, 0)),
                     pl.BlockSpec((1, cols, tm), lambda bi, ti: (bi, 0, ti)),
                     pl.BlockSpec((1, tm * N_HEADS, V_DIM), lambda bi, ti: (bi, ti, 0)),
                     row(cols)]
        gains = [jnp.broadcast_to(g.reshape(cols, 1), (cols, tm)) for g in (gq_row, gk_row)]
        gain_specs = [_const_spec((cols, tm))] * 2
    else:
        out_shape = [jax.ShapeDtypeStruct((b, t, cols), BF16)] + [f32_out] * 3
        out_specs = [row(cols)] * 4
        gains = [gq_row, gk_row, seg]
        gain_specs = [_const_spec((1, cols)), _const_spec((1, cols)), _const_spec((cols, cols))]
    return pl.pallas_call(
        functools.partial(_inproj_kernel, qk_cols=cols, emit_t=emit_t),
        grid=(b, nt),
        in_specs=[row(d), _const_spec((1, d)), _const_spec(w_in_bf.shape)] + gain_specs,
        out_specs=out_specs,
        out_shape=out_shape,
        compiler_params=_cparams(("parallel", "parallel")),
        name="inproj",
    )(x, g_mix.reshape(1, d), w_in_bf, *gains)


PREV, DIAG0, DIAG1, FAR = 0, 1, 2, None


def _build_qbd(qt_ref, qbd_ref):
    w = ATTN_TK
    upper = lax.broadcasted_iota(jnp.int32, (HEAD_COLS, w), 0) < HEAD_DIM
    for h in range(N_HEADS):
        for half in range(ATTN_TQ // w):
            qh = qt_ref[0, 0, h * HEAD_COLS:(h + 1) * HEAD_COLS, half * w:(half + 1) * w]
            zero = jnp.zeros_like(qh)
            qbd_ref[h, :, 2 * half * w:(2 * half + 1) * w] = jnp.where(upper, qh, zero)
            qbd_ref[h, :, (2 * half + 1) * w:(2 * half + 2) * w] = jnp.where(upper, zero, qh)


def _attn_finalize(acc_ref, denom, lam, gsub_ref, o_ref):
    w = ATTN_TK
    for h in range(N_HEADS):
        acc = acc_ref[h, :V_DIM, :] * (1.0 / denom(h))
        for half in range(ATTN_TQ // w):
            m0 = acc[:, 2 * half * w:(2 * half + 1) * w]
            m1 = acc[:, (2 * half + 1) * w:(2 * half + 2) * w]
            ot = m0 - lam * m1
            ms = jnp.mean(ot * ot, axis=0, keepdims=True)
            ot = ot * lax.rsqrt(ms + EPS)
            o_ref[0, half * w:(half + 1) * w, h * V_DIM:(h + 1) * V_DIM] = (
                ot.T * gsub_ref[...]).astype(BF16)


def _attn_prompt_kernel(sc_ref, qt_ref, kb_ref, vt_ref, bias_ref, gsub_ref, o_ref,
                        qbd_ref, acc_ref, sa_ref, sb_ref, *stat_refs, bounded):
    i = pl.program_id(1)
    tq, tk = ATTN_TQ, ATTN_TK
    lam = sc_ref[N_HEADS]
    if bounded:
        lsum_ref, = stat_refs
        slot_a, slot_b = (sa_ref, None), (sb_ref, None)
        lsum_ref[...] = jnp.zeros(lsum_ref.shape, F32)
    else:
        m_ref, mca_ref, mcb_ref = stat_refs
        slot_a, slot_b = (sa_ref, mca_ref), (sb_ref, mcb_ref)
        m_ref[...] = jnp.full(m_ref.shape, NEG_INF, F32)

    _build_qbd(qt_ref, qbd_ref)
    acc_ref[...] = jnp.zeros(acc_ref.shape, F32)

    every = slice(0, 2 * tq)

    def stage(j, slot, tile, cols=every):
        s_ref, mc_ref = slot
        row0 = pl.multiple_of(j * tk, tk)
        width = cols.stop - cols.start
        for h in range(N_HEADS):
            kh = kb_ref[0, pl.ds(row0, tk), h * HEAD_COLS:(h + 1) * HEAD_COLS]
            s = jnp.dot(kh, qbd_ref[h, :, cols], preferred_element_type=F32)
            if tile is not FAR:
                s = s + bias_ref[h, tile, :, cols]
            if bounded:
                p = jnp.exp2(s)
                s_ref[h, :, cols] = p.astype(BF16)
                lsum_ref[h, :, cols] += jnp.sum(p.reshape(tk // SUBLANES, SUBLANES, width), axis=0)
            else:
                s_ref[h, :, cols] = s
                mc_ref[h, :, cols] = jnp.max(s, axis=0, keepdims=True)

    def consume(j, slot, tile, cols=every):
        s_ref, mc_ref = slot
        for h in range(N_HEADS):
            if bounded:
                acc_ref[h, :V_DIM, cols] += jnp.dot(vt_ref[0, j, h, :V_DIM, :], s_ref[h, :, cols],
                                                    preferred_element_type=F32)
                continue
            m_prev = m_ref[h, :, cols]
            if tile is FAR:
                shift = sc_ref[h]
                m_new = jnp.maximum(m_prev, mc_ref[h, :, cols] + shift)
                p = jnp.exp2(s_ref[h, :, cols] - (m_new - shift))
            else:
                m_new = jnp.maximum(m_prev, mc_ref[h, :, cols])
                p = jnp.exp2(s_ref[h, :, cols] - m_new)
            alpha = jnp.exp2(m_prev - m_new)
            pv = jnp.dot(vt_ref[0, j, h], p.astype(BF16), preferred_element_type=F32)
            acc_ref[h, :, cols] = alpha * acc_ref[h, :, cols] + pv
            m_ref[h, :, cols] = m_new

    second_half = slice(tq, 2 * tq)
    stage(2 * i + 1, slot_a, DIAG1, second_half)
    consume(2 * i + 1, slot_a, DIAG1, second_half)
    stage(2 * i, slot_b, DIAG0)

    @pl.when(i == 0)
    def _():
        consume(2 * i, slot_b, DIAG0)

    @pl.when(i >= 1)
    def _():
        consume(2 * i, slot_b, DIAG0)
        stage(2 * i - 1, slot_a, PREV)
        consume(2 * i - 1, slot_a, PREV)
        if bounded:
            for h in range(N_HEADS):
                acc_ref[h] = acc_ref[h] * sc_ref[N_HEADS + 1 + h]
                lsum_ref[h] = lsum_ref[h] * sc_ref[N_HEADS + 1 + h]
        stage(0, slot_b, FAR)

    def pair_body(jj, carry):
        f = 2 * jj
        consume(f, slot_b, FAR)
        stage(f + 1, slot_a, FAR)
        consume(f + 1, slot_a, FAR)
        stage(f + 2, slot_b, FAR)
        return carry

    lax.fori_loop(0, jnp.maximum(i - 1, 0), pair_body, 0)

    @pl.when(i >= 1)
    def _():
        consume(2 * i - 2, slot_b, FAR)

    if bounded:
        denom = lambda h: jnp.sum(lsum_ref[h], axis=0, keepdims=True)
    else:
        denom = lambda h: acc_ref[h, V_DIM:V_DIM + 1, :]
    _attn_finalize(acc_ref, denom, lam, gsub_ref, o_ref)


def _attn_prompt(scalars, score_bound, qt, kb, vt, bias_t, gsub_row):
    b, nq, cols, tq = qt.shape
    t = kb.shape[1]
    nk = vt.shape[1]
    qbd = pltpu.VMEM((N_HEADS, HEAD_COLS, 2 * tq), BF16)
    acc = pltpu.VMEM((N_HEADS, VT_ROWS, 2 * tq), F32)
    stat = pltpu.VMEM((N_HEADS, 1, 2 * tq), F32)
    scores = pltpu.VMEM((N_HEADS, ATTN_TK, 2 * tq), F32)
    probs = pltpu.VMEM((N_HEADS, ATTN_TK, 2 * tq), BF16)
    colsum = pltpu.VMEM((N_HEADS, SUBLANES, 2 * tq), F32)

    def call(bounded, scratch, name):
        return pl.pallas_call(
            functools.partial(_attn_prompt_kernel, bounded=bounded),
            grid=(b, nq),
            in_specs=[pl.BlockSpec(memory_space=pltpu.SMEM),
                      pl.BlockSpec((1, 1, cols, tq), lambda bi, i: (bi, i, 0, 0)),
                      pl.BlockSpec((1, t, cols), lambda bi, i: (bi, 0, 0)),
                      pl.BlockSpec((1, nk, N_HEADS, VT_ROWS, ATTN_TK),
                                   lambda bi, i: (bi, 0, 0, 0, 0)),
                      _const_spec(bias_t.shape), _const_spec((1, V_DIM))],
            out_specs=pl.BlockSpec((1, tq, cols), lambda bi, i: (bi, i, 0)),
            out_shape=jax.ShapeDtypeStruct((b, t, cols), BF16),
            scratch_shapes=scratch,
            compiler_params=_cparams(("parallel", "arbitrary")),
            name=name,
        )(scalars, qt, kb, vt, bias_t, gsub_row)

    return lax.cond(
        score_bound <= MAX_UNSTABILISED_SCORE,
        lambda: call(True, [qbd, acc, probs, probs, colsum], "attn_prompt_bounded"),
        lambda: call(False, [qbd, acc, scores, scores, stat, stat, stat], "attn_prompt"))


def _attn_sample_kernel(sc_ref, q_ref, kt_ref, vc_ref, ktn_ref, vn_ref, bc_ref, bn_ref, gsub_ref,
                        o_ref, qbd_ref, m_ref, l_ref, acc_ref):
    j = pl.program_id(1)
    tq, cols = q_ref.shape[1], q_ref.shape[2]
    lam = sc_ref[N_HEADS]

    @pl.when(j == 0)
    def _():
        q = q_ref[0].astype(F32)
        col = lax.broadcasted_iota(jnp.int32, (tq, cols), 1)
        for hm in range(cols // HEAD_DIM):
            mine = (col >= hm * HEAD_DIM) & (col < (hm + 1) * HEAD_DIM)
            qbd_ref[hm * tq:(hm + 1) * tq, :] = jnp.where(mine, q, 0.0)
        m_ref[...] = jnp.full(m_ref.shape, NEG_INF, F32)
        l_ref[...] = jnp.zeros(l_ref.shape, F32)
        acc_ref[...] = jnp.zeros(acc_ref.shape, F32)

    def update(kt, v_of_head, bias):
        s = jnp.dot(qbd_ref[...].astype(BF16), kt, preferred_element_type=F32) + bias
        m_prev = m_ref[...]
        m_new = jnp.maximum(m_prev, jnp.max(s, axis=-1, keepdims=True))
        alpha = jnp.exp2(m_prev - m_new)
        p = jnp.exp2(s - m_new)
        l_ref[...] = alpha * l_ref[...] + jnp.sum(p, axis=-1, keepdims=True)
        pb = p.astype(BF16)
        for h in range(N_HEADS):
            rows = slice(h * 2 * tq, (h + 1) * 2 * tq)
            acc_ref[rows, :] = alpha[rows] * acc_ref[rows, :] + jnp.dot(
                pb[rows], v_of_head(h), preferred_element_type=F32)
        m_ref[...] = m_new

    tk = kt_ref.shape[2]
    update(kt_ref[0].astype(BF16),
           lambda h: vc_ref[0, pl.ds(h, tk, stride=N_HEADS), :].astype(BF16), bc_ref[j])

    @pl.when(j == pl.num_programs(1) - 1)
    def _():
        update(ktn_ref[0], lambda h: vn_ref[0, :, h * V_DIM:(h + 1) * V_DIM], bn_ref[...])
        for h in range(N_HEADS):
            r0 = h * 2 * tq
            inv0 = 1.0 / l_ref[r0:r0 + tq, :]
            inv1 = 1.0 / l_ref[r0 + tq:r0 + 2 * tq, :]
            o = acc_ref[r0:r0 + tq, :] * inv0 - lam * (acc_ref[r0 + tq:r0 + 2 * tq, :] * inv1)
            o_ref[0, :, h * V_DIM:(h + 1) * V_DIM] = _rms(o, gsub_ref[...]).astype(BF16)


def _attn_sample(scalars, q, kt, v4, ktn, vn, bias_c, bias_n, gsub_row):
    b, tq, cols = q.shape
    p = kt.shape[2]
    nkc = p // CACHE_TK
    rows = 2 * N_HEADS * tq
    return pl.pallas_call(
        _attn_sample_kernel,
        grid=(b, nkc),
        in_specs=[pl.BlockSpec(memory_space=pltpu.SMEM),
                  pl.BlockSpec((1, tq, cols), lambda bi, j: (bi, 0, 0)),
                  pl.BlockSpec((1, cols, CACHE_TK), lambda bi, j: (bi, 0, j)),
                  pl.BlockSpec((1, CACHE_TK * N_HEADS, V_DIM), lambda bi, j: (bi, j, 0)),
                  pl.BlockSpec((1, cols, NEW_KEY_PAD), lambda bi, j: (bi, 0, 0)),
                  pl.BlockSpec((1, NEW_KEY_PAD, cols), lambda bi, j: (bi, 0, 0)),
                  _const_spec(bias_c.shape), _const_spec(bias_n.shape), _const_spec((1, V_DIM))],
        out_specs=pl.BlockSpec((1, tq, cols), lambda bi, j: (bi, 0, 0)),
        out_shape=jax.ShapeDtypeStruct((b, tq, cols), BF16),
        scratch_shapes=[pltpu.VMEM((rows, cols), F32),
                        pltpu.VMEM((rows, 1), F32),
                        pltpu.VMEM((rows, 1), F32),
                        pltpu.VMEM((rows, V_DIM), F32)],
        compiler_params=_cparams(("parallel", "arbitrary")),
        name="attn_sample",
    )(scalars, q, kt, v4, ktn, vn, bias_c, bias_n, gsub_row)


def _mix_kernel(x_ref, o_ref, c_ref, hist_ref, wconv_ref, bconv_ref, lng_ref, lnb_ref,
                woo_ref, woc_ref, gcross_ref, wmq_ref, gmq_ref, mk_ref, mv_ref, wmo_ref,
                out_ref, ext_ref, xs_ref, cv_ref, ca_ref):
    bb, tt, d = x_ref.shape
    rows = bb * tt
    n_mem = mk_ref.shape[1] // MEM_HEADS
    rc = min(CONV_ROWS, tt)
    lead = CONV_HALO - (CONV_K - 1)

    for b in range(bb):
        ext_ref[b, :CONV_HALO, :] = hist_ref[b, 0]
        ext_ref[b, CONV_HALO:, :] = c_ref[b]
    span = xs_ref.shape[2]
    for b in range(bb):
        for r in range(1, SUBLANES):
            xs_ref[r - 1, b] = ext_ref[b, r:r + span, :]
    for b in range(bb):
        for r0 in range(0, tt, rc):
            acc = jnp.zeros((rc, c_ref.shape[2]), F32) + bconv_ref[...]
            for k in range(CONV_K):
                a, r = divmod(k + lead, SUBLANES)
                lo = r0 + SUBLANES * a
                src = ext_ref[b, lo:lo + rc, :] if r == 0 else xs_ref[r - 1, b, lo:lo + rc, :]
                acc = acc + wconv_ref[k:k + 1, :] * src
            mu = jnp.mean(acc, axis=-1, keepdims=True)
            xc = acc - mu
            var = jnp.mean(xc * xc, axis=-1, keepdims=True)
            y = xc * lax.rsqrt(var + EPS) * lng_ref[...] + lnb_ref[...]
            cv_ref[b * tt + r0:b * tt + r0 + rc, :] = (y * jax.nn.sigmoid(y)).astype(BF16)

    x = x_ref[...].reshape(rows, d)
    o = o_ref[...].reshape(rows, o_ref.shape[2])
    x1 = (x + jnp.dot(o, woo_ref[...], preferred_element_type=F32)
          + jnp.dot(cv_ref[...], woc_ref[...], preferred_element_type=F32))

    hc = _rms(x1, gcross_ref[...]).astype(BF16)
    qm = jnp.dot(hc, wmq_ref[...], preferred_element_type=F32)
    for h in range(MEM_HEADS):
        cols = slice(h * MEM_HEAD_DIM, (h + 1) * MEM_HEAD_DIM)
        qn = _rms(qm[:, cols], gmq_ref[...]).astype(BF16)
        for b in range(bb):
            mem_rows = pl.ds(h, n_mem, stride=MEM_HEADS)
            mk = mk_ref[b, mem_rows, :].astype(BF16)
            mv = mv_ref[b, mem_rows, :].astype(BF16)
            s = lax.dot_general(qn[b * tt:(b + 1) * tt], mk, (((1,), (1,)), ((), ())),
                                preferred_element_type=F32)
            p = jnp.exp2(s - jnp.max(s, axis=-1, keepdims=True))
            l = jnp.sum(p, axis=-1, keepdims=True)
            oh = jnp.dot(p.astype(BF16), mv, preferred_element_type=F32) / l
            ca_ref[b * tt:(b + 1) * tt, cols] = oh.astype(BF16)
    x2 = x1 + jnp.dot(ca_ref[...], wmo_ref[...], preferred_element_type=F32)
    out_ref[...] = x2.reshape(bb, tt, d)


def _mix(x, o, c, hist, wconv, bconv, lng, lnb, woo, woc, gcross, wmq, gmq_row, mk, mv, wmo,
         *, bb, tt):
    b, t, d = x.shape
    cw = c.shape[2]
    mem_rows, mhd = mk.shape[1], mk.shape[2]
    mw = wmq.shape[1]
    rows = bb * tt
    tile = lambda width: pl.BlockSpec((bb, tt, width), lambda bi, ti: (bi, ti, 0))
    return pl.pallas_call(
        _mix_kernel,
        grid=(b // bb, t // tt),
        in_specs=[tile(d), tile(cw), tile(cw),
                  pl.BlockSpec((bb, 1, CONV_HALO, cw), lambda bi, ti: (bi, ti, 0, 0)),
                  _const_spec(wconv.shape), _const_spec((1, cw)), _const_spec((1, cw)),
                  _const_spec((1, cw)), _const_spec(woo.shape), _const_spec(woc.shape),
                  _const_spec((1, d)), _const_spec(wmq.shape), _const_spec((1, MEM_HEAD_DIM)),
                  pl.BlockSpec((bb, mem_rows, mhd), lambda bi, ti: (bi, 0, 0)),
                  pl.BlockSpec((bb, mem_rows, mhd), lambda bi, ti: (bi, 0, 0)),
                  _const_spec(wmo.shape)],
        out_specs=tile(d),
        out_shape=jax.ShapeDtypeStruct((b, t, d), F32),
        scratch_shapes=[pltpu.VMEM((bb, CONV_HALO + tt, cw), F32),
                        pltpu.VMEM((SUBLANES - 1, bb, CONV_HALO - SUBLANES + tt, cw), F32),
                        pltpu.VMEM((rows, cw), BF16),
                        pltpu.VMEM((rows, mw), BF16)],
        compiler_params=_cparams(("parallel", "parallel")),
        name="mix",
    )(x, o, c, hist, wconv, bconv, lng, lnb, woo, woc, gcross, wmq, gmq_row, mk, mv, wmo)


def _mlp_kernel(x_ref, g_ref, w1_ref, w2_ref, out_ref):
    x = x_ref[...]
    hf = _rms(x, g_ref[...]).astype(BF16)
    u = jnp.maximum(jnp.dot(hf, w1_ref[...], preferred_element_type=F32), 0.0)
    out_ref[...] = x + jnp.dot((u * u).astype(BF16), w2_ref[...], preferred_element_type=F32)


def _mlp(x2d, g_ffn, w1, w2):
    n, d = x2d.shape
    tm = min(ROW_TILE, n)
    return pl.pallas_call(
        _mlp_kernel,
        grid=(n // tm,),
        in_specs=[pl.BlockSpec((tm, d), lambda i: (i, 0)), _const_spec((1, d)),
                  _const_spec(w1.shape), _const_spec(w2.shape)],
        out_specs=pl.BlockSpec((tm, d), lambda i: (i, 0)),
        out_shape=jax.ShapeDtypeStruct((n, d), F32),
        compiler_params=_cparams(("parallel",)),
        name="mlp",
    )(x2d, g_ffn.reshape(1, d), w1, w2)


def _rel_bucket(rel):
    half = N_BUCKETS // 2
    max_exact = half // 2
    ret = jnp.where(rel > 0, half, 0)
    n = jnp.abs(rel)
    nf = jnp.maximum(n, 1).astype(jnp.float32)
    large = max_exact + (jnp.log(nf / max_exact) / math.log(MAX_DISTANCE / max_exact)
                         * (half - max_exact)).astype(jnp.int32)
    large = jnp.minimum(large, half - 1)
    return ret + jnp.where(n < max_exact, n, large)


def _masked_bias(rel_table, q_pos, k_pos):
    bucket = _rel_bucket(k_pos[None, :] - q_pos[:, None])[None]
    table = rel_table.astype(F32) * LOG2E
    bias = jnp.zeros((rel_table.shape[1],) + bucket.shape[1:], F32)
    for bkt in range(N_BUCKETS):
        bias = jnp.where(bucket == bkt, table[bkt][:, None, None], bias)
    mask = (k_pos[None, :] // CHUNK) <= (q_pos[:, None] // CHUNK)
    return jnp.where(mask[None], bias, NEG_INF)


def _far_bucket_is_saturated(min_distance):
    half = N_BUCKETS // 2
    max_exact = half // 2
    large = max_exact + int(np.log(min_distance / max_exact) / math.log(MAX_DISTANCE / max_exact)
                            * (half - max_exact) * (1 - 1e-6))
    return large >= half - 1


def _layer(x, k_past, v_past, c_past, mk, mv, w, lam, lam_init):
    b, t, d = x.shape
    prompt = k_past is None
    cols = N_HEADS * HEAD_COLS
    gsub_row = (w["g_sub"] * (1.0 - lam_init)).reshape(1, V_DIM)

    if prompt:
        assert t % ROW_TILE == 0 and ROW_TILE % ATTN_TQ == 0 and ATTN_TQ == 2 * ATTN_TK
        assert ATTN_TK % CHUNK == 0 and _far_bucket_is_saturated(ATTN_TK + 1)
        qt, kb, vt, kt, v4, c = _inproj(x, w["g_mix"], w["w_in"], w["gq_row"], w["gk_row"],
                                        w["seg"], emit_t=True)
        k_out = jnp.transpose(kt.reshape(b, N_HEADS, 2, HEAD_DIM, t), (0, 4, 1, 2, 3))
        v_out = v4.reshape(b, t, N_HEADS, V_DIM)
        q_pos = ATTN_TQ + jnp.arange(ATTN_TQ, dtype=jnp.int32)
        k_pos = ATTN_TK + jnp.arange(3 * ATTN_TK, dtype=jnp.int32)
        near = _masked_bias(w["rel_table"], q_pos, k_pos)
        near = jnp.stack([near[:, :, n * ATTN_TK:(n + 1) * ATTN_TK] for n in range(3)], axis=1)
        near_t = jnp.swapaxes(near, 2, 3)
        halves = [near_t[..., n * ATTN_TK:(n + 1) * ATTN_TK] for n in range(ATTN_TQ // ATTN_TK)]
        bias_t = jnp.concatenate([hf for hf in halves for _ in range(2)], axis=3)
        far = w["rel_table"][_rel_bucket(jnp.int32(-(ATTN_TK + 1)))].astype(F32) * LOG2E
        scalars = jnp.concatenate([far, lam.reshape(1), jnp.exp2(-far)]).astype(F32)
        score_bound = (HEAD_DIM * jnp.max(jnp.abs(w["gq_row"])) * jnp.max(jnp.abs(w["gk_row"]))
                       + LOG2E * jnp.max(jnp.abs(w["rel_table"])))
        o = _attn_prompt(scalars, score_bound, qt, kb, vt, bias_t, gsub_row)
        bb, tt = 1, ROW_TILE
    else:
        p = k_past.shape[2]
        assert p % CACHE_TK == 0 and t <= NEW_KEY_PAD and t % 16 == 0
        qb, k, v, c = _inproj(x.reshape(1, b * t, d), w["g_mix"], w["w_in"], w["gq_row"],
                              w["gk_row"], w["seg"], emit_t=False)
        qb, k, v, c = (a.reshape(b, t, cols) for a in (qb, k, v, c))
        k_out = k.reshape(b, t, N_HEADS, 2, HEAD_DIM)
        v_out = v.reshape(b, t, N_HEADS, V_DIM)
        ktn = jnp.pad(jnp.swapaxes(k, 1, 2).astype(BF16), ((0, 0), (0, 0), (0, NEW_KEY_PAD - t)))
        vn = jnp.pad(v.astype(BF16), ((0, 0), (0, NEW_KEY_PAD - t), (0, 0)))
        q_pos = p + jnp.arange(t, dtype=jnp.int32)
        bias = _masked_bias(w["rel_table"], q_pos, jnp.arange(p + NEW_KEY_PAD, dtype=jnp.int32))
        bias = jnp.where(jnp.arange(p + NEW_KEY_PAD) < p + t, bias, NEG_INF)
        rows = 2 * N_HEADS * t
        bias = jnp.broadcast_to(bias[:, None], (N_HEADS, 2, t, p + NEW_KEY_PAD)).reshape(rows, -1)
        bias_c = jnp.swapaxes(bias[:, :p].reshape(rows, p // CACHE_TK, CACHE_TK), 0, 1)
        scalars = jnp.concatenate([jnp.zeros((N_HEADS,), F32), lam.reshape(1)]).astype(F32)
        o = _attn_sample(scalars, qb, k_past, v_past, ktn, vn, bias_c, bias[:, p:], gsub_row)
        bb, tt = SAMPLE_BATCH_TILE, t
        assert b % bb == 0

    nt = t // tt
    first = jnp.pad(c_past, ((0, 0), (CONV_HALO - (CONV_K - 1), 0), (0, 0)))[:, None]
    if nt > 1:
        tails = c.reshape(b, nt, tt, c.shape[2])[:, :-1, tt - CONV_HALO:, :]
        hist = jnp.concatenate([first, tails], axis=1)
    else:
        hist = first
    x2 = _mix(x, o, c, hist, w["w_conv"], w["b_conv"], w["ln_g"], w["ln_b"], w["w_out_o"],
              w["w_out_c"], w["g_cross"], w["w_mq"], w["gmq_row"], mk, mv, w["w_mo"], bb=bb, tt=tt)
    y = _mlp(x2.reshape(b * t, d), w["g_ffn"], w["w_ff1"], w["w_ff2"]).reshape(b, t, d)
    if t >= CONV_K - 1:
        c_hist_tail = c[:, t - (CONV_K - 1):]
    else:
        c_hist_tail = jnp.concatenate([c_past[:, t:], c], axis=1)
    return y, k_out, v_out, c_hist_tail


def kernel(x_prompt, x_sample, cache_k, cache_v, cache_conv, cache_mem_k, cache_mem_v, mem_prompt,
           rel_table, g_mix, w_in, g_q, g_k, lam_vec, g_sub, w_conv, b_conv, ln_g, ln_b, w_out,
           g_cross, g_mem, w_mq, w_mk, w_mv, g_mq, g_mk, w_mo, g_ffn, w_ff1, w_ff2):
    depth = g_mix.shape[0]
    assert depth == 1
    b, t, d = x_prompt.shape
    bs, ts, _ = x_sample.shape
    cols = N_HEADS * HEAD_COLS
    cw = w_conv.shape[2]
    attn_w = N_HEADS * V_DIM
    l = 0
    lam_init = 0.8 - 0.6 * math.exp(-0.3 * l)
    lp = lam_vec[l].astype(F32)
    lam = jnp.exp(jnp.sum(lp[0] * lp[1])) - jnp.exp(jnp.sum(lp[2] * lp[3])) + lam_init

    seg = jnp.kron(jnp.eye(cols // HEAD_DIM, dtype=F32),
                   jnp.full((HEAD_DIM, HEAD_DIM), 1.0 / HEAD_DIM, F32)).astype(BF16)
    n_maps = cols // HEAD_DIM
    w = dict(
        rel_table=rel_table, g_mix=g_mix[l], w_in=w_in[l].astype(BF16), seg=seg,
        gq_row=jnp.tile(g_q[l] * (HEAD_DIM ** -0.5 * LOG2E), n_maps).reshape(1, cols),
        gk_row=jnp.tile(g_k[l], n_maps).reshape(1, cols),
        g_sub=g_sub[l],
        w_conv=jnp.pad(w_conv[l], ((0, CONV_HALO - CONV_K), (0, 0))),
        b_conv=b_conv[l].reshape(1, cw), ln_g=ln_g[l].reshape(1, cw), ln_b=ln_b[l].reshape(1, cw),
        w_out_o=w_out[l][:attn_w].astype(BF16), w_out_c=w_out[l][attn_w:].astype(BF16),
        g_cross=g_cross[l].reshape(1, d), w_mq=w_mq[l].astype(BF16),
        gmq_row=(g_mq[l] * (MEM_HEAD_DIM ** -0.5 * LOG2E)).reshape(1, MEM_HEAD_DIM),
        w_mo=w_mo[l].astype(BF16), g_ffn=g_ffn[l],
        w_ff1=w_ff1[l].astype(BF16), w_ff2=w_ff2[l].astype(BF16),
    )

    n_mem = mem_prompt.shape[1]
    mk_p, mv_p = _memkv(mem_prompt.reshape(b * n_mem, d), g_mem[l], w_mk[l], w_mv[l], g_mk[l])
    mk_p = mk_p.reshape(b, n_mem * MEM_HEADS, MEM_HEAD_DIM)
    mv_p = mv_p.reshape(b, n_mem * MEM_HEADS, MEM_HEAD_DIM)

    zero_conv = jnp.zeros((b, CONV_K - 1, cw), F32)
    yp, kp, vp, cp = _layer(x_prompt, None, None, zero_conv, mk_p, mv_p, w, lam, lam_init)

    p = cache_k.shape[2]
    k_past = jnp.transpose(cache_k[l], (0, 2, 3, 4, 1)).reshape(bs, cols, p)
    v_past = cache_v[l].reshape(bs, p * N_HEADS, V_DIM)
    ys, kn, vn, cn = _layer(x_sample, k_past, v_past, cache_conv[l],
                            cache_mem_k[l].reshape(bs, n_mem * MEM_HEADS, MEM_HEAD_DIM),
                            cache_mem_v[l].reshape(bs, n_mem * MEM_HEADS, MEM_HEAD_DIM),
                            w, lam, lam_init)

    return (yp, ys, kp[None], vp[None], cp[None],
            mk_p.reshape(1, b, n_mem, MEM_HEADS, MEM_HEAD_DIM),
            mv_p.reshape(1, b, n_mem, MEM_HEADS, MEM_HEAD_DIM),
            kn[None], vn[None], cn[None])
```

```python
import functools
import math

import jax
import jax.numpy as jnp
import numpy as np
from jax import lax
from jax.experimental import pallas as pl
from jax.experimental.pallas import tpu as pltpu

F32 = jnp.float32
BF16 = jnp.bfloat16

CHUNK = 64
N_HEADS = 4
HEAD_DIM = 64
V_DIM = 2 * HEAD_DIM
HEAD_COLS = 2 * HEAD_DIM
CONV_K = 31
CONV_HALO = 32
N_BUCKETS = 32
MAX_DISTANCE = 128
MEM_HEADS = 4
MEM_HEAD_DIM = 128
EPS = 1e-6
NEG_INF = -1e30
LOG2E = math.log2(math.e)
SUBLANES = 8

ATTN_TQ = 512
ATTN_TK = 256
VT_ROWS = V_DIM + 16
MAX_UNSTABILISED_SCORE = 100.0
CACHE_TK = 4096
NEW_KEY_PAD = 128
ROW_TILE = 512
INPROJ_TILE = 1024
CONV_ROWS = 64
SAMPLE_BATCH_TILE = 8
VMEM_LIMIT = 56 * 1024 * 1024


def _cparams(sem):
    return pltpu.CompilerParams(dimension_semantics=sem, vmem_limit_bytes=VMEM_LIMIT)


def _rms(x, g):
    ms = jnp.mean(x * x, axis=-1, keepdims=True)
    return x * lax.rsqrt(ms + EPS) * g


def _const_spec(shape):
    return pl.BlockSpec(shape, lambda *_: (0,) * len(shape), pipeline_mode=pl.Buffered(1))


def _memkv_kernel(mem_ref, g_ref, wk_ref, wv_ref, gk_ref, mk_ref, mv_ref):
    m = _rms(mem_ref[...], g_ref[...]).astype(BF16)
    zk = jnp.dot(m, wk_ref[...], preferred_element_type=F32)
    zv = jnp.dot(m, wv_ref[...], preferred_element_type=F32)
    tm = mem_ref.shape[0]
    for h in range(MEM_HEADS):
        sl = slice(h * MEM_HEAD_DIM, (h + 1) * MEM_HEAD_DIM)
        rows = pl.ds(h, tm, stride=MEM_HEADS)
        mk_ref[rows, :] = _rms(zk[:, sl], gk_ref[...])
        mv_ref[rows, :] = zv[:, sl]


def _memkv(mem2d, g_mem, w_mk, w_mv, g_mk):
    n, d = mem2d.shape
    w = w_mk.shape[1]
    tm = min(ROW_TILE, n)
    return pl.pallas_call(
        _memkv_kernel,
        grid=(n // tm,),
        in_specs=[pl.BlockSpec((tm, d), lambda i: (i, 0)), _const_spec((1, d)),
                  _const_spec((d, w)), _const_spec((d, w)), _const_spec((1, MEM_HEAD_DIM))],
        out_specs=[pl.BlockSpec((tm * MEM_HEADS, MEM_HEAD_DIM), lambda i: (i, 0))] * 2,
        out_shape=[jax.ShapeDtypeStruct((n * MEM_HEADS, MEM_HEAD_DIM), F32)] * 2,
        compiler_params=_cparams(("parallel",)),
        name="memkv",
    )(mem2d, g_mem.reshape(1, d), w_mk.astype(BF16), w_mv.astype(BF16),
      g_mk.reshape(1, MEM_HEAD_DIM))


def _inproj_kernel(x_ref, g_ref, w_ref, gq_ref, gk_ref, *refs, qk_cols, emit_t):
    if emit_t:
        qt_ref, kb_ref, vt_ref, kt_ref, v4_ref, c_ref = refs
    else:
        seg_ref, qb_ref, k_ref, v_ref, c_ref = refs

    def mapnorm(z, g):
        ms = jnp.dot((z * z).astype(BF16), seg_ref[...], preferred_element_type=F32)
        return z * lax.rsqrt(ms + EPS) * g

    def mapnorm_t(z, g_t):
        zt = z.T
        z3 = zt.reshape(zt.shape[0] // HEAD_DIM, HEAD_DIM, zt.shape[1])
        ms = jnp.mean(z3 * z3, axis=1, keepdims=True)
        return (z3 * lax.rsqrt(ms + EPS)).reshape(zt.shape) * g_t

    h = _rms(x_ref[0], g_ref[...]).astype(BF16)

    def proj(lo, hi):
        return jnp.dot(h, w_ref[:, lo:hi], preferred_element_type=F32)

    c0 = qk_cols
    zq = proj(0, c0)
    zk = proj(c0, 2 * c0)
    v = proj(2 * c0, 3 * c0)
    a = proj(3 * c0, 4 * c0)
    gate = proj(4 * c0, 5 * c0)
    c_ref[0] = a * jax.nn.sigmoid(gate)
    if emit_t:
        kt = mapnorm_t(zk, gk_ref[...])
        kt_ref[0] = kt
        kb_ref[0] = kt.T.astype(BF16)
        tm = x_ref.shape[1]
        for hh in range(N_HEADS):
            v4_ref[0, pl.ds(hh, tm, stride=N_HEADS), :] = v[:, hh * V_DIM:(hh + 1) * V_DIM]
        qt = mapnorm_t(zq, gq_ref[...]).astype(BF16)
        vt = v.T.astype(BF16)
        for j in range(qt_ref.shape[1]):
            qt_ref[0, j] = qt[:, j * ATTN_TQ:(j + 1) * ATTN_TQ]
        ones_row = (lax.broadcasted_iota(jnp.int32, (VT_ROWS - V_DIM, ATTN_TK), 0) == 0).astype(BF16)
        for j in range(vt_ref.shape[1]):
            for hh in range(N_HEADS):
                vt_ref[0, j, hh, :V_DIM, :] = vt[hh * V_DIM:(hh + 1) * V_DIM,
                                                 j * ATTN_TK:(j + 1) * ATTN_TK]
                vt_ref[0, j, hh, V_DIM:, :] = ones_row
    else:
        k_ref[0] = mapnorm(zk, gk_ref[...])
        v_ref[0] = v
        qb_ref[0] = mapnorm(zq, gq_ref[...]).astype(BF16)


def _inproj(x, g_mix, w_in_bf, gq_row, gk_row, seg, *, emit_t):
    b, t, d = x.shape
    cols = gq_row.shape[1]
    tm = min(INPROJ_TILE if emit_t else ROW_TILE, t)
    nt = t // tm
    row = lambda width: pl.BlockSpec((1, tm, width), lambda bi, ti: (bi, ti, 0))
    f32_out = jax.ShapeDtypeStruct((b, t, cols), F32)
    if emit_t:
        nq, nk = tm // ATTN_TQ, tm // ATTN_TK
        out_shape = [jax.ShapeDtypeStruct((b, t // ATTN_TQ, cols, ATTN_TQ), BF16),
                     jax.ShapeDtypeStruct((b, t, cols), BF16),
                     jax.ShapeDtypeStruct((b, t // ATTN_TK, N_HEADS, VT_ROWS, ATTN_TK), BF16),
                     jax.ShapeDtypeStruct((b, cols, t), F32),
                     jax.ShapeDtypeStruct((b, t * N_HEADS, V_DIM), F32),
                     f32_out]
        out_specs = [pl.BlockSpec((1, nq, cols, ATTN_TQ), lambda bi, ti: (bi, ti, 0, 0)),
                     row(cols),
                     pl.BlockSpec((1, nk, N_HEADS, VT_ROWS, ATTN_TK),
                                  lambda bi, ti: (bi, ti, 0, 0, 0)),
                     pl.BlockSpec((1, cols, tm), lambda bi, ti: (bi, 0, ti)),
                     pl.BlockSpec((1, tm * N_HEADS, V_DIM), lambda bi, ti: (bi, ti, 0)),
                     row(cols)]
        gains = [jnp.broadcast_to(g.reshape(cols, 1), (cols, tm)) for g in (gq_row, gk_row)]
        gain_specs = [_const_spec((cols, tm))] * 2
    else:
        out_shape = [jax.ShapeDtypeStruct((b, t, cols), BF16)] + [f32_out] * 3
        out_specs = [row(cols)] * 4
        gains = [gq_row, gk_row, seg]
        gain_specs = [_const_spec((1, cols)), _const_spec((1, cols)), _const_spec((cols, cols))]
    return pl.pallas_call(
        functools.partial(_inproj_kernel, qk_cols=cols, emit_t=emit_t),
        grid=(b, nt),
        in_specs=[row(d), _const_spec((1, d)), _const_spec(w_in_bf.shape)] + gain_specs,
        out_specs=out_specs,
        out_shape=out_shape,
        compiler_params=_cparams(("parallel", "parallel")),
        name="inproj",
    )(x, g_mix.reshape(1, d), w_in_bf, *gains)


PREV, DIAG0, DIAG1, FAR = 0, 1, 2, None


def _build_qbd(qt_ref, qbd_ref):
    w = ATTN_TK
    upper = lax.broadcasted_iota(jnp.int32, (HEAD_COLS, w), 0) < HEAD_DIM
    for h in range(N_HEADS):
        for half in range(ATTN_TQ // w):
            qh = qt_ref[0, 0, h * HEAD_COLS:(h + 1) * HEAD_COLS, half * w:(half + 1) * w]
            zero = jnp.zeros_like(qh)
            qbd_ref[h, :, 2 * half * w:(2 * half + 1) * w] = jnp.where(upper, qh, zero)
            qbd_ref[h, :, (2 * half + 1) * w:(2 * half + 2) * w] = jnp.where(upper, zero, qh)


def _attn_finalize(acc_ref, denom, lam, gsub_ref, o_ref):
    w = ATTN_TK
    for h in range(N_HEADS):
        acc = acc_ref[h, :V_DIM, :] * (1.0 / denom(h))
        for half in range(ATTN_TQ // w):
            m0 = acc[:, 2 * half * w:(2 * half + 1) * w]
            m1 = acc[:, (2 * half + 1) * w:(2 * half + 2) * w]
            ot = m0 - lam * m1
            ms = jnp.mean(ot * ot, axis=0, keepdims=True)
            ot = ot * lax.rsqrt(ms + EPS)
            o_ref[0, half * w:(half + 1) * w, h * V_DIM:(h + 1) * V_DIM] = (
                ot.T * gsub_ref[...]).astype(BF16)


def _attn_prompt_kernel(sc_ref, qt_ref, kb_ref, vt_ref, bias_ref, gsub_ref, o_ref,
                        qbd_ref, acc_ref, sa_ref, sb_ref, *stat_refs, bounded):
    i = pl.program_id(1)
    tq, tk = ATTN_TQ, ATTN_TK
    lam = sc_ref[N_HEADS]
    if bounded:
        lsum_ref, = stat_refs
        slot_a, slot_b = (sa_ref, None), (sb_ref, None)
        lsum_ref[...] = jnp.zeros(lsum_ref.shape, F32)
    else:
        m_ref, mca_ref, mcb_ref = stat_refs
        slot_a, slot_b = (sa_ref, mca_ref), (sb_ref, mcb_ref)
        m_ref[...] = jnp.full(m_ref.shape, NEG_INF, F32)

    _build_qbd(qt_ref, qbd_ref)
    acc_ref[...] = jnp.zeros(acc_ref.shape, F32)

    every = slice(0, 2 * tq)

    def stage(j, slot, tile, cols=every):
        s_ref, mc_ref = slot
        row0 = pl.multiple_of(j * tk, tk)
        width = cols.stop - cols.start
        for h in range(N_HEADS):
            kh = kb_ref[0, pl.ds(row0, tk), h * HEAD_COLS:(h + 1) * HEAD_COLS]
            s = jnp.dot(kh, qbd_ref[h, :, cols], preferred_element_type=F32)
            if tile is not FAR:
                s = s + bias_ref[h, tile, :, cols]
            if bounded:
                p = jnp.exp2(s)
                s_ref[h, :, cols] = p.astype(BF16)
                lsum_ref[h, :, cols] += jnp.sum(p.reshape(tk // SUBLANES, SUBLANES, width), axis=0)
            else:
                s_ref[h, :, cols] = s
                mc_ref[h, :, cols] = jnp.max(s, axis=0, keepdims=True)

    def consume(j, slot, tile, cols=every):
        s_ref, mc_ref = slot
        for h in range(N_HEADS):
            if bounded:
                acc_ref[h, :V_DIM, cols] += jnp.dot(vt_ref[0, j, h, :V_DIM, :], s_ref[h, :, cols],
                                                    preferred_element_type=F32)
                continue
            m_prev = m_ref[h, :, cols]
            if tile is FAR:
                shift = sc_ref[h]
                m_new = jnp.maximum(m_prev, mc_ref[h, :, cols] + shift)
                p = jnp.exp2(s_ref[h, :, cols] - (m_new - shift))
            else:
                m_new = jnp.maximum(m_prev, mc_ref[h, :, cols])
                p = jnp.exp2(s_ref[h, :, cols] - m_new)
            alpha = jnp.exp2(m_prev - m_new)
            pv = jnp.dot(vt_ref[0, j, h], p.astype(BF16), preferred_element_type=F32)
            acc_ref[h, :, cols] = alpha * acc_ref[h, :, cols] + pv
            m_ref[h, :, cols] = m_new

    second_half = slice(tq, 2 * tq)
    stage(2 * i + 1, slot_a, DIAG1, second_half)
    consume(2 * i + 1, slot_a, DIAG1, second_half)
    stage(2 * i, slot_b, DIAG0)

    @pl.when(i == 0)
    def _():
        consume(2 * i, slot_b, DIAG0)

    @pl.when(i >= 1)
    def _():
        consume(2 * i, slot_b, DIAG0)
        stage(2 * i - 1, slot_a, PREV)
        consume(2 * i - 1, slot_a, PREV)
        if bounded:
            for h in range(N_HEADS):
                acc_ref[h] = acc_ref[h] * sc_ref[N_HEADS + 1 + h]
                lsum_ref[h] = lsum_ref[h] * sc_ref[N_HEADS + 1 + h]
        stage(0, slot_b, FAR)

    def pair_body(jj, carry):
        f = 2 * jj
        consume(f, slot_b, FAR)
        stage(f + 1, slot_a, FAR)
        consume(f + 1, slot_a, FAR)
        stage(f + 2, slot_b, FAR)
        return carry

    lax.fori_loop(0, jnp.maximum(i - 1, 0), pair_body, 0)

    @pl.when(i >= 1)
    def _():
        consume(2 * i - 2, slot_b, FAR)

    if bounded:
        denom = lambda h: jnp.sum(lsum_ref[h], axis=0, keepdims=True)
    else:
        denom = lambda h: acc_ref[h, V_DIM:V_DIM + 1, :]
    _attn_finalize(acc_ref, denom, lam, gsub_ref, o_ref)


def _attn_prompt(scalars, score_bound, qt, kb, vt, bias_t, gsub_row):
    b, nq, cols, tq = qt.shape
    t = kb.shape[1]
    nk = vt.shape[1]
    qbd = pltpu.VMEM((N_HEADS, HEAD_COLS, 2 * tq), BF16)
    acc = pltpu.VMEM((N_HEADS, VT_ROWS, 2 * tq), F32)
    stat = pltpu.VMEM((N_HEADS, 1, 2 * tq), F32)
    scores = pltpu.VMEM((N_HEADS, ATTN_TK, 2 * tq), F32)
    probs = pltpu.VMEM((N_HEADS, ATTN_TK, 2 * tq), BF16)
    colsum = pltpu.VMEM((N_HEADS, SUBLANES, 2 * tq), F32)

    def call(bounded, scratch, name):
        return pl.pallas_call(
            functools.partial(_attn_prompt_kernel, bounded=bounded),
            grid=(b, nq),
            in_specs=[pl.BlockSpec(memory_space=pltpu.SMEM),
                      pl.BlockSpec((1, 1, cols, tq), lambda bi, i: (bi, i, 0, 0)),
                      pl.BlockSpec((1, t, cols), lambda bi, i: (bi, 0, 0)),
                      pl.BlockSpec((1, nk, N_HEADS, VT_ROWS, ATTN_TK),
                                   lambda bi, i: (bi, 0, 0, 0, 0)),
                      _const_spec(bias_t.shape), _const_spec((1, V_DIM))],
            out_specs=pl.BlockSpec((1, tq, cols), lambda bi, i: (bi, i, 0)),
            out_shape=jax.ShapeDtypeStruct((b, t, cols), BF16),
            scratch_shapes=scratch,
            compiler_params=_cparams(("parallel", "arbitrary")),
            name=name,
        )(scalars, qt, kb, vt, bias_t, gsub_row)

    return lax.cond(
        score_bound <= MAX_UNSTABILISED_SCORE,
        lambda: call(True, [qbd, acc, probs, probs, colsum], "attn_prompt_bounded"),
        lambda: call(False, [qbd, acc, scores, scores, stat, stat, stat], "attn_prompt"))


def _attn_sample_kernel(sc_ref, q_ref, kt_ref, vc_ref, ktn_ref, vn_ref, bc_ref, bn_ref, gsub_ref,
                        o_ref, qbd_ref, m_ref, l_ref, acc_ref):
    j = pl.program_id(1)
    tq, cols = q_ref.shape[1], q_ref.shape[2]
    lam = sc_ref[N_HEADS]

    @pl.when(j == 0)
    def _():
        q = q_ref[0].astype(F32)
        col = lax.broadcasted_iota(jnp.int32, (tq, cols), 1)
        for hm in range(cols // HEAD_DIM):
            mine = (col >= hm * HEAD_DIM) & (col < (hm + 1) * HEAD_DIM)
            qbd_ref[hm * tq:(hm + 1) * tq, :] = jnp.where(mine, q, 0.0)
        m_ref[...] = jnp.full(m_ref.shape, NEG_INF, F32)
        l_ref[...] = jnp.zeros(l_ref.shape, F32)
        acc_ref[...] = jnp.zeros(acc_ref.shape, F32)

    def update(kt, v_of_head, bias):
        s = jnp.dot(qbd_ref[...].astype(BF16), kt, preferred_element_type=F32) + bias
        m_prev = m_ref[...]
        m_new = jnp.maximum(m_prev, jnp.max(s, axis=-1, keepdims=True))
        alpha = jnp.exp2(m_prev - m_new)
        p = jnp.exp2(s - m_new)
        l_ref[...] = alpha * l_ref[...] + jnp.sum(p, axis=-1, keepdims=True)
        pb = p.astype(BF16)
        for h in range(N_HEADS):
            rows = slice(h * 2 * tq, (h + 1) * 2 * tq)
            acc_ref[rows, :] = alpha[rows] * acc_ref[rows, :] + jnp.dot(
                pb[rows], v_of_head(h), preferred_element_type=F32)
        m_ref[...] = m_new

    tk = kt_ref.shape[2]
    update(kt_ref[0].astype(BF16),
           lambda h: vc_ref[0, pl.ds(h, tk, stride=N_HEADS), :].astype(BF16), bc_ref[j])

    @pl.when(j == pl.num_programs(1) - 1)
    def _():
        update(ktn_ref[0], lambda h: vn_ref[0, :, h * V_DIM:(h + 1) * V_DIM], bn_ref[...])
        for h in range(N_HEADS):
            r0 = h * 2 * tq
            inv0 = 1.0 / l_ref[r0:r0 + tq, :]
            inv1 = 1.0 / l_ref[r0 + tq:r0 + 2 * tq, :]
            o = acc_ref[r0:r0 + tq, :] * inv0 - lam * (acc_ref[r0 + tq:r0 + 2 * tq, :] * inv1)
            o_ref[0, :, h * V_DIM:(h + 1) * V_DIM] = _rms(o, gsub_ref[...]).astype(BF16)


def _attn_sample(scalars, q, kt, v4, ktn, vn, bias_c, bias_n, gsub_row):
    b, tq, cols = q.shape
    p = kt.shape[2]
    nkc = p // CACHE_TK
    rows = 2 * N_HEADS * tq
    return pl.pallas_call(
        _attn_sample_kernel,
        grid=(b, nkc),
        in_specs=[pl.BlockSpec(memory_space=pltpu.SMEM),
                  pl.BlockSpec((1, tq, cols), lambda bi, j: (bi, 0, 0)),
                  pl.BlockSpec((1, cols, CACHE_TK), lambda bi, j: (bi, 0, j)),
                  pl.BlockSpec((1, CACHE_TK * N_HEADS, V_DIM), lambda bi, j: (bi, j, 0)),
                  pl.BlockSpec((1, cols, NEW_KEY_PAD), lambda bi, j: (bi, 0, 0)),
                  pl.BlockSpec((1, NEW_KEY_PAD, cols), lambda bi, j: (bi, 0, 0)),
                  _const_spec(bias_c.shape), _const_spec(bias_n.shape), _const_spec((1, V_DIM))],
        out_specs=pl.BlockSpec((1, tq, cols), lambda bi, j: (bi, 0, 0)),
        out_shape=jax.ShapeDtypeStruct((b, tq, cols), BF16),
        scratch_shapes=[pltpu.VMEM((rows, cols), F32),
                        pltpu.VMEM((rows, 1), F32),
                        pltpu.VMEM((rows, 1), F32),
                        pltpu.VMEM((rows, V_DIM), F32)],
        compiler_params=_cparams(("parallel", "arbitrary")),
        name="attn_sample",
    )(scalars, q, kt, v4, ktn, vn, bias_c, bias_n, gsub_row)


def _mix_kernel(x_ref, o_ref, c_ref, hist_ref, wconv_ref, bconv_ref, lng_ref, lnb_ref,
                woo_ref, woc_ref, gcross_ref, wmq_ref, gmq_ref, mk_ref, mv_ref, wmo_ref,
                out_ref, ext_ref, xs_ref, cv_ref, ca_ref):
    bb, tt, d = x_ref.shape
    rows = bb * tt
    n_mem = mk_ref.shape[1] // MEM_HEADS
    rc = min(CONV_ROWS, tt)
    lead = CONV_HALO - (CONV_K - 1)

    for b in range(bb):
        ext_ref[b, :CONV_HALO, :] = hist_ref[b, 0]
        ext_ref[b, CONV_HALO:, :] = c_ref[b]
    span = xs_ref.shape[2]
    for b in range(bb):
        for r in range(1, SUBLANES):
            xs_ref[r - 1, b] = ext_ref[b, r:r + span, :]
    for b in range(bb):
        for r0 in range(0, tt, rc):
            acc = jnp.zeros((rc, c_ref.shape[2]), F32) + bconv_ref[...]
            for k in range(CONV_K):
                a, r = divmod(k + lead, SUBLANES)
                lo = r0 + SUBLANES * a
                src = ext_ref[b, lo:lo + rc, :] if r == 0 else xs_ref[r - 1, b, lo:lo + rc, :]
                acc = acc + wconv_ref[k:k + 1, :] * src
            mu = jnp.mean(acc, axis=-1, keepdims=True)
            xc = acc - mu
            var = jnp.mean(xc * xc, axis=-1, keepdims=True)
            y = xc * lax.rsqrt(var + EPS) * lng_ref[...] + lnb_ref[...]
            cv_ref[b * tt + r0:b * tt + r0 + rc, :] = (y * jax.nn.sigmoid(y)).astype(BF16)

    x = x_ref[...].reshape(rows, d)
    o = o_ref[...].reshape(rows, o_ref.shape[2])
    x1 = (x + jnp.dot(o, woo_ref[...], preferred_element_type=F32)
          + jnp.dot(cv_ref[...], woc_ref[...], preferred_element_type=F32))

    hc = _rms(x1, gcross_ref[...]).astype(BF16)
    qm = jnp.dot(hc, wmq_ref[...], preferred_element_type=F32)
    for h in range(MEM_HEADS):
        cols = slice(h * MEM_HEAD_DIM, (h + 1) * MEM_HEAD_DIM)
        qn = _rms(qm[:, cols], gmq_ref[...]).astype(BF16)
        for b in range(bb):
            mem_rows = pl.ds(h, n_mem, stride=MEM_HEADS)
            mk = mk_ref[b, mem_rows, :].astype(BF16)
            mv = mv_ref[b, mem_rows, :].astype(BF16)
            s = lax.dot_general(qn[b * tt:(b + 1) * tt], mk, (((1,), (1,)), ((), ())),
                                preferred_element_type=F32)
            p = jnp.exp2(s - jnp.max(s, axis=-1, keepdims=True))
            l = jnp.sum(p, axis=-1, keepdims=True)
            oh = jnp.dot(p.astype(BF16), mv, preferred_element_type=F32) / l
            ca_ref[b * tt:(b + 1) * tt, cols] = oh.astype(BF16)
    x2 = x1 + jnp.dot(ca_ref[...], wmo_ref[...], preferred_element_type=F32)
    out_ref[...] = x2.reshape(bb, tt, d)


def _mix(x, o, c, hist, wconv, bconv, lng, lnb, woo, woc, gcross, wmq, gmq_row, mk, mv, wmo,
         *, bb, tt):
    b, t, d = x.shape
    cw = c.shape[2]
    mem_rows, mhd = mk.shape[1], mk.shape[2]
    mw = wmq.shape[1]
    rows = bb * tt
    tile = lambda width: pl.BlockSpec((bb, tt, width), lambda bi, ti: (bi, ti, 0))
    return pl.pallas_call(
        _mix_kernel,
        grid=(b // bb, t // tt),
        in_specs=[tile(d), tile(cw), tile(cw),
                  pl.BlockSpec((bb, 1, CONV_HALO, cw), lambda bi, ti: (bi, ti, 0, 0)),
                  _const_spec(wconv.shape), _const_spec((1, cw)), _const_spec((1, cw)),
                  _const_spec((1, cw)), _const_spec(woo.shape), _const_spec(woc.shape),
                  _const_spec((1, d)), _const_spec(wmq.shape), _const_spec((1, MEM_HEAD_DIM)),
                  pl.BlockSpec((bb, mem_rows, mhd), lambda bi, ti: (bi, 0, 0)),
                  pl.BlockSpec((bb, mem_rows, mhd), lambda bi, ti: (bi, 0, 0)),
                  _const_spec(wmo.shape)],
        out_specs=tile(d),
        out_shape=jax.ShapeDtypeStruct((b, t, d), F32),
        scratch_shapes=[pltpu.VMEM((bb, CONV_HALO + tt, cw), F32),
                        pltpu.VMEM((SUBLANES - 1, bb, CONV_HALO - SUBLANES + tt, cw), F32),
                        pltpu.VMEM((rows, cw), BF16),
                        pltpu.VMEM((rows, mw), BF16)],
        compiler_params=_cparams(("parallel", "parallel")),
        name="mix",
    )(x, o, c, hist, wconv, bconv, lng, lnb, woo, woc, gcross, wmq, gmq_row, mk, mv, wmo)


def _mlp_kernel(x_ref, g_ref, w1_ref, w2_ref, out_ref):
    x = x_ref[...]
    hf = _rms(x, g_ref[...]).astype(BF16)
    u = jnp.maximum(jnp.dot(hf, w1_ref[...], preferred_element_type=F32), 0.0)
    out_ref[...] = x + jnp.dot((u * u).astype(BF16), w2_ref[...], preferred_element_type=F32)


def _mlp(x2d, g_ffn, w1, w2):
    n, d = x2d.shape
    tm = min(ROW_TILE, n)
    return pl.pallas_call(
        _mlp_kernel,
        grid=(n // tm,),
        in_specs=[pl.BlockSpec((tm, d), lambda i: (i, 0)), _const_spec((1, d)),
                  _const_spec(w1.shape), _const_spec(w2.shape)],
        out_specs=pl.BlockSpec((tm, d), lambda i: (i, 0)),
        out_shape=jax.ShapeDtypeStruct((n, d), F32),
        compiler_params=_cparams(("parallel",)),
        name="mlp",
    )(x2d, g_ffn.reshape(1, d), w1, w2)


def _rel_bucket(rel):
    half = N_BUCKETS // 2
    max_exact = half // 2
    ret = jnp.where(rel > 0, half, 0)
    n = jnp.abs(rel)
    nf = jnp.maximum(n, 1).astype(jnp.float32)
    large = max_exact + (jnp.log(nf / max_exact) / math.log(MAX_DISTANCE / max_exact)
                         * (half - max_exact)).astype(jnp.int32)
    large = jnp.minimum(large, half - 1)
    return ret + jnp.where(n < max_exact, n, large)


def _masked_bias(rel_table, q_pos, k_pos):
    bucket = _rel_bucket(k_pos[None, :] - q_pos[:, None])[None]
    table = rel_table.astype(F32) * LOG2E
    bias = jnp.zeros((rel_table.shape[1],) + bucket.shape[1:], F32)
    for bkt in range(N_BUCKETS):
        bias = jnp.where(bucket == bkt, table[bkt][:, None, None], bias)
    mask = (k_pos[None, :] // CHUNK) <= (q_pos[:, None] // CHUNK)
    return jnp.where(mask[None], bias, NEG_INF)


def _far_bucket_is_saturated(min_distance):
    half = N_BUCKETS // 2
    max_exact = half // 2
    large = max_exact + int(np.log(min_distance / max_exact) / math.log(MAX_DISTANCE / max_exact)
                            * (half - max_exact) * (1 - 1e-6))
    return large >= half - 1


def _layer(x, k_past, v_past, c_past, mk, mv, w, lam, lam_init):
    b, t, d = x.shape
    prompt = k_past is None
    cols = N_HEADS * HEAD_COLS
    gsub_row = (w["g_sub"] * (1.0 - lam_init)).reshape(1, V_DIM)

    if prompt:
        assert t % ROW_TILE == 0 and t % INPROJ_TILE == 0 and INPROJ_TILE % ATTN_TQ == 0
        assert ROW_TILE % ATTN_TQ == 0 and ATTN_TQ == 2 * ATTN_TK
        assert ATTN_TK % CHUNK == 0 and _far_bucket_is_saturated(ATTN_TK + 1)
        qt, kb, vt, kt, v4, c = _inproj(x, w["g_mix"], w["w_in"], w["gq_row"], w["gk_row"],
                                        w["seg"], emit_t=True)
        k_out = jnp.transpose(kt.reshape(b, N_HEADS, 2, HEAD_DIM, t), (0, 4, 1, 2, 3))
        v_out = v4.reshape(b, t, N_HEADS, V_DIM)
        q_pos = ATTN_TQ + jnp.arange(ATTN_TQ, dtype=jnp.int32)
        k_pos = ATTN_TK + jnp.arange(3 * ATTN_TK, dtype=jnp.int32)
        near = _masked_bias(w["rel_table"], q_pos, k_pos)
        near = jnp.stack([near[:, :, n * ATTN_TK:(n + 1) * ATTN_TK] for n in range(3)], axis=1)
        near_t = jnp.swapaxes(near, 2, 3)
        halves = [near_t[..., n * ATTN_TK:(n + 1) * ATTN_TK] for n in range(ATTN_TQ // ATTN_TK)]
        bias_t = jnp.concatenate([hf for hf in halves for _ in range(2)], axis=3)
        far = w["rel_table"][_rel_bucket(jnp.int32(-(ATTN_TK + 1)))].astype(F32) * LOG2E
        scalars = jnp.concatenate([far, lam.reshape(1), jnp.exp2(-far)]).astype(F32)
        score_bound = (HEAD_DIM * jnp.max(jnp.abs(w["gq_row"])) * jnp.max(jnp.abs(w["gk_row"]))
                       + LOG2E * jnp.max(jnp.abs(w["rel_table"])))
        o = _attn_prompt(scalars, score_bound, qt, kb, vt, bias_t, gsub_row)
        bb, tt = 1, ROW_TILE
    else:
        p = k_past.shape[2]
        assert p % CACHE_TK == 0 and t <= NEW_KEY_PAD and t % 16 == 0
        qb, k, v, c = _inproj(x.reshape(1, b * t, d), w["g_mix"], w["w_in"], w["gq_row"],
                              w["gk_row"], w["seg"], emit_t=False)
        qb, k, v, c = (a.reshape(b, t, cols) for a in (qb, k, v, c))
        k_out = k.reshape(b, t, N_HEADS, 2, HEAD_DIM)
        v_out = v.reshape(b, t, N_HEADS, V_DIM)
        ktn = jnp.pad(jnp.swapaxes(k, 1, 2).astype(BF16), ((0, 0), (0, 0), (0, NEW_KEY_PAD - t)))
        vn = jnp.pad(v.astype(BF16), ((0, 0), (0, NEW_KEY_PAD - t), (0, 0)))
        q_pos = p + jnp.arange(t, dtype=jnp.int32)
        bias = _masked_bias(w["rel_table"], q_pos, jnp.arange(p + NEW_KEY_PAD, dtype=jnp.int32))
        bias = jnp.where(jnp.arange(p + NEW_KEY_PAD) < p + t, bias, NEG_INF)
        rows = 2 * N_HEADS * t
        bias = jnp.broadcast_to(bias[:, None], (N_HEADS, 2, t, p + NEW_KEY_PAD)).reshape(rows, -1)
        bias_c = jnp.swapaxes(bias[:, :p].reshape(rows, p // CACHE_TK, CACHE_TK), 0, 1)
        scalars = jnp.concatenate([jnp.zeros((N_HEADS,), F32), lam.reshape(1)]).astype(F32)
        o = _attn_sample(scalars, qb, k_past, v_past, ktn, vn, bias_c, bias[:, p:], gsub_row)
        bb, tt = SAMPLE_BATCH_TILE, t
        assert b % bb == 0

    nt = t // tt
    first = jnp.pad(c_past, ((0, 0), (CONV_HALO - (CONV_K - 1), 0), (0, 0)))[:, None]
    if nt > 1:
        tails = c.reshape(b, nt, tt, c.shape[2])[:, :-1, tt - CONV_HALO:, :]
        hist = jnp.concatenate([first, tails], axis=1)
    else:
        hist = first
    x2 = _mix(x, o, c, hist, w["w_conv"], w["b_conv"], w["ln_g"], w["ln_b"], w["w_out_o"],
              w["w_out_c"], w["g_cross"], w["w_mq"], w["gmq_row"], mk, mv, w["w_mo"], bb=bb, tt=tt)
    y = _mlp(x2.reshape(b * t, d), w["g_ffn"], w["w_ff1"], w["w_ff2"]).reshape(b, t, d)
    if t >= CONV_K - 1:
        c_hist_tail = c[:, t - (CONV_K - 1):]
    else:
        c_hist_tail = jnp.concatenate([c_past[:, t:], c], axis=1)
    return y, k_out, v_out, c_hist_tail


def kernel(x_prompt, x_sample, cache_k, cache_v, cache_conv, cache_mem_k, cache_mem_v, mem_prompt,
           rel_table, g_mix, w_in, g_q, g_k, lam_vec, g_sub, w_conv, b_conv, ln_g, ln_b, w_out,
           g_cross, g_mem, w_mq, w_mk, w_mv, g_mq, g_mk, w_mo, g_ffn, w_ff1, w_ff2):
    depth = g_mix.shape[0]
    assert depth == 1
    b, t, d = x_prompt.shape
    bs, ts, _ = x_sample.shape
    cols = N_HEADS * HEAD_COLS
    cw = w_conv.shape[2]
    attn_w = N_HEADS * V_DIM
    l = 0
    lam_init = 0.8 - 0.6 * math.exp(-0.3 * l)
    lp = lam_vec[l].astype(F32)
    lam = jnp.exp(jnp.sum(lp[0] * lp[1])) - jnp.exp(jnp.sum(lp[2] * lp[3])) + lam_init

    seg = jnp.kron(jnp.eye(cols // HEAD_DIM, dtype=F32),
                   jnp.full((HEAD_DIM, HEAD_DIM), 1.0 / HEAD_DIM, F32)).astype(BF16)
    n_maps = cols // HEAD_DIM
    w = dict(
        rel_table=rel_table, g_mix=g_mix[l], w_in=w_in[l].astype(BF16), seg=seg,
        gq_row=jnp.tile(g_q[l] * (HEAD_DIM ** -0.5 * LOG2E), n_maps).reshape(1, cols),
        gk_row=jnp.tile(g_k[l], n_maps).reshape(1, cols),
        g_sub=g_sub[l],
        w_conv=jnp.pad(w_conv[l], ((0, CONV_HALO - CONV_K), (0, 0))),
        b_conv=b_conv[l].reshape(1, cw), ln_g=ln_g[l].reshape(1, cw), ln_b=ln_b[l].reshape(1, cw),
        w_out_o=w_out[l][:attn_w].astype(BF16), w_out_c=w_out[l][attn_w:].astype(BF16),
        g_cross=g_cross[l].reshape(1, d), w_mq=w_mq[l].astype(BF16),
        gmq_row=(g_mq[l] * (MEM_HEAD_DIM ** -0.5 * LOG2E)).reshape(1, MEM_HEAD_DIM),
        w_mo=w_mo[l].astype(BF16), g_ffn=g_ffn[l],
        w_ff1=w_ff1[l].astype(BF16), w_ff2=w_ff2[l].astype(BF16),
    )

    n_mem = mem_prompt.shape[1]
    mk_p, mv_p = _memkv(mem_prompt.reshape(b * n_mem, d), g_mem[l], w_mk[l], w_mv[l], g_mk[l])
    mk_p = mk_p.reshape(b, n_mem * MEM_HEADS, MEM_HEAD_DIM)
    mv_p = mv_p.reshape(b, n_mem * MEM_HEADS, MEM_HEAD_DIM)

    zero_conv = jnp.zeros((b, CONV_K - 1, cw), F32)
    yp, kp, vp, cp = _layer(x_prompt, None, None, zero_conv, mk_p, mv_p, w, lam, lam_init)

    p = cache_k.shape[2]
    k_past = jnp.transpose(cache_k[l], (0, 2, 3, 4, 1)).reshape(bs, cols, p)
    v_past = cache_v[l].reshape(bs, p * N_HEADS, V_DIM)
    ys, kn, vn, cn = _layer(x_sample, k_past, v_past, cache_conv[l],
                            cache_mem_k[l].reshape(bs, n_mem * MEM_HEADS, MEM_HEAD_DIM),
                            cache_mem_v[l].reshape(bs, n_mem * MEM_HEADS, MEM_HEAD_DIM),
                            w, lam, lam_init)

    return (yp, ys, kp[None], vp[None], cp[None],
            mk_p.reshape(1, b, n_mem, MEM_HEADS, MEM_HEAD_DIM),
            mv_p.reshape(1, b, n_mem, MEM_HEADS, MEM_HEAD_DIM),
            kn[None], vn[None], cn[None])
```

```python
import functools
import math

import jax
import jax.numpy as jnp
import numpy as np
from jax import lax
from jax.experimental import pallas as pl
from jax.experimental.pallas import tpu as pltpu

F32 = jnp.float32
BF16 = jnp.bfloat16

CHUNK = 64
N_HEADS = 4
HEAD_DIM = 64
V_DIM = 2 * HEAD_DIM
HEAD_COLS = 2 * HEAD_DIM
CONV_K = 31
CONV_HALO = 32
N_BUCKETS = 32
MAX_DISTANCE = 128
MEM_HEADS = 4
MEM_HEAD_DIM = 128
EPS = 1e-6
NEG_INF = -1e30
LOG2E = math.log2(math.e)
SUBLANES = 8

ATTN_TQ = 512
ATTN_TK = 256
VT_ROWS = V_DIM + 16
MAX_UNSTABILISED_SCORE = 100.0
CACHE_TK = 4096
NEW_KEY_PAD = 128
ROW_TILE = 512
INPROJ_TILE = 1024
MLP_TILE = 1024
MLP_HIDDEN_PART = 2048
CONV_ROWS = 64
SAMPLE_BATCH_TILE = 8
VMEM_LIMIT = 56 * 1024 * 1024


def _cparams(sem):
    return pltpu.CompilerParams(dimension_semantics=sem, vmem_limit_bytes=VMEM_LIMIT)


def _rms(x, g):
    ms = jnp.mean(x * x, axis=-1, keepdims=True)
    return x * lax.rsqrt(ms + EPS) * g


def _const_spec(shape):
    return pl.BlockSpec(shape, lambda *_: (0,) * len(shape), pipeline_mode=pl.Buffered(1))


def _memkv_kernel(mem_ref, g_ref, wk_ref, wv_ref, gk_ref, mk_ref, mv_ref):
    m = _rms(mem_ref[...], g_ref[...]).astype(BF16)
    zk = jnp.dot(m, wk_ref[...], preferred_element_type=F32)
    zv = jnp.dot(m, wv_ref[...], preferred_element_type=F32)
    tm = mem_ref.shape[0]
    for h in range(MEM_HEADS):
        sl = slice(h * MEM_HEAD_DIM, (h + 1) * MEM_HEAD_DIM)
        rows = pl.ds(h, tm, stride=MEM_HEADS)
        mk_ref[rows, :] = _rms(zk[:, sl], gk_ref[...])
        mv_ref[rows, :] = zv[:, sl]


def _memkv(mem2d, g_mem, w_mk, w_mv, g_mk):
    n, d = mem2d.shape
    w = w_mk.shape[1]
    tm = min(ROW_TILE, n)
    return pl.pallas_call(
        _memkv_kernel,
        grid=(n // tm,),
        in_specs=[pl.BlockSpec((tm, d), lambda i: (i, 0)), _const_spec((1, d)),
                  _const_spec((d, w)), _const_spec((d, w)), _const_spec((1, MEM_HEAD_DIM))],
        out_specs=[pl.BlockSpec((tm * MEM_HEADS, MEM_HEAD_DIM), lambda i: (i, 0))] * 2,
        out_shape=[jax.ShapeDtypeStruct((n * MEM_HEADS, MEM_HEAD_DIM), F32)] * 2,
        compiler_params=_cparams(("parallel",)),
        name="memkv",
    )(mem2d, g_mem.reshape(1, d), w_mk.astype(BF16), w_mv.astype(BF16),
      g_mk.reshape(1, MEM_HEAD_DIM))


def _inproj_kernel(x_ref, g_ref, w_ref, gq_ref, gk_ref, *refs, qk_cols, emit_t):
    if emit_t:
        qt_ref, kb_ref, vt_ref, kt_ref, v4_ref, c_ref = refs
    else:
        seg_ref, qb_ref, k_ref, v_ref, c_ref = refs

    def mapnorm(z, g):
        ms = jnp.dot((z * z).astype(BF16), seg_ref[...], preferred_element_type=F32)
        return z * lax.rsqrt(ms + EPS) * g

    def mapnorm_t(z, g_t):
        zt = z.T
        z3 = zt.reshape(zt.shape[0] // HEAD_DIM, HEAD_DIM, zt.shape[1])
        ms = jnp.mean(z3 * z3, axis=1, keepdims=True)
        return (z3 * lax.rsqrt(ms + EPS)).reshape(zt.shape) * g_t

    h = _rms(x_ref[0], g_ref[...]).astype(BF16)

    def proj(lo, hi):
        return jnp.dot(h, w_ref[:, lo:hi], preferred_element_type=F32)

    c0 = qk_cols
    zq = proj(0, c0)
    zk = proj(c0, 2 * c0)
    v = proj(2 * c0, 3 * c0)
    a = proj(3 * c0, 4 * c0)
    gate = proj(4 * c0, 5 * c0)
    c_ref[0] = a * jax.nn.sigmoid(gate)
    if emit_t:
        kt = mapnorm_t(zk, gk_ref[...])
        kt_ref[0] = kt
        kb_ref[0] = kt.T.astype(BF16)
        tm = x_ref.shape[1]
        for hh in range(N_HEADS):
            v4_ref[0, pl.ds(hh, tm, stride=N_HEADS), :] = v[:, hh * V_DIM:(hh + 1) * V_DIM]
        qt = mapnorm_t(zq, gq_ref[...]).astype(BF16)
        vt = v.T.astype(BF16)
        for j in range(qt_ref.shape[1]):
            qt_ref[0, j] = qt[:, j * ATTN_TQ:(j + 1) * ATTN_TQ]
        ones_row = (lax.broadcasted_iota(jnp.int32, (VT_ROWS - V_DIM, ATTN_TK), 0) == 0).astype(BF16)
        for j in range(vt_ref.shape[1]):
            for hh in range(N_HEADS):
                vt_ref[0, j, hh, :V_DIM, :] = vt[hh * V_DIM:(hh + 1) * V_DIM,
                                                 j * ATTN_TK:(j + 1) * ATTN_TK]
                vt_ref[0, j, hh, V_DIM:, :] = ones_row
    else:
        k_ref[0] = mapnorm(zk, gk_ref[...])
        v_ref[0] = v
        qb_ref[0] = mapnorm(zq, gq_ref[...]).astype(BF16)


def _inproj(x, g_mix, w_in_bf, gq_row, gk_row, seg, *, emit_t):
    b, t, d = x.shape
    cols = gq_row.shape[1]
    tm = min(INPROJ_TILE if emit_t else ROW_TILE, t)
    nt = t // tm
    row = lambda width: pl.BlockSpec((1, tm, width), lambda bi, ti: (bi, ti, 0))
    f32_out = jax.ShapeDtypeStruct((b, t, cols), F32)
    if emit_t:
        nq, nk = tm // ATTN_TQ, tm // ATTN_TK
        out_shape = [jax.ShapeDtypeStruct((b, t // ATTN_TQ, cols, ATTN_TQ), BF16),
                     jax.ShapeDtypeStruct((b, t, cols), BF16),
                     jax.ShapeDtypeStruct((b, t // ATTN_TK, N_HEADS, VT_ROWS, ATTN_TK), BF16),
                     jax.ShapeDtypeStruct((b, cols, t), F32),
                     jax.ShapeDtypeStruct((b, t * N_HEADS, V_DIM), F32),
                     f32_out]
        out_specs = [pl.BlockSpec((1, nq, cols, ATTN_TQ), lambda bi, ti: (bi, ti, 0, 0)),
                     row(cols),
                     pl.BlockSpec((1, nk, N_HEADS, VT_ROWS, ATTN_TK),
                                  lambda bi, ti: (bi, ti, 0, 0, 0)),
                     pl.BlockSpec((1, cols, tm), lambda bi, ti: (bi, 0, ti)),
                     pl.BlockSpec((1, tm * N_HEADS, V_DIM), lambda bi, ti: (bi, ti, 0)),
                     row(cols)]
        gains = [jnp.broadcast_to(g.reshape(cols, 1), (cols, tm)) for g in (gq_row, gk_row)]
        gain_specs = [_const_spec((cols, tm))] * 2
    else:
        out_shape = [jax.ShapeDtypeStruct((b, t, cols), BF16)] + [f32_out] * 3
        out_specs = [row(cols)] * 4
        gains = [gq_row, gk_row, seg]
        gain_specs = [_const_spec((1, cols)), _const_spec((1, cols)), _const_spec((cols, cols))]
    return pl.pallas_call(
        functools.partial(_inproj_kernel, qk_cols=cols, emit_t=emit_t),
        grid=(b, nt),
        in_specs=[row(d), _const_spec((1, d)), _const_spec(w_in_bf.shape)] + gain_specs,
        out_specs=out_specs,
        out_shape=out_shape,
        compiler_params=_cparams(("parallel", "parallel")),
        name="inproj",
    )(x, g_mix.reshape(1, d), w_in_bf, *gains)


PREV, DIAG0, DIAG1, FAR = 0, 1, 2, None


def _build_qbd(qt_ref, qbd_ref):
    w = ATTN_TK
    upper = lax.broadcasted_iota(jnp.int32, (HEAD_COLS, w), 0) < HEAD_DIM
    for h in range(N_HEADS):
        for half in range(ATTN_TQ // w):
            qh = qt_ref[0, 0, h * HEAD_COLS:(h + 1) * HEAD_COLS, half * w:(half + 1) * w]
            zero = jnp.zeros_like(qh)
            qbd_ref[h, :, 2 * half * w:(2 * half + 1) * w] = jnp.where(upper, qh, zero)
            qbd_ref[h, :, (2 * half + 1) * w:(2 * half + 2) * w] = jnp.where(upper, zero, qh)


def _attn_finalize(acc_ref, denom, lam, gsub_ref, o_ref):
    w = ATTN_TK
    for h in range(N_HEADS):
        acc = acc_ref[h, :V_DIM, :] * (1.0 / denom(h))
        for half in range(ATTN_TQ // w):
            m0 = acc[:, 2 * half * w:(2 * half + 1) * w]
            m1 = acc[:, (2 * half + 1) * w:(2 * half + 2) * w]
            ot = m0 - lam * m1
            ms = jnp.mean(ot * ot, axis=0, keepdims=True)
            ot = ot * lax.rsqrt(ms + EPS)
            o_ref[0, half * w:(half + 1) * w, h * V_DIM:(h + 1) * V_DIM] = (
                ot.T * gsub_ref[...]).astype(BF16)


def _attn_prompt_kernel(sc_ref, qt_ref, kb_ref, vt_ref, bias_ref, gsub_ref, o_ref,
                        qbd_ref, acc_ref, sa_ref, sb_ref, *stat_refs, bounded):
    i = pl.program_id(1)
    tq, tk = ATTN_TQ, ATTN_TK
    lam = sc_ref[N_HEADS]
    if bounded:
        lsum_ref, = stat_refs
        slot_a, slot_b = (sa_ref, None), (sb_ref, None)
        lsum_ref[...] = jnp.zeros(lsum_ref.shape, F32)
    else:
        m_ref, mca_ref, mcb_ref = stat_refs
        slot_a, slot_b = (sa_ref, mca_ref), (sb_ref, mcb_ref)
        m_ref[...] = jnp.full(m_ref.shape, NEG_INF, F32)

    _build_qbd(qt_ref, qbd_ref)
    acc_ref[...] = jnp.zeros(acc_ref.shape, F32)

    every = slice(0, 2 * tq)

    def stage(j, slot, tile, cols=every):
        s_ref, mc_ref = slot
        row0 = pl.multiple_of(j * tk, tk)
        width = cols.stop - cols.start
        for h in range(N_HEADS):
            kh = kb_ref[0, pl.ds(row0, tk), h * HEAD_COLS:(h + 1) * HEAD_COLS]
            s = jnp.dot(kh, qbd_ref[h, :, cols], preferred_element_type=F32)
            if tile is not FAR:
                s = s + bias_ref[h, tile, :, cols]
            if bounded:
                p = jnp.exp2(s)
                s_ref[h, :, cols] = p.astype(BF16)
                lsum_ref[h, :, cols] += jnp.sum(p.reshape(tk // SUBLANES, SUBLANES, width), axis=0)
            else:
                s_ref[h, :, cols] = s
                mc_ref[h, :, cols] = jnp.max(s, axis=0, keepdims=True)

    def consume(j, slot, tile, cols=every):
        s_ref, mc_ref = slot
        for h in range(N_HEADS):
            if bounded:
                acc_ref[h, :V_DIM, cols] += jnp.dot(vt_ref[0, j, h, :V_DIM, :], s_ref[h, :, cols],
                                                    preferred_element_type=F32)
                continue
            m_prev = m_ref[h, :, cols]
            if tile is FAR:
                shift = sc_ref[h]
                m_new = jnp.maximum(m_prev, mc_ref[h, :, cols] + shift)
                p = jnp.exp2(s_ref[h, :, cols] - (m_new - shift))
            else:
                m_new = jnp.maximum(m_prev, mc_ref[h, :, cols])
                p = jnp.exp2(s_ref[h, :, cols] - m_new)
            alpha = jnp.exp2(m_prev - m_new)
            pv = jnp.dot(vt_ref[0, j, h], p.astype(BF16), preferred_element_type=F32)
            acc_ref[h, :, cols] = alpha * acc_ref[h, :, cols] + pv
            m_ref[h, :, cols] = m_new

    second_half = slice(tq, 2 * tq)
    stage(2 * i + 1, slot_a, DIAG1, second_half)
    consume(2 * i + 1, slot_a, DIAG1, second_half)
    stage(2 * i, slot_b, DIAG0)

    @pl.when(i == 0)
    def _():
        consume(2 * i, slot_b, DIAG0)

    @pl.when(i >= 1)
    def _():
        consume(2 * i, slot_b, DIAG0)
        stage(2 * i - 1, slot_a, PREV)
        consume(2 * i - 1, slot_a, PREV)
        if bounded:
            for h in range(N_HEADS):
                acc_ref[h] = acc_ref[h] * sc_ref[N_HEADS + 1 + h]
                lsum_ref[h] = lsum_ref[h] * sc_ref[N_HEADS + 1 + h]
        stage(0, slot_b, FAR)

    def pair_body(jj, carry):
        f = 2 * jj
        consume(f, slot_b, FAR)
        stage(f + 1, slot_a, FAR)
        consume(f + 1, slot_a, FAR)
        stage(f + 2, slot_b, FAR)
        return carry

    lax.fori_loop(0, jnp.maximum(i - 1, 0), pair_body, 0)

    @pl.when(i >= 1)
    def _():
        consume(2 * i - 2, slot_b, FAR)

    if bounded:
        denom = lambda h: jnp.sum(lsum_ref[h], axis=0, keepdims=True)
    else:
        denom = lambda h: acc_ref[h, V_DIM:V_DIM + 1, :]
    _attn_finalize(acc_ref, denom, lam, gsub_ref, o_ref)


def _attn_prompt(scalars, score_bound, qt, kb, vt, bias_t, gsub_row):
    b, nq, cols, tq = qt.shape
    t = kb.shape[1]
    nk = vt.shape[1]
    qbd = pltpu.VMEM((N_HEADS, HEAD_COLS, 2 * tq), BF16)
    acc = pltpu.VMEM((N_HEADS, VT_ROWS, 2 * tq), F32)
    stat = pltpu.VMEM((N_HEADS, 1, 2 * tq), F32)
    scores = pltpu.VMEM((N_HEADS, ATTN_TK, 2 * tq), F32)
    probs = pltpu.VMEM((N_HEADS, ATTN_TK, 2 * tq), BF16)
    colsum = pltpu.VMEM((N_HEADS, SUBLANES, 2 * tq), F32)

    def call(bounded, scratch, name):
        return pl.pallas_call(
            functools.partial(_attn_prompt_kernel, bounded=bounded),
            grid=(b, nq),
            in_specs=[pl.BlockSpec(memory_space=pltpu.SMEM),
                      pl.BlockSpec((1, 1, cols, tq), lambda bi, i: (bi, i, 0, 0)),
                      pl.BlockSpec((1, t, cols), lambda bi, i: (bi, 0, 0)),
                      pl.BlockSpec((1, nk, N_HEADS, VT_ROWS, ATTN_TK),
                                   lambda bi, i: (bi, 0, 0, 0, 0)),
                      _const_spec(bias_t.shape), _const_spec((1, V_DIM))],
            out_specs=pl.BlockSpec((1, tq, cols), lambda bi, i: (bi, i, 0)),
            out_shape=jax.ShapeDtypeStruct((b, t, cols), BF16),
            scratch_shapes=scratch,
            compiler_params=_cparams(("parallel", "arbitrary")),
            name=name,
        )(scalars, qt, kb, vt, bias_t, gsub_row)

    return lax.cond(
        score_bound <= MAX_UNSTABILISED_SCORE,
        lambda: call(True, [qbd, acc, probs, probs, colsum], "attn_prompt_bounded"),
        lambda: call(False, [qbd, acc, scores, scores, stat, stat, stat], "attn_prompt"))


def _attn_sample_kernel(sc_ref, q_ref, kt_ref, vc_ref, ktn_ref, vn_ref, bc_ref, bn_ref, gsub_ref,
                        o_ref, qbd_ref, m_ref, l_ref, acc_ref):
    j = pl.program_id(1)
    tq, cols = q_ref.shape[1], q_ref.shape[2]
    lam = sc_ref[N_HEADS]

    @pl.when(j == 0)
    def _():
        q = q_ref[0].astype(F32)
        col = lax.broadcasted_iota(jnp.int32, (tq, cols), 1)
        for hm in range(cols // HEAD_DIM):
            mine = (col >= hm * HEAD_DIM) & (col < (hm + 1) * HEAD_DIM)
            qbd_ref[hm * tq:(hm + 1) * tq, :] = jnp.where(mine, q, 0.0)
        m_ref[...] = jnp.full(m_ref.shape, NEG_INF, F32)
        l_ref[...] = jnp.zeros(l_ref.shape, F32)
        acc_ref[...] = jnp.zeros(acc_ref.shape, F32)

    def update(kt, v_of_head, bias):
        s = jnp.dot(qbd_ref[...].astype(BF16), kt, preferred_element_type=F32) + bias
        m_prev = m_ref[...]
        m_new = jnp.maximum(m_prev, jnp.max(s, axis=-1, keepdims=True))
        alpha = jnp.exp2(m_prev - m_new)
        p = jnp.exp2(s - m_new)
        l_ref[...] = alpha * l_ref[...] + jnp.sum(p, axis=-1, keepdims=True)
        pb = p.astype(BF16)
        for h in range(N_HEADS):
            rows = slice(h * 2 * tq, (h + 1) * 2 * tq)
            acc_ref[rows, :] = alpha[rows] * acc_ref[rows, :] + jnp.dot(
                pb[rows], v_of_head(h), preferred_element_type=F32)
        m_ref[...] = m_new

    tk = kt_ref.shape[2]
    update(kt_ref[0].astype(BF16),
           lambda h: vc_ref[0, pl.ds(h, tk, stride=N_HEADS), :].astype(BF16), bc_ref[j])

    @pl.when(j == pl.num_programs(1) - 1)
    def _():
        update(ktn_ref[0], lambda h: vn_ref[0, :, h * V_DIM:(h + 1) * V_DIM], bn_ref[...])
        for h in range(N_HEADS):
            r0 = h * 2 * tq
            inv0 = 1.0 / l_ref[r0:r0 + tq, :]
            inv1 = 1.0 / l_ref[r0 + tq:r0 + 2 * tq, :]
            o = acc_ref[r0:r0 + tq, :] * inv0 - lam * (acc_ref[r0 + tq:r0 + 2 * tq, :] * inv1)
            o_ref[0, :, h * V_DIM:(h + 1) * V_DIM] = _rms(o, gsub_ref[...]).astype(BF16)


def _attn_sample(scalars, q, kt, v4, ktn, vn, bias_c, bias_n, gsub_row):
    b, tq, cols = q.shape
    p = kt.shape[2]
    nkc = p // CACHE_TK
    rows = 2 * N_HEADS * tq
    return pl.pallas_call(
        _attn_sample_kernel,
        grid=(b, nkc),
        in_specs=[pl.BlockSpec(memory_space=pltpu.SMEM),
                  pl.BlockSpec((1, tq, cols), lambda bi, j: (bi, 0, 0)),
                  pl.BlockSpec((1, cols, CACHE_TK), lambda bi, j: (bi, 0, j)),
                  pl.BlockSpec((1, CACHE_TK * N_HEADS, V_DIM), lambda bi, j: (bi, j, 0)),
                  pl.BlockSpec((1, cols, NEW_KEY_PAD), lambda bi, j: (bi, 0, 0)),
                  pl.BlockSpec((1, NEW_KEY_PAD, cols), lambda bi, j: (bi, 0, 0)),
                  _const_spec(bias_c.shape), _const_spec(bias_n.shape), _const_spec((1, V_DIM))],
        out_specs=pl.BlockSpec((1, tq, cols), lambda bi, j: (bi, 0, 0)),
        out_shape=jax.ShapeDtypeStruct((b, tq, cols), BF16),
        scratch_shapes=[pltpu.VMEM((rows, cols), F32),
                        pltpu.VMEM((rows, 1), F32),
                        pltpu.VMEM((rows, 1), F32),
                        pltpu.VMEM((rows, V_DIM), F32)],
        compiler_params=_cparams(("parallel", "arbitrary")),
        name="attn_sample",
    )(scalars, q, kt, v4, ktn, vn, bias_c, bias_n, gsub_row)


def _mix_kernel(x_ref, o_ref, c_ref, hist_ref, wconv_ref, bconv_ref, lng_ref, lnb_ref,
                woo_ref, woc_ref, gcross_ref, wmq_ref, gmq_ref, mk_ref, mv_ref, wmo_ref,
                out_ref, ext_ref, xs_ref, cv_ref, ca_ref):
    bb, tt, d = x_ref.shape
    rows = bb * tt
    n_mem = mk_ref.shape[1] // MEM_HEADS
    rc = min(CONV_ROWS, tt)
    lead = CONV_HALO - (CONV_K - 1)

    for b in range(bb):
        ext_ref[b, :CONV_HALO, :] = hist_ref[b, 0]
        ext_ref[b, CONV_HALO:, :] = c_ref[b]
    span = xs_ref.shape[2]
    for b in range(bb):
        for r in range(1, SUBLANES):
            xs_ref[r - 1, b] = ext_ref[b, r:r + span, :]
    for b in range(bb):
        for r0 in range(0, tt, rc):
            acc = jnp.zeros((rc, c_ref.shape[2]), F32) + bconv_ref[...]
            for k in range(CONV_K):
                a, r = divmod(k + lead, SUBLANES)
                lo = r0 + SUBLANES * a
                src = ext_ref[b, lo:lo + rc, :] if r == 0 else xs_ref[r - 1, b, lo:lo + rc, :]
                acc = acc + wconv_ref[k:k + 1, :] * src
            mu = jnp.mean(acc, axis=-1, keepdims=True)
            xc = acc - mu
            var = jnp.mean(xc * xc, axis=-1, keepdims=True)
            y = xc * lax.rsqrt(var + EPS) * lng_ref[...] + lnb_ref[...]
            cv_ref[b * tt + r0:b * tt + r0 + rc, :] = (y * jax.nn.sigmoid(y)).astype(BF16)

    x = x_ref[...].reshape(rows, d)
    o = o_ref[...].reshape(rows, o_ref.shape[2])
    x1 = (x + jnp.dot(o, woo_ref[...], preferred_element_type=F32)
          + jnp.dot(cv_ref[...], woc_ref[...], preferred_element_type=F32))

    hc = _rms(x1, gcross_ref[...]).astype(BF16)
    qm = jnp.dot(hc, wmq_ref[...], preferred_element_type=F32)
    for h in range(MEM_HEADS):
        cols = slice(h * MEM_HEAD_DIM, (h + 1) * MEM_HEAD_DIM)
        qn = _rms(qm[:, cols], gmq_ref[...]).astype(BF16)
        for b in range(bb):
            mem_rows = pl.ds(h, n_mem, stride=MEM_HEADS)
            mk = mk_ref[b, mem_rows, :].astype(BF16)
            mv = mv_ref[b, mem_rows, :].astype(BF16)
            s = lax.dot_general(qn[b * tt:(b + 1) * tt], mk, (((1,), (1,)), ((), ())),
                                preferred_element_type=F32)
            p = jnp.exp2(s - jnp.max(s, axis=-1, keepdims=True))
            l = jnp.sum(p, axis=-1, keepdims=True)
            oh = jnp.dot(p.astype(BF16), mv, preferred_element_type=F32) / l
            ca_ref[b * tt:(b + 1) * tt, cols] = oh.astype(BF16)
    x2 = x1 + jnp.dot(ca_ref[...], wmo_ref[...], preferred_element_type=F32)
    out_ref[...] = x2.reshape(bb, tt, d)


def _mix(x, o, c, hist, wconv, bconv, lng, lnb, woo, woc, gcross, wmq, gmq_row, mk, mv, wmo,
         *, bb, tt):
    b, t, d = x.shape
    cw = c.shape[2]
    mem_rows, mhd = mk.shape[1], mk.shape[2]
    mw = wmq.shape[1]
    rows = bb * tt
    tile = lambda width: pl.BlockSpec((bb, tt, width), lambda bi, ti: (bi, ti, 0))
    return pl.pallas_call(
        _mix_kernel,
        grid=(b // bb, t // tt),
        in_specs=[tile(d), tile(cw), tile(cw),
                  pl.BlockSpec((bb, 1, CONV_HALO, cw), lambda bi, ti: (bi, ti, 0, 0)),
                  _const_spec(wconv.shape), _const_spec((1, cw)), _const_spec((1, cw)),
                  _const_spec((1, cw)), _const_spec(woo.shape), _const_spec(woc.shape),
                  _const_spec((1, d)), _const_spec(wmq.shape), _const_spec((1, MEM_HEAD_DIM)),
                  pl.BlockSpec((bb, mem_rows, mhd), lambda bi, ti: (bi, 0, 0)),
                  pl.BlockSpec((bb, mem_rows, mhd), lambda bi, ti: (bi, 0, 0)),
                  _const_spec(wmo.shape)],
        out_specs=tile(d),
        out_shape=jax.ShapeDtypeStruct((b, t, d), F32),
        scratch_shapes=[pltpu.VMEM((bb, CONV_HALO + tt, cw), F32),
                        pltpu.VMEM((SUBLANES - 1, bb, CONV_HALO - SUBLANES + tt, cw), F32),
                        pltpu.VMEM((rows, cw), BF16),
                        pltpu.VMEM((rows, mw), BF16)],
        compiler_params=_cparams(("parallel", "parallel")),
        name="mix",
    )(x, o, c, hist, wconv, bconv, lng, lnb, woo, woc, gcross, wmq, gmq_row, mk, mv, wmo)


def _mlp_kernel(x_ref, g_ref, w1_ref, w2_ref, out_ref):
    x = x_ref[...]
    hf = _rms(x, g_ref[...]).astype(BF16)
    dff = w1_ref.shape[1]
    part = min(dff, MLP_HIDDEN_PART)
    y = x
    for lo in range(0, dff, part):
        u = jnp.maximum(jnp.dot(hf, w1_ref[:, lo:lo + part], preferred_element_type=F32), 0.0)
        y = y + jnp.dot((u * u).astype(BF16), w2_ref[lo:lo + part, :], preferred_element_type=F32)
    out_ref[...] = y


def _mlp(x2d, g_ffn, w1, w2):
    n, d = x2d.shape
    tm = min(MLP_TILE, n)
    return pl.pallas_call(
        _mlp_kernel,
        grid=(n // tm,),
        in_specs=[pl.BlockSpec((tm, d), lambda i: (i, 0)), _const_spec((1, d)),
                  _const_spec(w1.shape), _const_spec(w2.shape)],
        out_specs=pl.BlockSpec((tm, d), lambda i: (i, 0)),
        out_shape=jax.ShapeDtypeStruct((n, d), F32),
        compiler_params=_cparams(("parallel",)),
        name="mlp",
    )(x2d, g_ffn.reshape(1, d), w1, w2)


def _rel_bucket(rel):
    half = N_BUCKETS // 2
    max_exact = half // 2
    ret = jnp.where(rel > 0, half, 0)
    n = jnp.abs(rel)
    nf = jnp.maximum(n, 1).astype(jnp.float32)
    large = max_exact + (jnp.log(nf / max_exact) / math.log(MAX_DISTANCE / max_exact)
                         * (half - max_exact)).astype(jnp.int32)
    large = jnp.minimum(large, half - 1)
    return ret + jnp.where(n < max_exact, n, large)


def _masked_bias(rel_table, q_pos, k_pos):
    bucket = _rel_bucket(k_pos[None, :] - q_pos[:, None])[None]
    table = rel_table.astype(F32) * LOG2E
    bias = jnp.zeros((rel_table.shape[1],) + bucket.shape[1:], F32)
    for bkt in range(N_BUCKETS):
        bias = jnp.where(bucket == bkt, table[bkt][:, None, None], bias)
    mask = (k_pos[None, :] // CHUNK) <= (q_pos[:, None] // CHUNK)
    return jnp.where(mask[None], bias, NEG_INF)


def _far_bucket_is_saturated(min_distance):
    half = N_BUCKETS // 2
    max_exact = half // 2
    large = max_exact + int(np.log(min_distance / max_exact) / math.log(MAX_DISTANCE / max_exact)
                            * (half - max_exact) * (1 - 1e-6))
    return large >= half - 1


def _layer(x, k_past, v_past, c_past, mk, mv, w, lam, lam_init):
    b, t, d = x.shape
    prompt = k_past is None
    cols = N_HEADS * HEAD_COLS
    gsub_row = (w["g_sub"] * (1.0 - lam_init)).reshape(1, V_DIM)

    if prompt:
        assert t % ROW_TILE == 0 and t % INPROJ_TILE == 0 and INPROJ_TILE % ATTN_TQ == 0
        assert ROW_TILE % ATTN_TQ == 0 and ATTN_TQ == 2 * ATTN_TK
        assert ATTN_TK % CHUNK == 0 and _far_bucket_is_saturated(ATTN_TK + 1)
        qt, kb, vt, kt, v4, c = _inproj(x, w["g_mix"], w["w_in"], w["gq_row"], w["gk_row"],
                                        w["seg"], emit_t=True)
        k_out = jnp.transpose(kt.reshape(b, N_HEADS, 2, HEAD_DIM, t), (0, 4, 1, 2, 3))
        v_out = v4.reshape(b, t, N_HEADS, V_DIM)
        q_pos = ATTN_TQ + jnp.arange(ATTN_TQ, dtype=jnp.int32)
        k_pos = ATTN_TK + jnp.arange(3 * ATTN_TK, dtype=jnp.int32)
        near = _masked_bias(w["rel_table"], q_pos, k_pos)
        near = jnp.stack([near[:, :, n * ATTN_TK:(n + 1) * ATTN_TK] for n in range(3)], axis=1)
        near_t = jnp.swapaxes(near, 2, 3)
        halves = [near_t[..., n * ATTN_TK:(n + 1) * ATTN_TK] for n in range(ATTN_TQ // ATTN_TK)]
        bias_t = jnp.concatenate([hf for hf in halves for _ in range(2)], axis=3)
        far = w["rel_table"][_rel_bucket(jnp.int32(-(ATTN_TK + 1)))].astype(F32) * LOG2E
        scalars = jnp.concatenate([far, lam.reshape(1), jnp.exp2(-far)]).astype(F32)
        score_bound = (HEAD_DIM * jnp.max(jnp.abs(w["gq_row"])) * jnp.max(jnp.abs(w["gk_row"]))
                       + LOG2E * jnp.max(jnp.abs(w["rel_table"])))
        o = _attn_prompt(scalars, score_bound, qt, kb, vt, bias_t, gsub_row)
        bb, tt = 1, ROW_TILE
    else:
        p = k_past.shape[2]
        assert p % CACHE_TK == 0 and t <= NEW_KEY_PAD and t % 16 == 0
        qb, k, v, c = _inproj(x.reshape(1, b * t, d), w["g_mix"], w["w_in"], w["gq_row"],
                              w["gk_row"], w["seg"], emit_t=False)
        qb, k, v, c = (a.reshape(b, t, cols) for a in (qb, k, v, c))
        k_out = k.reshape(b, t, N_HEADS, 2, HEAD_DIM)
        v_out = v.reshape(b, t, N_HEADS, V_DIM)
        ktn = jnp.pad(jnp.swapaxes(k, 1, 2).astype(BF16), ((0, 0), (0, 0), (0, NEW_KEY_PAD - t)))
        vn = jnp.pad(v.astype(BF16), ((0, 0), (0, NEW_KEY_PAD - t), (0, 0)))
        q_pos = p + jnp.arange(t, dtype=jnp.int32)
        bias = _masked_bias(w["rel_table"], q_pos, jnp.arange(p + NEW_KEY_PAD, dtype=jnp.int32))
        bias = jnp.where(jnp.arange(p + NEW_KEY_PAD) < p + t, bias, NEG_INF)
        rows = 2 * N_HEADS * t
        bias = jnp.broadcast_to(bias[:, None], (N_HEADS, 2, t, p + NEW_KEY_PAD)).reshape(rows, -1)
        bias_c = jnp.swapaxes(bias[:, :p].reshape(rows, p // CACHE_TK, CACHE_TK), 0, 1)
        scalars = jnp.concatenate([jnp.zeros((N_HEADS,), F32), lam.reshape(1)]).astype(F32)
        o = _attn_sample(scalars, qb, k_past, v_past, ktn, vn, bias_c, bias[:, p:], gsub_row)
        bb, tt = SAMPLE_BATCH_TILE, t
        assert b % bb == 0

    nt = t // tt
    first = jnp.pad(c_past, ((0, 0), (CONV_HALO - (CONV_K - 1), 0), (0, 0)))[:, None]
    if nt > 1:
        tails = c.reshape(b, nt, tt, c.shape[2])[:, :-1, tt - CONV_HALO:, :]
        hist = jnp.concatenate([first, tails], axis=1)
    else:
        hist = first
    x2 = _mix(x, o, c, hist, w["w_conv"], w["b_conv"], w["ln_g"], w["ln_b"], w["w_out_o"],
              w["w_out_c"], w["g_cross"], w["w_mq"], w["gmq_row"], mk, mv, w["w_mo"], bb=bb, tt=tt)
    y = _mlp(x2.reshape(b * t, d), w["g_ffn"], w["w_ff1"], w["w_ff2"]).reshape(b, t, d)
    if t >= CONV_K - 1:
        c_hist_tail = c[:, t - (CONV_K - 1):]
    else:
        c_hist_tail = jnp.concatenate([c_past[:, t:], c], axis=1)
    return y, k_out, v_out, c_hist_tail


def kernel(x_prompt, x_sample, cache_k, cache_v, cache_conv, cache_mem_k, cache_mem_v, mem_prompt,
           rel_table, g_mix, w_in, g_q, g_k, lam_vec, g_sub, w_conv, b_conv, ln_g, ln_b, w_out,
           g_cross, g_mem, w_mq, w_mk, w_mv, g_mq, g_mk, w_mo, g_ffn, w_ff1, w_ff2):
    depth = g_mix.shape[0]
    assert depth == 1
    b, t, d = x_prompt.shape
    bs, ts, _ = x_sample.shape
    cols = N_HEADS * HEAD_COLS
    cw = w_conv.shape[2]
    attn_w = N_HEADS * V_DIM
    l = 0
    lam_init = 0.8 - 0.6 * math.exp(-0.3 * l)
    lp = lam_vec[l].astype(F32)
    lam = jnp.exp(jnp.sum(lp[0] * lp[1])) - jnp.exp(jnp.sum(lp[2] * lp[3])) + lam_init

    seg = jnp.kron(jnp.eye(cols // HEAD_DIM, dtype=F32),
                   jnp.full((HEAD_DIM, HEAD_DIM), 1.0 / HEAD_DIM, F32)).astype(BF16)
    n_maps = cols // HEAD_DIM
    w = dict(
        rel_table=rel_table, g_mix=g_mix[l], w_in=w_in[l].astype(BF16), seg=seg,
        gq_row=jnp.tile(g_q[l] * (HEAD_DIM ** -0.5 * LOG2E), n_maps).reshape(1, cols),
        gk_row=jnp.tile(g_k[l], n_maps).reshape(1, cols),
        g_sub=g_sub[l],
        w_conv=jnp.pad(w_conv[l], ((0, CONV_HALO - CONV_K), (0, 0))),
        b_conv=b_conv[l].reshape(1, cw), ln_g=ln_g[l].reshape(1, cw), ln_b=ln_b[l].reshape(1, cw),
        w_out_o=w_out[l][:attn_w].astype(BF16), w_out_c=w_out[l][attn_w:].astype(BF16),
        g_cross=g_cross[l].reshape(1, d), w_mq=w_mq[l].astype(BF16),
        gmq_row=(g_mq[l] * (MEM_HEAD_DIM ** -0.5 * LOG2E)).reshape(1, MEM_HEAD_DIM),
        w_mo=w_mo[l].astype(BF16), g_ffn=g_ffn[l],
        w_ff1=w_ff1[l].astype(BF16), w_ff2=w_ff2[l].astype(BF16),
    )

    n_mem = mem_prompt.shape[1]
    mk_p, mv_p = _memkv(mem_prompt.reshape(b * n_mem, d), g_mem[l], w_mk[l], w_mv[l], g_mk[l])
    mk_p = mk_p.reshape(b, n_mem * MEM_HEADS, MEM_HEAD_DIM)
    mv_p = mv_p.reshape(b, n_mem * MEM_HEADS, MEM_HEAD_DIM)

    zero_conv = jnp.zeros((b, CONV_K - 1, cw), F32)
    yp, kp, vp, cp = _layer(x_prompt, None, None, zero_conv, mk_p, mv_p, w, lam, lam_init)

    p = cache_k.shape[2]
    k_past = jnp.transpose(cache_k[l], (0, 2, 3, 4, 1)).reshape(bs, cols, p)
    v_past = cache_v[l].reshape(bs, p * N_HEADS, V_DIM)
    ys, kn, vn, cn = _layer(x_sample, k_past, v_past, cache_conv[l],
                            cache_mem_k[l].reshape(bs, n_mem * MEM_HEADS, MEM_HEAD_DIM),
                            cache_mem_v[l].reshape(bs, n_mem * MEM_HEADS, MEM_HEAD_DIM),
                            w, lam, lam_init)

    return (yp, ys, kp[None], vp[None], cp[None],
            mk_p.reshape(1, b, n_mem, MEM_HEADS, MEM_HEAD_DIM),
            mv_p.reshape(1, b, n_mem, MEM_HEADS, MEM_HEAD_DIM),
            kn[None], vn[None], cn[None])
```

```python
import functools
import math

import jax
import jax.numpy as jnp
import numpy as np
from jax import lax
from jax.experimental import pallas as pl
from jax.experimental.pallas import tpu as pltpu

F32 = jnp.float32
BF16 = jnp.bfloat16

CHUNK = 64
N_HEADS = 4
HEAD_DIM = 64
V_DIM = 2 * HEAD_DIM
HEAD_COLS = 2 * HEAD_DIM
CONV_K = 31
CONV_HALO = 32
N_BUCKETS = 32
MAX_DISTANCE = 128
MEM_HEADS = 4
MEM_HEAD_DIM = 128
EPS = 1e-6
NEG_INF = -1e30
LOG2E = math.log2(math.e)
SUBLANES = 8

ATTN_TQ = 512
ATTN_TK = 256
VT_ROWS = V_DIM + 16
MAX_UNSTABILISED_SCORE = 100.0
CACHE_TK = 4096
NEW_KEY_PAD = 128
ROW_TILE = 512
INPROJ_TILE = 1024
MIX_TILE = 1024
MLP_TILE = 1024
MLP_HIDDEN_PART = 2048
CONV_ROWS = 64
SAMPLE_BATCH_TILE = 8
VMEM_LIMIT = 56 * 1024 * 1024


def _cparams(sem):
    return pltpu.CompilerParams(dimension_semantics=sem, vmem_limit_bytes=VMEM_LIMIT)


def _rms(x, g):
    ms = jnp.mean(x * x, axis=-1, keepdims=True)
    return x * lax.rsqrt(ms + EPS) * g


def _const_spec(shape):
    return pl.BlockSpec(shape, lambda *_: (0,) * len(shape), pipeline_mode=pl.Buffered(1))


def _memkv_kernel(mem_ref, g_ref, wk_ref, wv_ref, gk_ref, mk_ref, mv_ref):
    m = _rms(mem_ref[...], g_ref[...]).astype(BF16)
    zk = jnp.dot(m, wk_ref[...], preferred_element_type=F32)
    zv = jnp.dot(m, wv_ref[...], preferred_element_type=F32)
    tm = mem_ref.shape[0]
    for h in range(MEM_HEADS):
        sl = slice(h * MEM_HEAD_DIM, (h + 1) * MEM_HEAD_DIM)
        rows = pl.ds(h, tm, stride=MEM_HEADS)
        mk_ref[rows, :] = _rms(zk[:, sl], gk_ref[...])
        mv_ref[rows, :] = zv[:, sl]


def _memkv(mem2d, g_mem, w_mk, w_mv, g_mk):
    n, d = mem2d.shape
    w = w_mk.shape[1]
    tm = min(ROW_TILE, n)
    return pl.pallas_call(
        _memkv_kernel,
        grid=(n // tm,),
        in_specs=[pl.BlockSpec((tm, d), lambda i: (i, 0)), _const_spec((1, d)),
                  _const_spec((d, w)), _const_spec((d, w)), _const_spec((1, MEM_HEAD_DIM))],
        out_specs=[pl.BlockSpec((tm * MEM_HEADS, MEM_HEAD_DIM), lambda i: (i, 0))] * 2,
        out_shape=[jax.ShapeDtypeStruct((n * MEM_HEADS, MEM_HEAD_DIM), F32)] * 2,
        compiler_params=_cparams(("parallel",)),
        name="memkv",
    )(mem2d, g_mem.reshape(1, d), w_mk.astype(BF16), w_mv.astype(BF16),
      g_mk.reshape(1, MEM_HEAD_DIM))


def _inproj_kernel(x_ref, g_ref, w_ref, gq_ref, gk_ref, *refs, qk_cols, emit_t):
    if emit_t:
        qt_ref, kb_ref, vt_ref, kt_ref, v4_ref, c_ref = refs
    else:
        seg_ref, qb_ref, k_ref, v_ref, c_ref = refs

    def mapnorm(z, g):
        ms = jnp.dot((z * z).astype(BF16), seg_ref[...], preferred_element_type=F32)
        return z * lax.rsqrt(ms + EPS) * g

    def mapnorm_t(z, g_t):
        zt = z.T
        z3 = zt.reshape(zt.shape[0] // HEAD_DIM, HEAD_DIM, zt.shape[1])
        ms = jnp.mean(z3 * z3, axis=1, keepdims=True)
        return (z3 * lax.rsqrt(ms + EPS)).reshape(zt.shape) * g_t

    h = _rms(x_ref[0], g_ref[...]).astype(BF16)

    def proj(lo, hi):
        return jnp.dot(h, w_ref[:, lo:hi], preferred_element_type=F32)

    c0 = qk_cols
    zq = proj(0, c0)
    zk = proj(c0, 2 * c0)
    v = proj(2 * c0, 3 * c0)
    a = proj(3 * c0, 4 * c0)
    gate = proj(4 * c0, 5 * c0)
    c_ref[0] = a * jax.nn.sigmoid(gate)
    if emit_t:
        kt = mapnorm_t(zk, gk_ref[...])
        kt_ref[0] = kt
        kb_ref[0] = kt.T.astype(BF16)
        tm = x_ref.shape[1]
        for hh in range(N_HEADS):
            v4_ref[0, pl.ds(hh, tm, stride=N_HEADS), :] = v[:, hh * V_DIM:(hh + 1) * V_DIM]
        qt = mapnorm_t(zq, gq_ref[...]).astype(BF16)
        vt = v.T.astype(BF16)
        for j in range(qt_ref.shape[1]):
            qt_ref[0, j] = qt[:, j * ATTN_TQ:(j + 1) * ATTN_TQ]
        ones_row = (lax.broadcasted_iota(jnp.int32, (VT_ROWS - V_DIM, ATTN_TK), 0) == 0).astype(BF16)
        for j in range(vt_ref.shape[1]):
            for hh in range(N_HEADS):
                vt_ref[0, j, hh, :V_DIM, :] = vt[hh * V_DIM:(hh + 1) * V_DIM,
                                                 j * ATTN_TK:(j + 1) * ATTN_TK]
                vt_ref[0, j, hh, V_DIM:, :] = ones_row
    else:
        k_ref[0] = mapnorm(zk, gk_ref[...])
        v_ref[0] = v
        qb_ref[0] = mapnorm(zq, gq_ref[...]).astype(BF16)


def _inproj(x, g_mix, w_in_bf, gq_row, gk_row, seg, *, emit_t):
    b, t, d = x.shape
    cols = gq_row.shape[1]
    tm = min(INPROJ_TILE if emit_t else ROW_TILE, t)
    nt = t // tm
    row = lambda width: pl.BlockSpec((1, tm, width), lambda bi, ti: (bi, ti, 0))
    f32_out = jax.ShapeDtypeStruct((b, t, cols), F32)
    if emit_t:
        nq, nk = tm // ATTN_TQ, tm // ATTN_TK
        out_shape = [jax.ShapeDtypeStruct((b, t // ATTN_TQ, cols, ATTN_TQ), BF16),
                     jax.ShapeDtypeStruct((b, t, cols), BF16),
                     jax.ShapeDtypeStruct((b, t // ATTN_TK, N_HEADS, VT_ROWS, ATTN_TK), BF16),
                     jax.ShapeDtypeStruct((b, cols, t), F32),
                     jax.ShapeDtypeStruct((b, t * N_HEADS, V_DIM), F32),
                     f32_out]
        out_specs = [pl.BlockSpec((1, nq, cols, ATTN_TQ), lambda bi, ti: (bi, ti, 0, 0)),
                     row(cols),
                     pl.BlockSpec((1, nk, N_HEADS, VT_ROWS, ATTN_TK),
                                  lambda bi, ti: (bi, ti, 0, 0, 0)),
                     pl.BlockSpec((1, cols, tm), lambda bi, ti: (bi, 0, ti)),
                     pl.BlockSpec((1, tm * N_HEADS, V_DIM), lambda bi, ti: (bi, ti, 0)),
                     row(cols)]
        gains = [jnp.broadcast_to(g.reshape(cols, 1), (cols, tm)) for g in (gq_row, gk_row)]
        gain_specs = [_const_spec((cols, tm))] * 2
    else:
        out_shape = [jax.ShapeDtypeStruct((b, t, cols), BF16)] + [f32_out] * 3
        out_specs = [row(cols)] * 4
        gains = [gq_row, gk_row, seg]
        gain_specs = [_const_spec((1, cols)), _const_spec((1, cols)), _const_spec((cols, cols))]
    return pl.pallas_call(
        functools.partial(_inproj_kernel, qk_cols=cols, emit_t=emit_t),
        grid=(b, nt),
        in_specs=[row(d), _const_spec((1, d)), _const_spec(w_in_bf.shape)] + gain_specs,
        out_specs=out_specs,
        out_shape=out_shape,
        compiler_params=_cparams(("parallel", "parallel")),
        name="inproj",
    )(x, g_mix.reshape(1, d), w_in_bf, *gains)


PREV, DIAG0, DIAG1, FAR = 0, 1, 2, None


def _build_qbd(qt_ref, qbd_ref):
    w = ATTN_TK
    upper = lax.broadcasted_iota(jnp.int32, (HEAD_COLS, w), 0) < HEAD_DIM
    for h in range(N_HEADS):
        for half in range(ATTN_TQ // w):
            qh = qt_ref[0, 0, h * HEAD_COLS:(h + 1) * HEAD_COLS, half * w:(half + 1) * w]
            zero = jnp.zeros_like(qh)
            qbd_ref[h, :, 2 * half * w:(2 * half + 1) * w] = jnp.where(upper, qh, zero)
            qbd_ref[h, :, (2 * half + 1) * w:(2 * half + 2) * w] = jnp.where(upper, zero, qh)


def _attn_finalize(acc_ref, denom, lam, gsub_ref, o_ref):
    w = ATTN_TK
    for h in range(N_HEADS):
        acc = acc_ref[h, :V_DIM, :] * (1.0 / denom(h))
        for half in range(ATTN_TQ // w):
            m0 = acc[:, 2 * half * w:(2 * half + 1) * w]
            m1 = acc[:, (2 * half + 1) * w:(2 * half + 2) * w]
            ot = m0 - lam * m1
            ms = jnp.mean(ot * ot, axis=0, keepdims=True)
            ot = ot * lax.rsqrt(ms + EPS)
            o_ref[0, half * w:(half + 1) * w, h * V_DIM:(h + 1) * V_DIM] = (
                ot.T * gsub_ref[...]).astype(BF16)


def _attn_prompt_kernel(sc_ref, qt_ref, kb_ref, vt_ref, bias_ref, gsub_ref, o_ref,
                        qbd_ref, acc_ref, sa_ref, sb_ref, *stat_refs, bounded):
    i = pl.program_id(1)
    tq, tk = ATTN_TQ, ATTN_TK
    lam = sc_ref[N_HEADS]
    if bounded:
        lsum_ref, = stat_refs
        slot_a, slot_b = (sa_ref, None), (sb_ref, None)
        lsum_ref[...] = jnp.zeros(lsum_ref.shape, F32)
    else:
        m_ref, mca_ref, mcb_ref = stat_refs
        slot_a, slot_b = (sa_ref, mca_ref), (sb_ref, mcb_ref)
        m_ref[...] = jnp.full(m_ref.shape, NEG_INF, F32)

    _build_qbd(qt_ref, qbd_ref)
    acc_ref[...] = jnp.zeros(acc_ref.shape, F32)

    every = slice(0, 2 * tq)

    def stage(j, slot, tile, cols=every):
        s_ref, mc_ref = slot
        row0 = pl.multiple_of(j * tk, tk)
        width = cols.stop - cols.start
        for h in range(N_HEADS):
            kh = kb_ref[0, pl.ds(row0, tk), h * HEAD_COLS:(h + 1) * HEAD_COLS]
            s = jnp.dot(kh, qbd_ref[h, :, cols], preferred_element_type=F32)
            if tile is not FAR:
                s = s + bias_ref[h, tile, :, cols]
            if bounded:
                p = jnp.exp2(s)
                s_ref[h, :, cols] = p.astype(BF16)
                lsum_ref[h, :, cols] += jnp.sum(p.reshape(tk // SUBLANES, SUBLANES, width), axis=0)
            else:
                s_ref[h, :, cols] = s
                mc_ref[h, :, cols] = jnp.max(s, axis=0, keepdims=True)

    def consume(j, slot, tile, cols=every):
        s_ref, mc_ref = slot
        for h in range(N_HEADS):
            if bounded:
                acc_ref[h, :V_DIM, cols] += jnp.dot(vt_ref[0, j, h, :V_DIM, :], s_ref[h, :, cols],
                                                    preferred_element_type=F32)
                continue
            m_prev = m_ref[h, :, cols]
            if tile is FAR:
                shift = sc_ref[h]
                m_new = jnp.maximum(m_prev, mc_ref[h, :, cols] + shift)
                p = jnp.exp2(s_ref[h, :, cols] - (m_new - shift))
            else:
                m_new = jnp.maximum(m_prev, mc_ref[h, :, cols])
                p = jnp.exp2(s_ref[h, :, cols] - m_new)
            alpha = jnp.exp2(m_prev - m_new)
            pv = jnp.dot(vt_ref[0, j, h], p.astype(BF16), preferred_element_type=F32)
            acc_ref[h, :, cols] = alpha * acc_ref[h, :, cols] + pv
            m_ref[h, :, cols] = m_new

    second_half = slice(tq, 2 * tq)
    stage(2 * i + 1, slot_a, DIAG1, second_half)
    consume(2 * i + 1, slot_a, DIAG1, second_half)
    stage(2 * i, slot_b, DIAG0)

    @pl.when(i == 0)
    def _():
        consume(2 * i, slot_b, DIAG0)

    @pl.when(i >= 1)
    def _():
        consume(2 * i, slot_b, DIAG0)
        stage(2 * i - 1, slot_a, PREV)
        consume(2 * i - 1, slot_a, PREV)
        if bounded:
            for h in range(N_HEADS):
                acc_ref[h] = acc_ref[h] * sc_ref[N_HEADS + 1 + h]
                lsum_ref[h] = lsum_ref[h] * sc_ref[N_HEADS + 1 + h]
        stage(0, slot_b, FAR)

    def pair_body(jj, carry):
        f = 2 * jj
        consume(f, slot_b, FAR)
        stage(f + 1, slot_a, FAR)
        consume(f + 1, slot_a, FAR)
        stage(f + 2, slot_b, FAR)
        return carry

    lax.fori_loop(0, jnp.maximum(i - 1, 0), pair_body, 0)

    @pl.when(i >= 1)
    def _():
        consume(2 * i - 2, slot_b, FAR)

    if bounded:
        denom = lambda h: jnp.sum(lsum_ref[h], axis=0, keepdims=True)
    else:
        denom = lambda h: acc_ref[h, V_DIM:V_DIM + 1, :]
    _attn_finalize(acc_ref, denom, lam, gsub_ref, o_ref)


def _attn_prompt(scalars, score_bound, qt, kb, vt, bias_t, gsub_row):
    b, nq, cols, tq = qt.shape
    t = kb.shape[1]
    nk = vt.shape[1]
    qbd = pltpu.VMEM((N_HEADS, HEAD_COLS, 2 * tq), BF16)
    acc = pltpu.VMEM((N_HEADS, VT_ROWS, 2 * tq), F32)
    stat = pltpu.VMEM((N_HEADS, 1, 2 * tq), F32)
    scores = pltpu.VMEM((N_HEADS, ATTN_TK, 2 * tq), F32)
    probs = pltpu.VMEM((N_HEADS, ATTN_TK, 2 * tq), BF16)
    colsum = pltpu.VMEM((N_HEADS, SUBLANES, 2 * tq), F32)

    def call(bounded, scratch, name):
        return pl.pallas_call(
            functools.partial(_attn_prompt_kernel, bounded=bounded),
            grid=(b, nq),
            in_specs=[pl.BlockSpec(memory_space=pltpu.SMEM),
                      pl.BlockSpec((1, 1, cols, tq), lambda bi, i: (bi, i, 0, 0)),
                      pl.BlockSpec((1, t, cols), lambda bi, i: (bi, 0, 0)),
                      pl.BlockSpec((1, nk, N_HEADS, VT_ROWS, ATTN_TK),
                                   lambda bi, i: (bi, 0, 0, 0, 0)),
                      _const_spec(bias_t.shape), _const_spec((1, V_DIM))],
            out_specs=pl.BlockSpec((1, tq, cols), lambda bi, i: (bi, i, 0)),
            out_shape=jax.ShapeDtypeStruct((b, t, cols), BF16),
            scratch_shapes=scratch,
            compiler_params=_cparams(("parallel", "arbitrary")),
            name=name,
        )(scalars, qt, kb, vt, bias_t, gsub_row)

    return lax.cond(
        score_bound <= MAX_UNSTABILISED_SCORE,
        lambda: call(True, [qbd, acc, probs, probs, colsum], "attn_prompt_bounded"),
        lambda: call(False, [qbd, acc, scores, scores, stat, stat, stat], "attn_prompt"))


def _attn_sample_kernel(sc_ref, q_ref, kt_ref, vc_ref, ktn_ref, vn_ref, bc_ref, bn_ref, gsub_ref,
                        o_ref, qbd_ref, m_ref, l_ref, acc_ref):
    j = pl.program_id(1)
    tq, cols = q_ref.shape[1], q_ref.shape[2]
    lam = sc_ref[N_HEADS]

    @pl.when(j == 0)
    def _():
        q = q_ref[0].astype(F32)
        col = lax.broadcasted_iota(jnp.int32, (tq, cols), 1)
        for hm in range(cols // HEAD_DIM):
            mine = (col >= hm * HEAD_DIM) & (col < (hm + 1) * HEAD_DIM)
            qbd_ref[hm * tq:(hm + 1) * tq, :] = jnp.where(mine, q, 0.0)
        m_ref[...] = jnp.full(m_ref.shape, NEG_INF, F32)
        l_ref[...] = jnp.zeros(l_ref.shape, F32)
        acc_ref[...] = jnp.zeros(acc_ref.shape, F32)

    def update(kt, v_of_head, bias):
        s = jnp.dot(qbd_ref[...].astype(BF16), kt, preferred_element_type=F32) + bias
        m_prev = m_ref[...]
        m_new = jnp.maximum(m_prev, jnp.max(s, axis=-1, keepdims=True))
        alpha = jnp.exp2(m_prev - m_new)
        p = jnp.exp2(s - m_new)
        l_ref[...] = alpha * l_ref[...] + jnp.sum(p, axis=-1, keepdims=True)
        pb = p.astype(BF16)
        for h in range(N_HEADS):
            rows = slice(h * 2 * tq, (h + 1) * 2 * tq)
            acc_ref[rows, :] = alpha[rows] * acc_ref[rows, :] + jnp.dot(
                pb[rows], v_of_head(h), preferred_element_type=F32)
        m_ref[...] = m_new

    tk = kt_ref.shape[2]
    update(kt_ref[0].astype(BF16),
           lambda h: vc_ref[0, pl.ds(h, tk, stride=N_HEADS), :].astype(BF16), bc_ref[j])

    @pl.when(j == pl.num_programs(1) - 1)
    def _():
        update(ktn_ref[0], lambda h: vn_ref[0, :, h * V_DIM:(h + 1) * V_DIM], bn_ref[...])
        for h in range(N_HEADS):
            r0 = h * 2 * tq
            inv0 = 1.0 / l_ref[r0:r0 + tq, :]
            inv1 = 1.0 / l_ref[r0 + tq:r0 + 2 * tq, :]
            o = acc_ref[r0:r0 + tq, :] * inv0 - lam * (acc_ref[r0 + tq:r0 + 2 * tq, :] * inv1)
            o_ref[0, :, h * V_DIM:(h + 1) * V_DIM] = _rms(o, gsub_ref[...]).astype(BF16)


def _attn_sample(scalars, q, kt, v4, ktn, vn, bias_c, bias_n, gsub_row):
    b, tq, cols = q.shape
    p = kt.shape[2]
    nkc = p // CACHE_TK
    rows = 2 * N_HEADS * tq
    return pl.pallas_call(
        _attn_sample_kernel,
        grid=(b, nkc),
        in_specs=[pl.BlockSpec(memory_space=pltpu.SMEM),
                  pl.BlockSpec((1, tq, cols), lambda bi, j: (bi, 0, 0)),
                  pl.BlockSpec((1, cols, CACHE_TK), lambda bi, j: (bi, 0, j)),
                  pl.BlockSpec((1, CACHE_TK * N_HEADS, V_DIM), lambda bi, j: (bi, j, 0)),
                  pl.BlockSpec((1, cols, NEW_KEY_PAD), lambda bi, j: (bi, 0, 0)),
                  pl.BlockSpec((1, NEW_KEY_PAD, cols), lambda bi, j: (bi, 0, 0)),
                  _const_spec(bias_c.shape), _const_spec(bias_n.shape), _const_spec((1, V_DIM))],
        out_specs=pl.BlockSpec((1, tq, cols), lambda bi, j: (bi, 0, 0)),
        out_shape=jax.ShapeDtypeStruct((b, tq, cols), BF16),
        scratch_shapes=[pltpu.VMEM((rows, cols), F32),
                        pltpu.VMEM((rows, 1), F32),
                        pltpu.VMEM((rows, 1), F32),
                        pltpu.VMEM((rows, V_DIM), F32)],
        compiler_params=_cparams(("parallel", "arbitrary")),
        name="attn_sample",
    )(scalars, q, kt, v4, ktn, vn, bias_c, bias_n, gsub_row)


def _mix_kernel(x_ref, o_ref, c_ref, hist_ref, wconv_ref, bconv_ref, lng_ref, lnb_ref,
                woo_ref, woc_ref, gcross_ref, wmq_ref, gmq_ref, mk_ref, mv_ref, wmo_ref,
                out_ref, ext_ref, xs_ref, cv_ref, ca_ref):
    bb, tt, d = x_ref.shape
    rows = bb * tt
    n_mem = mk_ref.shape[1] // MEM_HEADS
    rc = min(CONV_ROWS, tt)
    lead = CONV_HALO - (CONV_K - 1)

    for b in range(bb):
        ext_ref[b, :CONV_HALO, :] = hist_ref[b, 0]
        ext_ref[b, CONV_HALO:, :] = c_ref[b]
    span = xs_ref.shape[2]
    for b in range(bb):
        for r in range(1, SUBLANES):
            xs_ref[r - 1, b] = ext_ref[b, r:r + span, :]
    for b in range(bb):
        for r0 in range(0, tt, rc):
            acc = jnp.zeros((rc, c_ref.shape[2]), F32) + bconv_ref[...]
            for k in range(CONV_K):
                a, r = divmod(k + lead, SUBLANES)
                lo = r0 + SUBLANES * a
                src = ext_ref[b, lo:lo + rc, :] if r == 0 else xs_ref[r - 1, b, lo:lo + rc, :]
                acc = acc + wconv_ref[k:k + 1, :] * src
            mu = jnp.mean(acc, axis=-1, keepdims=True)
            xc = acc - mu
            var = jnp.mean(xc * xc, axis=-1, keepdims=True)
            y = xc * lax.rsqrt(var + EPS) * lng_ref[...] + lnb_ref[...]
            cv_ref[b * tt + r0:b * tt + r0 + rc, :] = (y * jax.nn.sigmoid(y)).astype(BF16)

    x = x_ref[...].reshape(rows, d)
    o = o_ref[...].reshape(rows, o_ref.shape[2])
    x1 = (x + jnp.dot(o, woo_ref[...], preferred_element_type=F32)
          + jnp.dot(cv_ref[...], woc_ref[...], preferred_element_type=F32))

    hc = _rms(x1, gcross_ref[...]).astype(BF16)
    qm = jnp.dot(hc, wmq_ref[...], preferred_element_type=F32)
    for h in range(MEM_HEADS):
        cols = slice(h * MEM_HEAD_DIM, (h + 1) * MEM_HEAD_DIM)
        qn = _rms(qm[:, cols], gmq_ref[...]).astype(BF16)
        for b in range(bb):
            mem_rows = pl.ds(h, n_mem, stride=MEM_HEADS)
            mk = mk_ref[b, mem_rows, :].astype(BF16)
            mv = mv_ref[b, mem_rows, :].astype(BF16)
            s = lax.dot_general(qn[b * tt:(b + 1) * tt], mk, (((1,), (1,)), ((), ())),
                                preferred_element_type=F32)
            p = jnp.exp2(s - jnp.max(s, axis=-1, keepdims=True))
            l = jnp.sum(p, axis=-1, keepdims=True)
            oh = jnp.dot(p.astype(BF16), mv, preferred_element_type=F32) / l
            ca_ref[b * tt:(b + 1) * tt, cols] = oh.astype(BF16)
    x2 = x1 + jnp.dot(ca_ref[...], wmo_ref[...], preferred_element_type=F32)
    out_ref[...] = x2.reshape(bb, tt, d)


def _mix(x, o, c, hist, wconv, bconv, lng, lnb, woo, woc, gcross, wmq, gmq_row, mk, mv, wmo,
         *, bb, tt):
    b, t, d = x.shape
    cw = c.shape[2]
    mem_rows, mhd = mk.shape[1], mk.shape[2]
    mw = wmq.shape[1]
    rows = bb * tt
    tile = lambda width: pl.BlockSpec((bb, tt, width), lambda bi, ti: (bi, ti, 0))
    return pl.pallas_call(
        _mix_kernel,
        grid=(b // bb, t // tt),
        in_specs=[tile(d), tile(cw), tile(cw),
                  pl.BlockSpec((bb, 1, CONV_HALO, cw), lambda bi, ti: (bi, ti, 0, 0)),
                  _const_spec(wconv.shape), _const_spec((1, cw)), _const_spec((1, cw)),
                  _const_spec((1, cw)), _const_spec(woo.shape), _const_spec(woc.shape),
                  _const_spec((1, d)), _const_spec(wmq.shape), _const_spec((1, MEM_HEAD_DIM)),
                  pl.BlockSpec((bb, mem_rows, mhd), lambda bi, ti: (bi, 0, 0)),
                  pl.BlockSpec((bb, mem_rows, mhd), lambda bi, ti: (bi, 0, 0)),
                  _const_spec(wmo.shape)],
        out_specs=tile(d),
        out_shape=jax.ShapeDtypeStruct((b, t, d), F32),
        scratch_shapes=[pltpu.VMEM((bb, CONV_HALO + tt, cw), F32),
                        pltpu.VMEM((SUBLANES - 1, bb, CONV_HALO - SUBLANES + tt, cw), F32),
                        pltpu.VMEM((rows, cw), BF16),
                        pltpu.VMEM((rows, mw), BF16)],
        compiler_params=_cparams(("parallel", "parallel")),
        name="mix",
    )(x, o, c, hist, wconv, bconv, lng, lnb, woo, woc, gcross, wmq, gmq_row, mk, mv, wmo)


def _mlp_kernel(x_ref, g_ref, w1_ref, w2_ref, out_ref):
    x = x_ref[...]
    hf = _rms(x, g_ref[...]).astype(BF16)
    dff = w1_ref.shape[1]
    part = min(dff, MLP_HIDDEN_PART)
    y = x
    for lo in range(0, dff, part):
        u = jnp.maximum(jnp.dot(hf, w1_ref[:, lo:lo + part], preferred_element_type=F32), 0.0)
        y = y + jnp.dot((u * u).astype(BF16), w2_ref[lo:lo + part, :], preferred_element_type=F32)
    out_ref[...] = y


def _mlp(x2d, g_ffn, w1, w2):
    n, d = x2d.shape
    tm = min(MLP_TILE, n)
    return pl.pallas_call(
        _mlp_kernel,
        grid=(n // tm,),
        in_specs=[pl.BlockSpec((tm, d), lambda i: (i, 0)), _const_spec((1, d)),
                  _const_spec(w1.shape), _const_spec(w2.shape)],
        out_specs=pl.BlockSpec((tm, d), lambda i: (i, 0)),
        out_shape=jax.ShapeDtypeStruct((n, d), F32),
        compiler_params=_cparams(("parallel",)),
        name="mlp",
    )(x2d, g_ffn.reshape(1, d), w1, w2)


def _rel_bucket(rel):
    half = N_BUCKETS // 2
    max_exact = half // 2
    ret = jnp.where(rel > 0, half, 0)
    n = jnp.abs(rel)
    nf = jnp.maximum(n, 1).astype(jnp.float32)
    large = max_exact + (jnp.log(nf / max_exact) / math.log(MAX_DISTANCE / max_exact)
                         * (half - max_exact)).astype(jnp.int32)
    large = jnp.minimum(large, half - 1)
    return ret + jnp.where(n < max_exact, n, large)


def _masked_bias(rel_table, q_pos, k_pos):
    bucket = _rel_bucket(k_pos[None, :] - q_pos[:, None])[None]
    table = rel_table.astype(F32) * LOG2E
    bias = jnp.zeros((rel_table.shape[1],) + bucket.shape[1:], F32)
    for bkt in range(N_BUCKETS):
        bias = jnp.where(bucket == bkt, table[bkt][:, None, None], bias)
    mask = (k_pos[None, :] // CHUNK) <= (q_pos[:, None] // CHUNK)
    return jnp.where(mask[None], bias, NEG_INF)


def _far_bucket_is_saturated(min_distance):
    half = N_BUCKETS // 2
    max_exact = half // 2
    large = max_exact + int(np.log(min_distance / max_exact) / math.log(MAX_DISTANCE / max_exact)
                            * (half - max_exact) * (1 - 1e-6))
    return large >= half - 1


def _layer(x, k_past, v_past, c_past, mk, mv, w, lam, lam_init):
    b, t, d = x.shape
    prompt = k_past is None
    cols = N_HEADS * HEAD_COLS
    gsub_row = (w["g_sub"] * (1.0 - lam_init)).reshape(1, V_DIM)

    if prompt:
        assert t % ROW_TILE == 0 and t % INPROJ_TILE == 0 and INPROJ_TILE % ATTN_TQ == 0
        assert ROW_TILE % ATTN_TQ == 0 and ATTN_TQ == 2 * ATTN_TK
        assert ATTN_TK % CHUNK == 0 and _far_bucket_is_saturated(ATTN_TK + 1)
        qt, kb, vt, kt, v4, c = _inproj(x, w["g_mix"], w["w_in"], w["gq_row"], w["gk_row"],
                                        w["seg"], emit_t=True)
        k_out = jnp.transpose(kt.reshape(b, N_HEADS, 2, HEAD_DIM, t), (0, 4, 1, 2, 3))
        v_out = v4.reshape(b, t, N_HEADS, V_DIM)
        q_pos = ATTN_TQ + jnp.arange(ATTN_TQ, dtype=jnp.int32)
        k_pos = ATTN_TK + jnp.arange(3 * ATTN_TK, dtype=jnp.int32)
        near = _masked_bias(w["rel_table"], q_pos, k_pos)
        near = jnp.stack([near[:, :, n * ATTN_TK:(n + 1) * ATTN_TK] for n in range(3)], axis=1)
        near_t = jnp.swapaxes(near, 2, 3)
        halves = [near_t[..., n * ATTN_TK:(n + 1) * ATTN_TK] for n in range(ATTN_TQ // ATTN_TK)]
        bias_t = jnp.concatenate([hf for hf in halves for _ in range(2)], axis=3)
        far = w["rel_table"][_rel_bucket(jnp.int32(-(ATTN_TK + 1)))].astype(F32) * LOG2E
        scalars = jnp.concatenate([far, lam.reshape(1), jnp.exp2(-far)]).astype(F32)
        score_bound = (HEAD_DIM * jnp.max(jnp.abs(w["gq_row"])) * jnp.max(jnp.abs(w["gk_row"]))
                       + LOG2E * jnp.max(jnp.abs(w["rel_table"])))
        o = _attn_prompt(scalars, score_bound, qt, kb, vt, bias_t, gsub_row)
        bb, tt = 1, MIX_TILE
    else:
        p = k_past.shape[2]
        assert p % CACHE_TK == 0 and t <= NEW_KEY_PAD and t % 16 == 0
        qb, k, v, c = _inproj(x.reshape(1, b * t, d), w["g_mix"], w["w_in"], w["gq_row"],
                              w["gk_row"], w["seg"], emit_t=False)
        qb, k, v, c = (a.reshape(b, t, cols) for a in (qb, k, v, c))
        k_out = k.reshape(b, t, N_HEADS, 2, HEAD_DIM)
        v_out = v.reshape(b, t, N_HEADS, V_DIM)
        ktn = jnp.pad(jnp.swapaxes(k, 1, 2).astype(BF16), ((0, 0), (0, 0), (0, NEW_KEY_PAD - t)))
        vn = jnp.pad(v.astype(BF16), ((0, 0), (0, NEW_KEY_PAD - t), (0, 0)))
        q_pos = p + jnp.arange(t, dtype=jnp.int32)
        bias = _masked_bias(w["rel_table"], q_pos, jnp.arange(p + NEW_KEY_PAD, dtype=jnp.int32))
        bias = jnp.where(jnp.arange(p + NEW_KEY_PAD) < p + t, bias, NEG_INF)
        rows = 2 * N_HEADS * t
        bias = jnp.broadcast_to(bias[:, None], (N_HEADS, 2, t, p + NEW_KEY_PAD)).reshape(rows, -1)
        bias_c = jnp.swapaxes(bias[:, :p].reshape(rows, p // CACHE_TK, CACHE_TK), 0, 1)
        scalars = jnp.concatenate([jnp.zeros((N_HEADS,), F32), lam.reshape(1)]).astype(F32)
        o = _attn_sample(scalars, qb, k_past, v_past, ktn, vn, bias_c, bias[:, p:], gsub_row)
        bb, tt = SAMPLE_BATCH_TILE, t
        assert b % bb == 0

    nt = t // tt
    first = jnp.pad(c_past, ((0, 0), (CONV_HALO - (CONV_K - 1), 0), (0, 0)))[:, None]
    if nt > 1:
        tails = c.reshape(b, nt, tt, c.shape[2])[:, :-1, tt - CONV_HALO:, :]
        hist = jnp.concatenate([first, tails], axis=1)
    else:
        hist = first
    x2 = _mix(x, o, c, hist, w["w_conv"], w["b_conv"], w["ln_g"], w["ln_b"], w["w_out_o"],
              w["w_out_c"], w["g_cross"], w["w_mq"], w["gmq_row"], mk, mv, w["w_mo"], bb=bb, tt=tt)
    y = _mlp(x2.reshape(b * t, d), w["g_ffn"], w["w_ff1"], w["w_ff2"]).reshape(b, t, d)
    if t >= CONV_K - 1:
        c_hist_tail = c[:, t - (CONV_K - 1):]
    else:
        c_hist_tail = jnp.concatenate([c_past[:, t:], c], axis=1)
    return y, k_out, v_out, c_hist_tail


def kernel(x_prompt, x_sample, cache_k, cache_v, cache_conv, cache_mem_k, cache_mem_v, mem_prompt,
           rel_table, g_mix, w_in, g_q, g_k, lam_vec, g_sub, w_conv, b_conv, ln_g, ln_b, w_out,
           g_cross, g_mem, w_mq, w_mk, w_mv, g_mq, g_mk, w_mo, g_ffn, w_ff1, w_ff2):
    depth = g_mix.shape[0]
    assert depth == 1
    b, t, d = x_prompt.shape
    bs, ts, _ = x_sample.shape
    cols = N_HEADS * HEAD_COLS
    cw = w_conv.shape[2]
    attn_w = N_HEADS * V_DIM
    l = 0
    lam_init = 0.8 - 0.6 * math.exp(-0.3 * l)
    lp = lam_vec[l].astype(F32)
    lam = jnp.exp(jnp.sum(lp[0] * lp[1])) - jnp.exp(jnp.sum(lp[2] * lp[3])) + lam_init

    seg = jnp.kron(jnp.eye(cols // HEAD_DIM, dtype=F32),
                   jnp.full((HEAD_DIM, HEAD_DIM), 1.0 / HEAD_DIM, F32)).astype(BF16)
    n_maps = cols // HEAD_DIM
    w = dict(
        rel_table=rel_table, g_mix=g_mix[l], w_in=w_in[l].astype(BF16), seg=seg,
        gq_row=jnp.tile(g_q[l] * (HEAD_DIM ** -0.5 * LOG2E), n_maps).reshape(1, cols),
        gk_row=jnp.tile(g_k[l], n_maps).reshape(1, cols),
        g_sub=g_sub[l],
        w_conv=jnp.pad(w_conv[l], ((0, CONV_HALO - CONV_K), (0, 0))),
        b_conv=b_conv[l].reshape(1, cw), ln_g=ln_g[l].reshape(1, cw), ln_b=ln_b[l].reshape(1, cw),
        w_out_o=w_out[l][:attn_w].astype(BF16), w_out_c=w_out[l][attn_w:].astype(BF16),
        g_cross=g_cross[l].reshape(1, d), w_mq=w_mq[l].astype(BF16),
        gmq_row=(g_mq[l] * (MEM_HEAD_DIM ** -0.5 * LOG2E)).reshape(1, MEM_HEAD_DIM),
        w_mo=w_mo[l].astype(BF16), g_ffn=g_ffn[l],
        w_ff1=w_ff1[l].astype(BF16), w_ff2=w_ff2[l].astype(BF16),
    )

    n_mem = mem_prompt.shape[1]
    mk_p, mv_p = _memkv(mem_prompt.reshape(b * n_mem, d), g_mem[l], w_mk[l], w_mv[l], g_mk[l])
    mk_p = mk_p.reshape(b, n_mem * MEM_HEADS, MEM_HEAD_DIM)
    mv_p = mv_p.reshape(b, n_mem * MEM_HEADS, MEM_HEAD_DIM)

    zero_conv = jnp.zeros((b, CONV_K - 1, cw), F32)
    yp, kp, vp, cp = _layer(x_prompt, None, None, zero_conv, mk_p, mv_p, w, lam, lam_init)

    p = cache_k.shape[2]
    k_past = jnp.transpose(cache_k[l], (0, 2, 3, 4, 1)).reshape(bs, cols, p)
    v_past = cache_v[l].reshape(bs, p * N_HEADS, V_DIM)
    ys, kn, vn, cn = _layer(x_sample, k_past, v_past, cache_conv[l],
                            cache_mem_k[l].reshape(bs, n_mem * MEM_HEADS, MEM_HEAD_DIM),
                            cache_mem_v[l].reshape(bs, n_mem * MEM_HEADS, MEM_HEAD_DIM),
                            w, lam, lam_init)

    return (yp, ys, kp[None], vp[None], cp[None],
            mk_p.reshape(1, b, n_mem, MEM_HEADS, MEM_HEAD_DIM),
            mv_p.reshape(1, b, n_mem, MEM_HEADS, MEM_HEAD_DIM),
            kn[None], vn[None], cn[None])
```

```python
import functools
import math

import jax
import jax.numpy as jnp
import numpy as np
from jax import lax
from jax.experimental import pallas as pl
from jax.experimental.pallas import tpu as pltpu

F32 = jnp.float32
BF16 = jnp.bfloat16

CHUNK = 64
N_HEADS = 4
HEAD_DIM = 64
V_DIM = 2 * HEAD_DIM
HEAD_COLS = 2 * HEAD_DIM
CONV_K = 31
CONV_HALO = 32
N_BUCKETS = 32
MAX_DISTANCE = 128
MEM_HEADS = 4
MEM_HEAD_DIM = 128
EPS = 1e-6
NEG_INF = -1e30
LOG2E = math.log2(math.e)
SUBLANES = 8

ATTN_TQ = 512
ATTN_TK = 256
VT_ROWS = V_DIM + 16
MAX_UNSTABILISED_SCORE = 100.0
CACHE_TK = 4096
NEW_KEY_PAD = 128
ROW_TILE = 512
INPROJ_TILE = 1024
MIX_TILE = 1024
MLP_TILE = 1024
MLP_HIDDEN_PART = 2048
CONV_ROWS = 64
SAMPLE_BATCH_TILE = 8
VMEM_LIMIT = 56 * 1024 * 1024


def _cparams(sem):
    return pltpu.CompilerParams(dimension_semantics=sem, vmem_limit_bytes=VMEM_LIMIT)


def _rms(x, g):
    ms = jnp.mean(x * x, axis=-1, keepdims=True)
    return x * lax.rsqrt(ms + EPS) * g


def _const_spec(shape):
    return pl.BlockSpec(shape, lambda *_: (0,) * len(shape), pipeline_mode=pl.Buffered(1))


def _memkv_kernel(mem_ref, g_ref, wk_ref, wv_ref, gk_ref, mk_ref, mv_ref):
    m = _rms(mem_ref[...], g_ref[...]).astype(BF16)
    zk = jnp.dot(m, wk_ref[...], preferred_element_type=F32)
    zv = jnp.dot(m, wv_ref[...], preferred_element_type=F32)
    tm = mem_ref.shape[0]
    for h in range(MEM_HEADS):
        sl = slice(h * MEM_HEAD_DIM, (h + 1) * MEM_HEAD_DIM)
        rows = pl.ds(h, tm, stride=MEM_HEADS)
        mk_ref[rows, :] = _rms(zk[:, sl], gk_ref[...])
        mv_ref[rows, :] = zv[:, sl]


def _memkv(mem2d, g_mem, w_mk, w_mv, g_mk):
    n, d = mem2d.shape
    w = w_mk.shape[1]
    tm = min(ROW_TILE, n)
    return pl.pallas_call(
        _memkv_kernel,
        grid=(n // tm,),
        in_specs=[pl.BlockSpec((tm, d), lambda i: (i, 0)), _const_spec((1, d)),
                  _const_spec((d, w)), _const_spec((d, w)), _const_spec((1, MEM_HEAD_DIM))],
        out_specs=[pl.BlockSpec((tm * MEM_HEADS, MEM_HEAD_DIM), lambda i: (i, 0))] * 2,
        out_shape=[jax.ShapeDtypeStruct((n * MEM_HEADS, MEM_HEAD_DIM), F32)] * 2,
        compiler_params=_cparams(("parallel",)),
        name="memkv",
    )(mem2d, g_mem.reshape(1, d), w_mk.astype(BF16), w_mv.astype(BF16),
      g_mk.reshape(1, MEM_HEAD_DIM))


def _inproj_kernel(x_ref, g_ref, w_ref, gq_ref, gk_ref, *refs, qk_cols, emit_t):
    if emit_t:
        qt_ref, kb_ref, vt_ref, kt_ref, v4_ref, c_ref = refs
    else:
        seg_ref, qb_ref, k_ref, v_ref, c_ref = refs

    def mapnorm(z, g):
        ms = jnp.dot((z * z).astype(BF16), seg_ref[...], preferred_element_type=F32)
        return z * lax.rsqrt(ms + EPS) * g

    def mapnorm_t(z, g_t):
        zt = z.T
        z3 = zt.reshape(zt.shape[0] // HEAD_DIM, HEAD_DIM, zt.shape[1])
        ms = jnp.mean(z3 * z3, axis=1, keepdims=True)
        return (z3 * lax.rsqrt(ms + EPS)).reshape(zt.shape) * g_t

    h = _rms(x_ref[0], g_ref[...]).astype(BF16)

    def proj(lo, hi):
        return jnp.dot(h, w_ref[:, lo:hi], preferred_element_type=F32)

    c0 = qk_cols
    zq = proj(0, c0)
    zk = proj(c0, 2 * c0)
    v = proj(2 * c0, 3 * c0)
    a = proj(3 * c0, 4 * c0)
    gate = proj(4 * c0, 5 * c0)
    c_ref[0] = a * jax.nn.sigmoid(gate)
    if emit_t:
        kt = mapnorm_t(zk, gk_ref[...])
        kt_ref[0] = kt
        kb_ref[0] = kt.T.astype(BF16)
        tm = x_ref.shape[1]
        for hh in range(N_HEADS):
            v4_ref[0, pl.ds(hh, tm, stride=N_HEADS), :] = v[:, hh * V_DIM:(hh + 1) * V_DIM]
        qt = mapnorm_t(zq, gq_ref[...]).astype(BF16)
        vt = v.T.astype(BF16)
        for j in range(qt_ref.shape[1]):
            qt_ref[0, j] = qt[:, j * ATTN_TQ:(j + 1) * ATTN_TQ]
        ones_row = (lax.broadcasted_iota(jnp.int32, (VT_ROWS - V_DIM, ATTN_TK), 0) == 0).astype(BF16)
        for j in range(vt_ref.shape[1]):
            for hh in range(N_HEADS):
                vt_ref[0, j, hh, :V_DIM, :] = vt[hh * V_DIM:(hh + 1) * V_DIM,
                                                 j * ATTN_TK:(j + 1) * ATTN_TK]
                vt_ref[0, j, hh, V_DIM:, :] = ones_row
    else:
        k_ref[0] = mapnorm(zk, gk_ref[...])
        v_ref[0] = v
        qb_ref[0] = mapnorm(zq, gq_ref[...]).astype(BF16)


def _inproj(x, g_mix, w_in_bf, gq_row, gk_row, seg, *, emit_t):
    b, t, d = x.shape
    cols = gq_row.shape[1]
    tm = min(INPROJ_TILE if emit_t else ROW_TILE, t)
    nt = t // tm
    row = lambda width: pl.BlockSpec((1, tm, width), lambda bi, ti: (bi, ti, 0))
    f32_out = jax.ShapeDtypeStruct((b, t, cols), F32)
    if emit_t:
        nq, nk = tm // ATTN_TQ, tm // ATTN_TK
        out_shape = [jax.ShapeDtypeStruct((b, t // ATTN_TQ, cols, ATTN_TQ), BF16),
                     jax.ShapeDtypeStruct((b, t, cols), BF16),
                     jax.ShapeDtypeStruct((b, t // ATTN_TK, N_HEADS, VT_ROWS, ATTN_TK), BF16),
                     jax.ShapeDtypeStruct((b, cols, t), F32),
                     jax.ShapeDtypeStruct((b, t * N_HEADS, V_DIM), F32),
                     f32_out]
        out_specs = [pl.BlockSpec((1, nq, cols, ATTN_TQ), lambda bi, ti: (bi, ti, 0, 0)),
                     row(cols),
                     pl.BlockSpec((1, nk, N_HEADS, VT_ROWS, ATTN_TK),
                                  lambda bi, ti: (bi, ti, 0, 0, 0)),
                     pl.BlockSpec((1, cols, tm), lambda bi, ti: (bi, 0, ti)),
                     pl.BlockSpec((1, tm * N_HEADS, V_DIM), lambda bi, ti: (bi, ti, 0)),
                     row(cols)]
        gains = [jnp.broadcast_to(g.reshape(cols, 1), (cols, tm)) for g in (gq_row, gk_row)]
        gain_specs = [_const_spec((cols, tm))] * 2
    else:
        out_shape = [jax.ShapeDtypeStruct((b, t, cols), BF16)] + [f32_out] * 3
        out_specs = [row(cols)] * 4
        gains = [gq_row, gk_row, seg]
        gain_specs = [_const_spec((1, cols)), _const_spec((1, cols)), _const_spec((cols, cols))]
    return pl.pallas_call(
        functools.partial(_inproj_kernel, qk_cols=cols, emit_t=emit_t),
        grid=(b, nt),
        in_specs=[row(d), _const_spec((1, d)), _const_spec(w_in_bf.shape)] + gain_specs,
        out_specs=out_specs,
        out_shape=out_shape,
        compiler_params=_cparams(("parallel", "parallel")),
        name="inproj",
    )(x, g_mix.reshape(1, d), w_in_bf, *gains)


PREV, DIAG0, DIAG1, FAR = 0, 1, 2, None


def _build_qbd(qt_ref, qbd_ref):
    w = ATTN_TK
    upper = lax.broadcasted_iota(jnp.int32, (HEAD_COLS, w), 0) < HEAD_DIM
    for h in range(N_HEADS):
        for half in range(ATTN_TQ // w):
            qh = qt_ref[0, 0, h * HEAD_COLS:(h + 1) * HEAD_COLS, half * w:(half + 1) * w]
            zero = jnp.zeros_like(qh)
            qbd_ref[h, :, 2 * half * w:(2 * half + 1) * w] = jnp.where(upper, qh, zero)
            qbd_ref[h, :, (2 * half + 1) * w:(2 * half + 2) * w] = jnp.where(upper, zero, qh)


def _attn_finalize(acc_ref, denom, lam, gsub_ref, o_ref):
    w = ATTN_TK
    for h in range(N_HEADS):
        acc = acc_ref[h, :V_DIM, :] * (1.0 / denom(h))
        for half in range(ATTN_TQ // w):
            m0 = acc[:, 2 * half * w:(2 * half + 1) * w]
            m1 = acc[:, (2 * half + 1) * w:(2 * half + 2) * w]
            ot = m0 - lam * m1
            ms = jnp.mean(ot * ot, axis=0, keepdims=True)
            ot = ot * lax.rsqrt(ms + EPS)
            o_ref[0, half * w:(half + 1) * w, h * V_DIM:(h + 1) * V_DIM] = (
                ot.T * gsub_ref[...]).astype(BF16)


def _attn_prompt_kernel(sc_ref, qt_ref, kb_ref, vt_ref, bias_ref, gsub_ref, o_ref,
                        qbd_ref, acc_ref, sa_ref, sb_ref, *stat_refs, bounded):
    i = pl.program_id(1)
    tq, tk = ATTN_TQ, ATTN_TK
    lam = sc_ref[N_HEADS]
    if bounded:
        lsum_ref, = stat_refs
        slot_a, slot_b = (sa_ref, None), (sb_ref, None)
        lsum_ref[...] = jnp.zeros(lsum_ref.shape, F32)
    else:
        m_ref, mca_ref, mcb_ref = stat_refs
        slot_a, slot_b = (sa_ref, mca_ref), (sb_ref, mcb_ref)
        m_ref[...] = jnp.full(m_ref.shape, NEG_INF, F32)

    _build_qbd(qt_ref, qbd_ref)
    acc_ref[...] = jnp.zeros(acc_ref.shape, F32)

    every = slice(0, 2 * tq)

    def stage(j, slot, tile, cols=every):
        s_ref, mc_ref = slot
        row0 = pl.multiple_of(j * tk, tk)
        width = cols.stop - cols.start
        for h in range(N_HEADS):
            kh = kb_ref[0, pl.ds(row0, tk), h * HEAD_COLS:(h + 1) * HEAD_COLS]
            s = jnp.dot(kh, qbd_ref[h, :, cols], preferred_element_type=F32)
            if tile is not FAR:
                s = s + bias_ref[h, tile, :, cols]
            if bounded:
                p = jnp.exp2(s)
                s_ref[h, :, cols] = p.astype(BF16)
                lsum_ref[h, :, cols] += jnp.sum(p.reshape(tk // SUBLANES, SUBLANES, width), axis=0)
            else:
                s_ref[h, :, cols] = s
                mc_ref[h, :, cols] = jnp.max(s, axis=0, keepdims=True)

    def consume(j, slot, tile, cols=every):
        s_ref, mc_ref = slot
        for h in range(N_HEADS):
            if bounded:
                acc_ref[h, :V_DIM, cols] += jnp.dot(vt_ref[0, j, h, :V_DIM, :], s_ref[h, :, cols],
                                                    preferred_element_type=F32)
                continue
            m_prev = m_ref[h, :, cols]
            if tile is FAR:
                shift = sc_ref[h]
                m_new = jnp.maximum(m_prev, mc_ref[h, :, cols] + shift)
                p = jnp.exp2(s_ref[h, :, cols] - (m_new - shift))
            else:
                m_new = jnp.maximum(m_prev, mc_ref[h, :, cols])
                p = jnp.exp2(s_ref[h, :, cols] - m_new)
            alpha = jnp.exp2(m_prev - m_new)
            pv = jnp.dot(vt_ref[0, j, h], p.astype(BF16), preferred_element_type=F32)
            acc_ref[h, :, cols] = alpha * acc_ref[h, :, cols] + pv
            m_ref[h, :, cols] = m_new

    second_half = slice(tq, 2 * tq)
    stage(2 * i + 1, slot_a, DIAG1, second_half)
    consume(2 * i + 1, slot_a, DIAG1, second_half)
    stage(2 * i, slot_b, DIAG0)

    @pl.when(i == 0)
    def _():
        consume(2 * i, slot_b, DIAG0)

    @pl.when(i >= 1)
    def _():
        consume(2 * i, slot_b, DIAG0)
        stage(2 * i - 1, slot_a, PREV)
        consume(2 * i - 1, slot_a, PREV)
        if bounded:
            for h in range(N_HEADS):
                acc_ref[h] = acc_ref[h] * sc_ref[N_HEADS + 1 + h]
                lsum_ref[h] = lsum_ref[h] * sc_ref[N_HEADS + 1 + h]
        stage(0, slot_b, FAR)

    def pair_body(jj, carry):
        f = 2 * jj
        consume(f, slot_b, FAR)
        stage(f + 1, slot_a, FAR)
        consume(f + 1, slot_a, FAR)
        stage(f + 2, slot_b, FAR)
        return carry

    lax.fori_loop(0, jnp.maximum(i - 1, 0), pair_body, 0)

    @pl.when(i >= 1)
    def _():
        consume(2 * i - 2, slot_b, FAR)

    if bounded:
        denom = lambda h: jnp.sum(lsum_ref[h], axis=0, keepdims=True)
    else:
        denom = lambda h: acc_ref[h, V_DIM:V_DIM + 1, :]
    _attn_finalize(acc_ref, denom, lam, gsub_ref, o_ref)


def _attn_prompt(scalars, score_bound, qt, kb, vt, bias_t, gsub_row):
    b, nq, cols, tq = qt.shape
    t = kb.shape[1]
    nk = vt.shape[1]
    qbd = pltpu.VMEM((N_HEADS, HEAD_COLS, 2 * tq), BF16)
    acc = pltpu.VMEM((N_HEADS, VT_ROWS, 2 * tq), F32)
    stat = pltpu.VMEM((N_HEADS, 1, 2 * tq), F32)
    scores = pltpu.VMEM((N_HEADS, ATTN_TK, 2 * tq), F32)
    probs = pltpu.VMEM((N_HEADS, ATTN_TK, 2 * tq), BF16)
    colsum = pltpu.VMEM((N_HEADS, SUBLANES, 2 * tq), F32)

    def call(bounded, scratch, name):
        return pl.pallas_call(
            functools.partial(_attn_prompt_kernel, bounded=bounded),
            grid=(b, nq),
            in_specs=[pl.BlockSpec(memory_space=pltpu.SMEM),
                      pl.BlockSpec((1, 1, cols, tq), lambda bi, i: (bi, i, 0, 0)),
                      pl.BlockSpec((1, t, cols), lambda bi, i: (bi, 0, 0)),
                      pl.BlockSpec((1, nk, N_HEADS, VT_ROWS, ATTN_TK),
                                   lambda bi, i: (bi, 0, 0, 0, 0)),
                      _const_spec(bias_t.shape), _const_spec((1, V_DIM))],
            out_specs=pl.BlockSpec((1, tq, cols), lambda bi, i: (bi, i, 0)),
            out_shape=jax.ShapeDtypeStruct((b, t, cols), BF16),
            scratch_shapes=scratch,
            compiler_params=_cparams(("parallel", "arbitrary")),
            name=name,
        )(scalars, qt, kb, vt, bias_t, gsub_row)

    return lax.cond(
        score_bound <= MAX_UNSTABILISED_SCORE,
        lambda: call(True, [qbd, acc, probs, probs, colsum], "attn_prompt_bounded"),
        lambda: call(False, [qbd, acc, scores, scores, stat, stat, stat], "attn_prompt"))


def _attn_sample_kernel(sc_ref, q_ref, kt_ref, vc_ref, ktn_ref, vn_ref, bc_ref, bn_ref, gsub_ref,
                        o_ref, qbd_ref, m_ref, l_ref, acc_ref):
    j = pl.program_id(1)
    tq, cols = q_ref.shape[1], q_ref.shape[2]
    lam = sc_ref[N_HEADS]

    @pl.when(j == 0)
    def _():
        q = q_ref[0].astype(F32)
        col = lax.broadcasted_iota(jnp.int32, (tq, cols), 1)
        for hm in range(cols // HEAD_DIM):
            mine = (col >= hm * HEAD_DIM) & (col < (hm + 1) * HEAD_DIM)
            qbd_ref[hm * tq:(hm + 1) * tq, :] = jnp.where(mine, q, 0.0)
        m_ref[...] = jnp.full(m_ref.shape, NEG_INF, F32)
        l_ref[...] = jnp.zeros(l_ref.shape, F32)
        acc_ref[...] = jnp.zeros(acc_ref.shape, F32)

    def update(kt, v_of_head, bias):
        s = jnp.dot(qbd_ref[...].astype(BF16), kt, preferred_element_type=F32) + bias
        m_prev = m_ref[...]
        m_new = jnp.maximum(m_prev, jnp.max(s, axis=-1, keepdims=True))
        alpha = jnp.exp2(m_prev - m_new)
        p = jnp.exp2(s - m_new)
        l_ref[...] = alpha * l_ref[...] + jnp.sum(p, axis=-1, keepdims=True)
        pb = p.astype(BF16)
        for h in range(N_HEADS):
            rows = slice(h * 2 * tq, (h + 1) * 2 * tq)
            acc_ref[rows, :] = alpha[rows] * acc_ref[rows, :] + jnp.dot(
                pb[rows], v_of_head(h), preferred_element_type=F32)
        m_ref[...] = m_new

    tk = kt_ref.shape[2]
    update(kt_ref[0].astype(BF16),
           lambda h: vc_ref[0, pl.ds(h, tk, stride=N_HEADS), :].astype(BF16), bc_ref[j])

    @pl.when(j == pl.num_programs(1) - 1)
    def _():
        update(ktn_ref[0], lambda h: vn_ref[0, :, h * V_DIM:(h + 1) * V_DIM], bn_ref[...])
        for h in range(N_HEADS):
            r0 = h * 2 * tq
            inv0 = 1.0 / l_ref[r0:r0 + tq, :]
            inv1 = 1.0 / l_ref[r0 + tq:r0 + 2 * tq, :]
            o = acc_ref[r0:r0 + tq, :] * inv0 - lam * (acc_ref[r0 + tq:r0 + 2 * tq, :] * inv1)
            o_ref[0, :, h * V_DIM:(h + 1) * V_DIM] = _rms(o, gsub_ref[...]).astype(BF16)


def _attn_sample(scalars, q, kt, v4, ktn, vn, bias_c, bias_n, gsub_row):
    b, tq, cols = q.shape
    p = kt.shape[2]
    nkc = p // CACHE_TK
    rows = 2 * N_HEADS * tq
    return pl.pallas_call(
        _attn_sample_kernel,
        grid=(b, nkc),
        in_specs=[pl.BlockSpec(memory_space=pltpu.SMEM),
                  pl.BlockSpec((1, tq, cols), lambda bi, j: (bi, 0, 0)),
                  pl.BlockSpec((1, cols, CACHE_TK), lambda bi, j: (bi, 0, j)),
                  pl.BlockSpec((1, CACHE_TK * N_HEADS, V_DIM), lambda bi, j: (bi, j, 0)),
                  pl.BlockSpec((1, cols, NEW_KEY_PAD), lambda bi, j: (bi, 0, 0)),
                  pl.BlockSpec((1, NEW_KEY_PAD, cols), lambda bi, j: (bi, 0, 0)),
                  _const_spec(bias_c.shape), _const_spec(bias_n.shape), _const_spec((1, V_DIM))],
        out_specs=pl.BlockSpec((1, tq, cols), lambda bi, j: (bi, 0, 0)),
        out_shape=jax.ShapeDtypeStruct((b, tq, cols), BF16),
        scratch_shapes=[pltpu.VMEM((rows, cols), F32),
                        pltpu.VMEM((rows, 1), F32),
                        pltpu.VMEM((rows, 1), F32),
                        pltpu.VMEM((rows, V_DIM), F32)],
        compiler_params=_cparams(("parallel", "arbitrary")),
        name="attn_sample",
    )(scalars, q, kt, v4, ktn, vn, bias_c, bias_n, gsub_row)


def _mix_kernel(x_ref, o_ref, c_ref, hist_ref, wconv_ref, bconv_ref, lng_ref, lnb_ref,
                woo_ref, woc_ref, gcross_ref, wmq_ref, gmq_ref, mk_ref, mv_ref, wmo_ref,
                out_ref, ext_ref, xs_ref, cv_ref, ca_ref):
    bb, tt, d = x_ref.shape
    rows = bb * tt
    n_mem = mk_ref.shape[1] // MEM_HEADS
    rc = min(CONV_ROWS, tt)
    lead = CONV_HALO - (CONV_K - 1)

    for b in range(bb):
        ext_ref[b, :CONV_HALO, :] = hist_ref[b, 0]
        ext_ref[b, CONV_HALO:, :] = c_ref[b]
    span = xs_ref.shape[2]
    for b in range(bb):
        for r in range(1, SUBLANES):
            xs_ref[r - 1, b] = ext_ref[b, r:r + span, :]
    for b in range(bb):
        for r0 in range(0, tt, rc):
            acc = jnp.zeros((rc, c_ref.shape[2]), F32) + bconv_ref[...]
            for k in range(CONV_K):
                a, r = divmod(k + lead, SUBLANES)
                lo = r0 + SUBLANES * a
                src = ext_ref[b, lo:lo + rc, :] if r == 0 else xs_ref[r - 1, b, lo:lo + rc, :]
                acc = acc + wconv_ref[k:k + 1, :] * src
            mu = jnp.mean(acc, axis=-1, keepdims=True)
            xc = acc - mu
            var = jnp.mean(xc * xc, axis=-1, keepdims=True)
            y = xc * lax.rsqrt(var + EPS) * lng_ref[...] + lnb_ref[...]
            cv_ref[b * tt + r0:b * tt + r0 + rc, :] = (y * jax.nn.sigmoid(y)).astype(BF16)

    x = x_ref[...].reshape(rows, d)
    o = o_ref[...].reshape(rows, o_ref.shape[2])
    x1 = (x + jnp.dot(o, woo_ref[...], preferred_element_type=F32)
          + jnp.dot(cv_ref[...], woc_ref[...], preferred_element_type=F32))

    hc = _rms(x1, gcross_ref[...]).astype(BF16)
    qm = jnp.dot(hc, wmq_ref[...], preferred_element_type=F32)
    for h in range(MEM_HEADS):
        cols = slice(h * MEM_HEAD_DIM, (h + 1) * MEM_HEAD_DIM)
        qn = _rms(qm[:, cols], gmq_ref[...]).astype(BF16)
        for b in range(bb):
            mem_rows = pl.ds(h, n_mem, stride=MEM_HEADS)
            mk = mk_ref[b, mem_rows, :].astype(BF16)
            mv = mv_ref[b, mem_rows, :].astype(BF16)
            s = lax.dot_general(qn[b * tt:(b + 1) * tt], mk, (((1,), (1,)), ((), ())),
                                preferred_element_type=F32)
            p = jnp.exp2(s - jnp.max(s, axis=-1, keepdims=True))
            l = jnp.sum(p, axis=-1, keepdims=True)
            oh = jnp.dot(p.astype(BF16), mv, preferred_element_type=F32) / l
            ca_ref[b * tt:(b + 1) * tt, cols] = oh.astype(BF16)
    x2 = x1 + jnp.dot(ca_ref[...], wmo_ref[...], preferred_element_type=F32)
    out_ref[...] = x2.reshape(bb, tt, d)


def _mix(x, o, c, hist, wconv, bconv, lng, lnb, woo, woc, gcross, wmq, gmq_row, mk, mv, wmo,
         *, bb, tt):
    b, t, d = x.shape
    cw = c.shape[2]
    mem_rows, mhd = mk.shape[1], mk.shape[2]
    mw = wmq.shape[1]
    rows = bb * tt
    tile = lambda width: pl.BlockSpec((bb, tt, width), lambda bi, ti: (bi, ti, 0))
    return pl.pallas_call(
        _mix_kernel,
        grid=(b // bb, t // tt),
        in_specs=[tile(d), tile(cw), tile(cw),
                  pl.BlockSpec((bb, 1, CONV_HALO, cw), lambda bi, ti: (bi, ti, 0, 0)),
                  _const_spec(wconv.shape), _const_spec((1, cw)), _const_spec((1, cw)),
                  _const_spec((1, cw)), _const_spec(woo.shape), _const_spec(woc.shape),
                  _const_spec((1, d)), _const_spec(wmq.shape), _const_spec((1, MEM_HEAD_DIM)),
                  pl.BlockSpec((bb, mem_rows, mhd), lambda bi, ti: (bi, 0, 0)),
                  pl.BlockSpec((bb, mem_rows, mhd), lambda bi, ti: (bi, 0, 0)),
                  _const_spec(wmo.shape)],
        out_specs=tile(d),
        out_shape=jax.ShapeDtypeStruct((b, t, d), F32),
        scratch_shapes=[pltpu.VMEM((bb, CONV_HALO + tt, cw), F32),
                        pltpu.VMEM((SUBLANES - 1, bb, CONV_HALO - SUBLANES + tt, cw), F32),
                        pltpu.VMEM((rows, cw), BF16),
                        pltpu.VMEM((rows, mw), BF16)],
        compiler_params=_cparams(("parallel", "parallel")),
        name="mix",
    )(x, o, c, hist, wconv, bconv, lng, lnb, woo, woc, gcross, wmq, gmq_row, mk, mv, wmo)


def _mlp_kernel(x_ref, g_ref, w1_ref, w2_ref, out_ref):
    x = x_ref[...]
    hf = _rms(x, g_ref[...]).astype(BF16)
    dff = w1_ref.shape[1]
    part = min(dff, MLP_HIDDEN_PART)
    y = x
    for lo in range(0, dff, part):
        u = jnp.maximum(jnp.dot(hf, w1_ref[:, lo:lo + part], preferred_element_type=F32), 0.0)
        y = y + jnp.dot((u * u).astype(BF16), w2_ref[lo:lo + part, :], preferred_element_type=F32)
    out_ref[...] = y


def _mlp(x2d, g_ffn, w1, w2):
    n, d = x2d.shape
    tm = min(MLP_TILE, n)
    return pl.pallas_call(
        _mlp_kernel,
        grid=(n // tm,),
        in_specs=[pl.BlockSpec((tm, d), lambda i: (i, 0)), _const_spec((1, d)),
                  _const_spec(w1.shape), _const_spec(w2.shape)],
        out_specs=pl.BlockSpec((tm, d), lambda i: (i, 0)),
        out_shape=jax.ShapeDtypeStruct((n, d), F32),
        compiler_params=_cparams(("parallel",)),
        name="mlp",
    )(x2d, g_ffn.reshape(1, d), w1, w2)


def _rel_bucket(rel):
    half = N_BUCKETS // 2
    max_exact = half // 2
    ret = jnp.where(rel > 0, half, 0)
    n = jnp.abs(rel)
    nf = jnp.maximum(n, 1).astype(jnp.float32)
    large = max_exact + (jnp.log(nf / max_exact) / math.log(MAX_DISTANCE / max_exact)
                         * (half - max_exact)).astype(jnp.int32)
    large = jnp.minimum(large, half - 1)
    return ret + jnp.where(n < max_exact, n, large)


def _masked_bias(rel_table, q_pos, k_pos):
    bucket = _rel_bucket(k_pos[None, :] - q_pos[:, None])[None]
    table = rel_table.astype(F32) * LOG2E
    bias = jnp.zeros((rel_table.shape[1],) + bucket.shape[1:], F32)
    for bkt in range(N_BUCKETS):
        bias = jnp.where(bucket == bkt, table[bkt][:, None, None], bias)
    mask = (k_pos[None, :] // CHUNK) <= (q_pos[:, None] // CHUNK)
    return jnp.where(mask[None], bias, NEG_INF)


def _far_bucket_is_saturated(min_distance):
    half = N_BUCKETS // 2
    max_exact = half // 2
    large = max_exact + int(np.log(min_distance / max_exact) / math.log(MAX_DISTANCE / max_exact)
                            * (half - max_exact) * (1 - 1e-6))
    return large >= half - 1


def _layer(x, k_past, v_past, c_past, mk, mv, w, lam, lam_init):
    b, t, d = x.shape
    prompt = k_past is None
    cols = N_HEADS * HEAD_COLS
    gsub_row = (w["g_sub"] * (1.0 - lam_init)).reshape(1, V_DIM)

    if prompt:
        assert t % INPROJ_TILE == 0 and t % MIX_TILE == 0 and INPROJ_TILE % ATTN_TQ == 0
        assert ATTN_TQ == 2 * ATTN_TK
        assert ATTN_TK % CHUNK == 0 and _far_bucket_is_saturated(ATTN_TK + 1)
        qt, kb, vt, kt, v4, c = _inproj(x, w["g_mix"], w["w_in"], w["gq_row"], w["gk_row"],
                                        w["seg"], emit_t=True)
        k_out = jnp.transpose(kt.reshape(b, N_HEADS, 2, HEAD_DIM, t), (0, 4, 1, 2, 3))
        v_out = v4.reshape(b, t, N_HEADS, V_DIM)
        q_pos = ATTN_TQ + jnp.arange(ATTN_TQ, dtype=jnp.int32)
        k_pos = ATTN_TK + jnp.arange(3 * ATTN_TK, dtype=jnp.int32)
        near = _masked_bias(w["rel_table"], q_pos, k_pos)
        near = jnp.stack([near[:, :, n * ATTN_TK:(n + 1) * ATTN_TK] for n in range(3)], axis=1)
        near_t = jnp.swapaxes(near, 2, 3)
        halves = [near_t[..., n * ATTN_TK:(n + 1) * ATTN_TK] for n in range(ATTN_TQ // ATTN_TK)]
        bias_t = jnp.concatenate([hf for hf in halves for _ in range(2)], axis=3)
        far = w["rel_table"][_rel_bucket(jnp.int32(-(ATTN_TK + 1)))].astype(F32) * LOG2E
        scalars = jnp.concatenate([far, lam.reshape(1), jnp.exp2(-far)]).astype(F32)
        score_bound = (HEAD_DIM * jnp.max(jnp.abs(w["gq_row"])) * jnp.max(jnp.abs(w["gk_row"]))
                       + LOG2E * jnp.max(jnp.abs(w["rel_table"])))
        o = _attn_prompt(scalars, score_bound, qt, kb, vt, bias_t, gsub_row)
        bb, tt = 1, MIX_TILE
    else:
        p = k_past.shape[2]
        assert p % CACHE_TK == 0 and t <= NEW_KEY_PAD and t % 16 == 0
        qb, k, v, c = _inproj(x.reshape(1, b * t, d), w["g_mix"], w["w_in"], w["gq_row"],
                              w["gk_row"], w["seg"], emit_t=False)
        qb, k, v, c = (a.reshape(b, t, cols) for a in (qb, k, v, c))
        k_out = k.reshape(b, t, N_HEADS, 2, HEAD_DIM)
        v_out = v.reshape(b, t, N_HEADS, V_DIM)
        ktn = jnp.pad(jnp.swapaxes(k, 1, 2).astype(BF16), ((0, 0), (0, 0), (0, NEW_KEY_PAD - t)))
        vn = jnp.pad(v.astype(BF16), ((0, 0), (0, NEW_KEY_PAD - t), (0, 0)))
        q_pos = p + jnp.arange(t, dtype=jnp.int32)
        bias = _masked_bias(w["rel_table"], q_pos, jnp.arange(p + NEW_KEY_PAD, dtype=jnp.int32))
        bias = jnp.where(jnp.arange(p + NEW_KEY_PAD) < p + t, bias, NEG_INF)
        rows = 2 * N_HEADS * t
        bias = jnp.broadcast_to(bias[:, None], (N_HEADS, 2, t, p + NEW_KEY_PAD)).reshape(rows, -1)
        bias_c = jnp.swapaxes(bias[:, :p].reshape(rows, p // CACHE_TK, CACHE_TK), 0, 1)
        scalars = jnp.concatenate([jnp.zeros((N_HEADS,), F32), lam.reshape(1)]).astype(F32)
        o = _attn_sample(scalars, qb, k_past, v_past, ktn, vn, bias_c, bias[:, p:], gsub_row)
        bb, tt = SAMPLE_BATCH_TILE, t
        assert b % bb == 0

    nt = t // tt
    first = jnp.pad(c_past, ((0, 0), (CONV_HALO - (CONV_K - 1), 0), (0, 0)))[:, None]
    if nt > 1:
        tails = c.reshape(b, nt, tt, c.shape[2])[:, :-1, tt - CONV_HALO:, :]
        hist = jnp.concatenate([first, tails], axis=1)
    else:
        hist = first
    x2 = _mix(x, o, c, hist, w["w_conv"], w["b_conv"], w["ln_g"], w["ln_b"], w["w_out_o"],
              w["w_out_c"], w["g_cross"], w["w_mq"], w["gmq_row"], mk, mv, w["w_mo"], bb=bb, tt=tt)
    y = _mlp(x2.reshape(b * t, d), w["g_ffn"], w["w_ff1"], w["w_ff2"]).reshape(b, t, d)
    if t >= CONV_K - 1:
        c_hist_tail = c[:, t - (CONV_K - 1):]
    else:
        c_hist_tail = jnp.concatenate([c_past[:, t:], c], axis=1)
    return y, k_out, v_out, c_hist_tail


def kernel(x_prompt, x_sample, cache_k, cache_v, cache_conv, cache_mem_k, cache_mem_v, mem_prompt,
           rel_table, g_mix, w_in, g_q, g_k, lam_vec, g_sub, w_conv, b_conv, ln_g, ln_b, w_out,
           g_cross, g_mem, w_mq, w_mk, w_mv, g_mq, g_mk, w_mo, g_ffn, w_ff1, w_ff2):
    depth = g_mix.shape[0]
    assert depth == 1
    b, t, d = x_prompt.shape
    bs, ts, _ = x_sample.shape
    cols = N_HEADS * HEAD_COLS
    cw = w_conv.shape[2]
    attn_w = N_HEADS * V_DIM
    l = 0
    lam_init = 0.8 - 0.6 * math.exp(-0.3 * l)
    lp = lam_vec[l].astype(F32)
    lam = jnp.exp(jnp.sum(lp[0] * lp[1])) - jnp.exp(jnp.sum(lp[2] * lp[3])) + lam_init

    seg = jnp.kron(jnp.eye(cols // HEAD_DIM, dtype=F32),
                   jnp.full((HEAD_DIM, HEAD_DIM), 1.0 / HEAD_DIM, F32)).astype(BF16)
    n_maps = cols // HEAD_DIM
    w = dict(
        rel_table=rel_table, g_mix=g_mix[l], w_in=w_in[l].astype(BF16), seg=seg,
        gq_row=jnp.tile(g_q[l] * (HEAD_DIM ** -0.5 * LOG2E), n_maps).reshape(1, cols),
        gk_row=jnp.tile(g_k[l], n_maps).reshape(1, cols),
        g_sub=g_sub[l],
        w_conv=jnp.pad(w_conv[l], ((0, CONV_HALO - CONV_K), (0, 0))),
        b_conv=b_conv[l].reshape(1, cw), ln_g=ln_g[l].reshape(1, cw), ln_b=ln_b[l].reshape(1, cw),
        w_out_o=w_out[l][:attn_w].astype(BF16), w_out_c=w_out[l][attn_w:].astype(BF16),
        g_cross=g_cross[l].reshape(1, d), w_mq=w_mq[l].astype(BF16),
        gmq_row=(g_mq[l] * (MEM_HEAD_DIM ** -0.5 * LOG2E)).reshape(1, MEM_HEAD_DIM),
        w_mo=w_mo[l].astype(BF16), g_ffn=g_ffn[l],
        w_ff1=w_ff1[l].astype(BF16), w_ff2=w_ff2[l].astype(BF16),
    )

    n_mem = mem_prompt.shape[1]
    mk_p, mv_p = _memkv(mem_prompt.reshape(b * n_mem, d), g_mem[l], w_mk[l], w_mv[l], g_mk[l])
    mk_p = mk_p.reshape(b, n_mem * MEM_HEADS, MEM_HEAD_DIM)
    mv_p = mv_p.reshape(b, n_mem * MEM_HEADS, MEM_HEAD_DIM)

    zero_conv = jnp.zeros((b, CONV_K - 1, cw), F32)
    yp, kp, vp, cp = _layer(x_prompt, None, None, zero_conv, mk_p, mv_p, w, lam, lam_init)

    p = cache_k.shape[2]
    k_past = jnp.transpose(cache_k[l], (0, 2, 3, 4, 1)).reshape(bs, cols, p)
    v_past = cache_v[l].reshape(bs, p * N_HEADS, V_DIM)
    ys, kn, vn, cn = _layer(x_sample, k_past, v_past, cache_conv[l],
                            cache_mem_k[l].reshape(bs, n_mem * MEM_HEADS, MEM_HEAD_DIM),
                            cache_mem_v[l].reshape(bs, n_mem * MEM_HEADS, MEM_HEAD_DIM),
                            w, lam, lam_init)

    return (yp, ys, kp[None], vp[None], cp[None],
            mk_p.reshape(1, b, n_mem, MEM_HEADS, MEM_HEAD_DIM),
            mv_p.reshape(1, b, n_mem, MEM_HEADS, MEM_HEAD_DIM),
            kn[None], vn[None], cn[None])
```
